```python
import jax, jax.numpy as jnp
from jax import lax
import numpy as np

D_MODEL = 1024
BATCH = 4
SEQ = 4096
DEPTH = 4
DEC_BATCH = 128
DEC_SEQ = 4
PAST_LEN = 8192
PAGE_SIZE = 128

N_META = 16
N_EVEN = (DEPTH + 1) // 2
N_ODD = DEPTH // 2
CHUNK = 128
ROPE_THETA = 10000.0
A_HEADS = 8
A_NOPE = 64
A_ROPE = 32
A_V = 64
A_QLORA = 384
A_KVLORA = 256
A_LAT = A_KVLORA + A_ROPE
B_HEADS = 4
B_DK = 128
B_DV = 128
C_WIDTH = 512
C_BLOCKS = 8
C_BW = C_WIDTH // C_BLOCKS
C_CONV = 4
C_POW = 8.0
D_HEADS = 4
D_DK = 128
D_DV = 128
N_GROUPS = 4
E_PER_GROUP = 8
N_EXPERTS = N_GROUPS * E_PER_GROUP
TOP_K = 2
D_EXPERT = 128
DN_ALPHA = (2.0 * DEPTH) ** 0.25
DN_BETA = (8.0 * DEPTH) ** -0.25
EPS = 1e-5

EVEN_SPLITS = (A_QLORA, A_KVLORA, A_ROPE, B_HEADS * B_DK, B_HEADS * B_DK, B_HEADS * B_DV, B_HEADS * B_DV, B_HEADS, B_HEADS)
EVEN_IN = sum(EVEN_SPLITS)
ODD_SPLITS = (C_WIDTH, C_WIDTH, D_HEADS * D_DK, D_HEADS * D_DK, D_HEADS * D_DV, D_HEADS * D_DV)
ODD_IN = sum(ODD_SPLITS)
MIX_OUT_EVEN = A_HEADS * A_V + B_HEADS * B_DV
MIX_OUT_ODD = C_WIDTH + D_HEADS * D_DV

kernel_name = 'hybrid_mla_mlstm_rglru_retention_hmoe_step'

F32 = jnp.float32


def _cuts(sizes):
    return [int(c) for c in np.cumsum(sizes)[:-1]]


def layer_norm(x, g, b):
    xf = x.astype(F32)
    mu = xf.mean(-1, keepdims=True)
    var = jnp.square(xf - mu).mean(-1, keepdims=True)
    return ((xf - mu) * lax.rsqrt(var + EPS) * g + b).astype(x.dtype)


def rms_norm(x, g):
    xf = x.astype(F32)
    return (xf * lax.rsqrt(jnp.square(xf).mean(-1, keepdims=True) + EPS) * g).astype(x.dtype)


def head_norm(h, g):
    hf = h.astype(F32)
    mu = hf.mean(-1, keepdims=True)
    var = jnp.square(hf - mu).mean(-1, keepdims=True)
    return ((hf - mu) * lax.rsqrt(var + EPS) * g.reshape(h.shape[-2:])).astype(h.dtype)


def rope(x, pos):
    half = x.shape[-1] // 2
    freqs = ROPE_THETA ** (-jnp.arange(half, dtype=F32) / half)
    ang = pos.astype(F32)[:, None] * freqs
    ang = ang.reshape((ang.shape[0],) + (1,) * (x.ndim - 3) + (half,))
    cos, sin = jnp.cos(ang), jnp.sin(ang)
    x1, x2 = x[..., :half].astype(F32), x[..., half:].astype(F32)
    return jnp.concatenate([x1 * cos - x2 * sin, x1 * sin + x2 * cos], -1).astype(x.dtype)


def run_chunked(chunk_fn, state, xs, lead):
    outs = []
    if lead > 0:
        state, y = chunk_fn(state, tuple(a[:, :lead] for a in xs))
        outs.append(y)
        xs = tuple(a[:, lead:] for a in xs)
    rest = xs[0].shape[1]
    n_full, rem = divmod(rest, CHUNK)
    if n_full > 0:
        body = tuple(jnp.moveaxis(a[:, :n_full * CHUNK].reshape((a.shape[0], n_full, CHUNK) + a.shape[2:]), 1, 0) for a in xs)
        state, ys = lax.scan(chunk_fn, state, body)
        ys = jnp.moveaxis(ys, 0, 1)
        outs.append(ys.reshape((ys.shape[0], n_full * CHUNK) + ys.shape[3:]))
    if rem > 0:
        state, y = chunk_fn(state, tuple(a[:, n_full * CHUNK:] for a in xs))
        outs.append(y)
    return state, (jnp.concatenate(outs, axis=1) if len(outs) > 1 else outs[0])


def mlstm_chunk(state, xs):
    C, n, m = state
    q, k, v, ig, lf = xs
    out_dtype = v.dtype
    q, k, v = q.astype(F32), k.astype(F32), v.astype(F32)
    Cf, nf, mf = C.astype(F32), n.astype(F32), m.astype(F32)
    ig = jnp.swapaxes(ig.astype(F32), 1, 2)
    b = jnp.cumsum(jnp.swapaxes(lf.astype(F32), 1, 2), axis=-1)
    L = q.shape[1]
    causal = jnp.tril(jnp.ones((L, L), bool))
    dmat = jnp.where(causal, b[..., :, None] - b[..., None, :] + ig[..., None, :], -jnp.inf)
    inter = b + mf[..., None]
    m_t = jnp.maximum(inter, dmat.max(-1))
    w_inter = jnp.exp(inter - m_t)
    s = jnp.einsum('bthk,bshk->bhts', q, k) * jnp.exp(dmat - m_t[..., None])
    num = w_inter[..., None] * jnp.einsum('bthk,bhkv->bhtv', q, Cf) + jnp.einsum('bhts,bshv->bhtv', s, v)
    den = w_inter * jnp.einsum('bthk,bhk->bht', q, nf) + s.sum(-1)
    h = num / jnp.maximum(jnp.abs(den), jnp.exp(-m_t))[..., None]
    m_new = m_t[..., -1]
    w_end = jnp.exp(b[..., -1:] - b + ig - m_new[..., None])
    decay = jnp.exp(b[..., -1] + mf - m_new)
    C_new = decay[..., None, None] * Cf + jnp.einsum('bhs,bshk,bshv->bhkv', w_end, k, v)
    n_new = decay[..., None] * nf + jnp.einsum('bhs,bshk->bhk', w_end, k)
    y = jnp.swapaxes(h, 1, 2).astype(out_dtype)
    return (C_new.astype(C.dtype), n_new.astype(n.dtype), m_new.astype(m.dtype)), y


def retention_log_gamma():
    return jnp.log1p(-(2.0 ** (-5.0 - jnp.arange(D_HEADS, dtype=F32))))


def retention_chunk(S, xs):
    q, k, v = xs
    out_dtype = v.dtype
    q, k, v = q.astype(F32), k.astype(F32), v.astype(F32)
    Sf = S.astype(F32)
    L = q.shape[1]
    lg = retention_log_gamma()
    idx = jnp.arange(L, dtype=F32)
    rel = idx[:, None] - idx[None, :]
    dec = jnp.where(rel >= 0, jnp.exp(lg[:, None, None] * jnp.maximum(rel, 0.0)), 0.0)
    inner = jnp.einsum('bthk,bshk->bhts', q, k) * dec
    o = jnp.einsum('bhts,bshv->bthv', inner, v)
    o = o + jnp.exp((idx[:, None] + 1.0) * lg[None, :])[..., None] * jnp.einsum('bthk,bhkv->bthv', q, Sf)
    w_end = jnp.exp((L - 1.0 - idx)[:, None] * lg[None, :])
    S_new = jnp.exp(L * lg)[:, None, None] * Sf + jnp.einsum('bshk,bshv,sh->bhkv', k, v, w_end)
    return S_new.astype(S.dtype), o.astype(out_dtype)


def causal_conv(xb, buf, w, bias):
    xp = jnp.concatenate([buf.astype(xb.dtype), xb], axis=1)
    L = xb.shape[1]
    out = sum(xp[:, j:j + L] * w[j] for j in range(C_CONV)) + bias
    return out, xp[:, -(C_CONV - 1):]


def rglru(u, h0, w_a, b_a, w_x, b_x, lam):
    Bx, L, W = u.shape
    ub = u.reshape(Bx, L, C_BLOCKS, C_BW)
    r = jax.nn.sigmoid(jnp.einsum('blnc,ncd->blnd', ub, w_a).reshape(Bx, L, W) + b_a)
    i = jax.nn.sigmoid(jnp.einsum('blnc,ncd->blnd', ub, w_x).reshape(Bx, L, W) + b_x)
    log_a = (-C_POW * r * jax.nn.softplus(-lam)).astype(F32)
    a = jnp.exp(log_a)
    xin = jnp.sqrt(-jnp.expm1(2.0 * log_a)) * (i * u).astype(F32)

    def comb(e1, e2):
        return (e1[0] * e2[0], e2[0] * e1[1] + e2[1])

    a_cum, b_cum = lax.associative_scan(comb, (a, xin), axis=1)
    h = a_cum * h0.astype(F32)[:, None] + b_cum
    return h.astype(u.dtype), h[:, -1].astype(h0.dtype)


def mla_prompt_attn(qc, kv):
    scale = (A_NOPE + A_ROPE) ** -0.5
    Bx, Lt = kv.shape[:2]
    kpos = jnp.arange(Lt)

    def attend(q_blk, q_pos, k_rows, k_pos):
        s = jnp.einsum('bthk,bsk->bhts', q_blk, k_rows).astype(F32) * scale
        s = jnp.where(k_pos[None, :] <= q_pos[:, None], s, -jnp.inf)
        p = jax.nn.softmax(s, axis=-1).astype(k_rows.dtype)
        return jnp.einsum('bhts,bsk->bthk', p, k_rows)[..., :A_KVLORA]

    o_meta = attend(qc[:, :N_META], kpos[:N_META], kv[:, :N_META], kpos[:N_META])
    n_blk = (Lt - N_META) // CHUNK
    qb = jnp.moveaxis(qc[:, N_META:].reshape(Bx, n_blk, CHUNK, A_HEADS, A_LAT), 1, 0)
    qposb = (N_META + jnp.arange(n_blk * CHUNK)).reshape(n_blk, CHUNK)
    o_blk = lax.map(lambda a: attend(a[0], a[1], kv, kpos), (qb, qposb))
    o_blk = jnp.moveaxis(o_blk, 0, 1).reshape(Bx, n_blk * CHUNK, A_HEADS, A_KVLORA)
    return jnp.concatenate([o_meta, o_blk], axis=1)


def mla_sample_attn(qc, kv_new, kv_past):
    scale = (A_NOPE + A_ROPE) ** -0.5
    T = qc.shape[1]
    P = kv_past.shape[1]
    s_past = jnp.einsum('bthk,bsk->bhts', qc, kv_past).astype(F32)
    s_new = jnp.einsum('bthk,bsk->bhts', qc, kv_new).astype(F32)
    s_new = jnp.where(jnp.tril(jnp.ones((T, T), bool)), s_new, -jnp.inf)
    p = jax.nn.softmax(jnp.concatenate([s_past, s_new], -1) * scale, axis=-1).astype(kv_new.dtype)
    o = jnp.einsum('bhts,bsk->bthk', p[..., :P], kv_past) + jnp.einsum('bhts,bsk->bthk', p[..., P:], kv_new)
    return o[..., :A_KVLORA]


def even_mixer(x, pos, lead, kv_past, mstate, W, j):
    Bx, L, _ = x.shape
    c_q, c_kv, k_r, mq, mk, mv, mo, mi, mf = jnp.split(x @ W['w_in_e'][j], _cuts(EVEN_SPLITS), axis=-1)
    c_q = rms_norm(c_q, W['g_cq'][j])
    c_kv = rms_norm(c_kv, W['g_ckv'][j])
    q = (c_q @ W['w_uq'][j]).reshape(Bx, L, A_HEADS, A_NOPE + A_ROPE)
    q_lat = jnp.einsum('blhn,khn->blhk', q[..., :A_NOPE], W['w_uk'][j])
    qc = jnp.concatenate([q_lat, rope(q[..., A_NOPE:], pos)], -1)
    kv_new = jnp.concatenate([c_kv, rope(k_r, pos)], -1)
    if kv_past is None:
        o_lat = mla_prompt_attn(qc, kv_new)
    else:
        o_lat = mla_sample_attn(qc, kv_new, kv_past)
    o_a = jnp.einsum('blhk,khv->blhv', o_lat, W['w_uv'][j]).reshape(Bx, L, A_HEADS * A_V)
    hs = lambda t, d: t.reshape(Bx, L, B_HEADS, d)
    ig = (mi + W['b_mi'][j]).astype(F32)
    lf = jax.nn.log_sigmoid((mf + W['b_mf'][j]).astype(F32))
    mstate, h = run_chunked(mlstm_chunk, mstate, (hs(mq, B_DK), hs(mk, B_DK) * (B_DK ** -0.5), hs(mv, B_DV), ig, lf), lead)
    o_b = (head_norm(h, W['g_mlstm'][j]) * jax.nn.sigmoid(hs(mo, B_DV))).reshape(Bx, L, B_HEADS * B_DV)
    return jnp.concatenate([o_a, o_b], -1) @ W['w_out_e'][j], kv_new, mstate


def odd_mixer(x, pos, lead, ostate, W, j):
    h0, buf, S0 = ostate
    Bx, L, _ = x.shape
    xr, gr, rq, rk, rv, rg = jnp.split(x @ W['w_in_o'][j], _cuts(ODD_SPLITS), axis=-1)
    u, buf_new = causal_conv(xr, buf, W['conv_w'][j], W['conv_b'][j])
    hc, h_last = rglru(u, h0, W['w_ra'][j], W['b_ra'][j], W['w_ri'][j], W['b_ri'][j], W['lam'][j])
    o_c = hc * jax.nn.gelu(gr)
    hs = lambda t, d: t.reshape(Bx, L, D_HEADS, d)
    S_new, o = run_chunked(retention_chunk, S0, (rope(hs(rq, D_DK), pos), rope(hs(rk, D_DK), pos) * (D_DK ** -0.5), hs(rv, D_DV)), lead)
    o_d = (head_norm(o, W['g_ret'][j]) * jax.nn.silu(hs(rg, D_DV))).reshape(Bx, L, D_HEADS * D_DV)
    return jnp.concatenate([o_c, o_d], -1) @ W['w_out_o'][j], (h_last, buf_new.astype(buf.dtype), S_new)


def hier_moe(x, W, l):
    shp = x.shape
    t = x.reshape(-1, shp[-1])
    g_prob = jax.nn.softmax((t @ W['w_gr'][l] + W['b_gr'][l]).astype(F32), -1)
    g_idx = jnp.argmax(g_prob, -1)
    g_gate = jnp.max(g_prob, -1)
    e_logits = jnp.einsum('td,gde->tge', t, W['w_er'][l]) + W['b_er'][l]
    e_logits = jnp.take_along_axis(e_logits, g_idx[:, None, None], axis=1)[:, 0]
    e_prob = jax.nn.softmax(e_logits.astype(F32), -1)
    top_w, top_i = lax.top_k(e_prob, TOP_K)
    top_w = top_w / top_w.sum(-1, keepdims=True)
    e_gate = jnp.einsum('tk,tke->te', top_w, jax.nn.one_hot(top_i, E_PER_GROUP, dtype=F32))
    gate = (g_gate[:, None, None] * jax.nn.one_hot(g_idx, N_GROUPS, dtype=F32)[:, :, None] * e_gate[:, None, :])
    gate = gate.reshape(-1, N_EXPERTS).astype(x.dtype)
    h = jax.nn.silu(jnp.einsum('td,edf->tef', t, W['w_e1'][l])) * jnp.einsum('td,edf->tef', t, W['w_e3'][l])
    y = jnp.einsum('tef,efd->td', h * gate[..., None], W['w_e2'][l])
    return y.reshape(shp)


def run_trunk(x, pos, lead, cache_mla, page_table, st_c, st_n, st_m, st_h, st_conv, st_s, W):
    kv_rows, c_o, n_o, m_o, h_o, cv_o, s_o = [], [], [], [], [], [], []
    for l in range(DEPTH):
        j = l // 2
        if l % 2 == 0:
            kv_past = None
            if cache_mla is not None:
                pages = cache_mla[j, page_table]
                kv_past = pages.reshape(pages.shape[0], -1, A_LAT)
            mix, kv_new, (c, n, m) = even_mixer(x, pos, lead, kv_past, (st_c[j], st_n[j], st_m[j]), W, j)
            kv_rows.append(kv_new); c_o.append(c); n_o.append(n); m_o.append(m)
        else:
            mix, (h, cv, s) = odd_mixer(x, pos, lead, (st_h[j], st_conv[j], st_s[j]), W, j)
            h_o.append(h); cv_o.append(cv); s_o.append(s)
        x = layer_norm(DN_ALPHA * x + mix, W['ln_mix_g'][l], W['ln_mix_b'][l])
        x = layer_norm(DN_ALPHA * x + hier_moe(x, W, l), W['ln_ffn_g'][l], W['ln_ffn_b'][l])
    return (x, jnp.stack(kv_rows), jnp.stack(c_o), jnp.stack(n_o), jnp.stack(m_o),
            jnp.stack(h_o), jnp.stack(cv_o), jnp.stack(s_o))


def setup_inputs(seed: int = 0) -> dict:
    key = jax.random.key(seed)
    keys = iter(jax.random.split(key, 64))
    n_pages = PAST_LEN // PAGE_SIZE
    n_used = DEC_BATCH * n_pages
    n_pool = n_used + max(1, n_used // 4)

    def nrm(shape, scale=1.0):
        return jax.random.normal(next(keys), shape, F32) * scale

    def gain(shape):
        return 1.0 + 0.02 * jax.random.normal(next(keys), shape, F32)

    page_table = jax.random.permutation(next(keys), n_pool)[:n_used].reshape(DEC_BATCH, n_pages).astype(jnp.int32)
    a_c = jax.random.uniform(next(keys), (N_ODD, C_WIDTH), F32, 0.9, 0.999)
    a_base = a_c ** (1.0 / C_POW)
    lam = jnp.log(a_base) - jnp.log1p(-a_base)
    b_mf = jnp.linspace(3.0, 6.0, B_HEADS, dtype=F32)[None, :] + nrm((N_EVEN, B_HEADS), 0.1)
    return {
        'x_prompt': nrm((BATCH, SEQ, D_MODEL)),
        'x_sample': nrm((DEC_BATCH, DEC_SEQ, D_MODEL)),
        'cache_mla': nrm((N_EVEN, n_pool, PAGE_SIZE, A_LAT)),
        'page_table': page_table,
        'state_mlstm_c': nrm((N_EVEN, DEC_BATCH, B_HEADS, B_DK, B_DV), 0.1),
        'state_mlstm_n': nrm((N_EVEN, DEC_BATCH, B_HEADS, B_DK), 0.1),
        'state_mlstm_m': nrm((N_EVEN, DEC_BATCH, B_HEADS), 0.5),
        'state_rglru_h': nrm((N_ODD, DEC_BATCH, C_WIDTH), 0.5),
        'state_rglru_conv': nrm((N_ODD, DEC_BATCH, C_CONV - 1, C_WIDTH)),
        'state_ret': nrm((N_ODD, DEC_BATCH, D_HEADS, D_DK, D_DV), 0.1),
        'meta_tokens': nrm((N_META, D_MODEL)),
        'w_in_e': nrm((N_EVEN, D_MODEL, EVEN_IN), D_MODEL ** -0.5),
        'g_cq': gain((N_EVEN, A_QLORA)),
        'g_ckv': gain((N_EVEN, A_KVLORA)),
        'w_uq': nrm((N_EVEN, A_QLORA, A_HEADS * (A_NOPE + A_ROPE)), A_QLORA ** -0.5),
        'w_uk': nrm((N_EVEN, A_KVLORA, A_HEADS, A_NOPE), A_KVLORA ** -0.5),
        'w_uv': nrm((N_EVEN, A_KVLORA, A_HEADS, A_V), A_KVLORA ** -0.5),
        'b_mi': nrm((N_EVEN, B_HEADS), 0.1),
        'b_mf': b_mf,
        'g_mlstm': gain((N_EVEN, B_HEADS * B_DV)),
        'w_out_e': nrm((N_EVEN, MIX_OUT_EVEN, D_MODEL), DN_BETA * MIX_OUT_EVEN ** -0.5),
        'w_in_o': nrm((N_ODD, D_MODEL, ODD_IN), D_MODEL ** -0.5),
        'conv_w': nrm((N_ODD, C_CONV, C_WIDTH), C_CONV ** -0.5),
        'conv_b': nrm((N_ODD, C_WIDTH), 0.02),
        'w_ra': nrm((N_ODD, C_BLOCKS, C_BW, C_BW), C_BW ** -0.5),
        'b_ra': nrm((N_ODD, C_WIDTH), 0.02),
        'w_ri': nrm((N_ODD, C_BLOCKS, C_BW, C_BW), C_BW ** -0.5),
        'b_ri': nrm((N_ODD, C_WIDTH), 0.02),
        'lam': lam,
        'g_ret': gain((N_ODD, D_HEADS * D_DV)),
        'w_out_o': nrm((N_ODD, MIX_OUT_ODD, D_MODEL), DN_BETA * MIX_OUT_ODD ** -0.5),
        'ln_mix_g': gain((DEPTH, D_MODEL)),
        'ln_mix_b': nrm((DEPTH, D_MODEL), 0.02),
        'ln_ffn_g': gain((DEPTH, D_MODEL)),
        'ln_ffn_b': nrm((DEPTH, D_MODEL), 0.02),
        'w_gr': nrm((DEPTH, D_MODEL, N_GROUPS), D_MODEL ** -0.5),
        'b_gr': nrm((DEPTH, N_GROUPS), 0.01),
        'w_er': nrm((DEPTH, N_GROUPS, D_MODEL, E_PER_GROUP), D_MODEL ** -0.5),
        'b_er': nrm((DEPTH, N_GROUPS, E_PER_GROUP), 0.01),
        'w_e1': nrm((DEPTH, N_EXPERTS, D_MODEL, D_EXPERT), D_MODEL ** -0.5),
        'w_e3': nrm((DEPTH, N_EXPERTS, D_MODEL, D_EXPERT), D_MODEL ** -0.5),
        'w_e2': nrm((DEPTH, N_EXPERTS, D_EXPERT, D_MODEL), DN_BETA * D_EXPERT ** -0.5),
    }


def reference(x_prompt, x_sample, cache_mla, page_table, state_mlstm_c, state_mlstm_n, state_mlstm_m,
              state_rglru_h, state_rglru_conv, state_ret, meta_tokens, w_in_e, g_cq, g_ckv, w_uq, w_uk, w_uv,
              b_mi, b_mf, g_mlstm, w_out_e, w_in_o, conv_w, conv_b, w_ra, b_ra, w_ri, b_ri, lam, g_ret, w_out_o,
              ln_mix_g, ln_mix_b, ln_ffn_g, ln_ffn_b, w_gr, b_gr, w_er, b_er, w_e1, w_e3, w_e2):
    W = dict(w_in_e=w_in_e, g_cq=g_cq, g_ckv=g_ckv, w_uq=w_uq, w_uk=w_uk, w_uv=w_uv, b_mi=b_mi, b_mf=b_mf,
             g_mlstm=g_mlstm, w_out_e=w_out_e, w_in_o=w_in_o, conv_w=conv_w, conv_b=conv_b, w_ra=w_ra, b_ra=b_ra,
             w_ri=w_ri, b_ri=b_ri, lam=lam, g_ret=g_ret, w_out_o=w_out_o, ln_mix_g=ln_mix_g, ln_mix_b=ln_mix_b,
             ln_ffn_g=ln_ffn_g, ln_ffn_b=ln_ffn_b, w_gr=w_gr, b_gr=b_gr, w_er=w_er, b_er=b_er,
             w_e1=w_e1, w_e3=w_e3, w_e2=w_e2)
    Bp = x_prompt.shape[0]
    dt = x_prompt.dtype
    meta = jnp.broadcast_to(meta_tokens[None].astype(dt), (Bp, N_META, x_prompt.shape[-1]))
    xp = jnp.concatenate([meta, x_prompt], axis=1)
    pos_p = jnp.arange(xp.shape[1])
    (yp, kv_p, c_p, n_p, m_p, h_p, cv_p, s_p) = run_trunk(
        xp, pos_p, N_META, None, None,
        jnp.zeros((N_EVEN, Bp, B_HEADS, B_DK, B_DV), dt), jnp.zeros((N_EVEN, Bp, B_HEADS, B_DK), dt),
        jnp.zeros((N_EVEN, Bp, B_HEADS), dt), jnp.zeros((N_ODD, Bp, C_WIDTH), dt),
        jnp.zeros((N_ODD, Bp, C_CONV - 1, C_WIDTH), dt), jnp.zeros((N_ODD, Bp, D_HEADS, D_DK, D_DV), dt), W)
    y_prompt = yp[:, N_META:]
    past_len = page_table.shape[1] * PAGE_SIZE
    pos_s = past_len + jnp.arange(x_sample.shape[1])
    (y_sample, kv_s, c_s, n_s, m_s, h_s, cv_s, s_s) = run_trunk(
        x_sample, pos_s, 0, cache_mla, page_table, state_mlstm_c, state_mlstm_n, state_mlstm_m,
        state_rglru_h, state_rglru_conv, state_ret, W)
    return (y_prompt, y_sample, kv_p, kv_s, c_p, c_s, n_p, n_s, m_p, m_s, h_p, h_s, cv_p, cv_s, s_p, s_s)
```

```python
import functools
import math

import jax
import jax.numpy as jnp
from jax import lax
from jax.experimental import pallas as pl
from jax.experimental.pallas import tpu as pltpu

F32 = jnp.float32
BF16 = jnp.bfloat16

D_MODEL = 1024
BATCH = 4
SEQ = 4096
DEPTH = 4
DEC_BATCH = 128
DEC_SEQ = 4
PAST_LEN = 8192
PAGE_SIZE = 128
N_PAGES = PAST_LEN // PAGE_SIZE
N_META = 16
N_EVEN = (DEPTH + 1) // 2
N_ODD = DEPTH // 2
CHUNK = 128
ROPE_THETA = 10000.0
A_HEADS = 8
A_NOPE = 64
A_ROPE = 32
A_V = 64
A_QLORA = 384
A_KVLORA = 256
A_LAT = A_KVLORA + A_ROPE
B_HEADS = 4
B_DK = 128
B_DV = 128
C_WIDTH = 512
C_BLOCKS = 8
C_BW = C_WIDTH // C_BLOCKS
C_CONV = 4
C_POW = 8.0
D_HEADS = 4
D_DK = 128
D_DV = 128
N_GROUPS = 4
E_PER_GROUP = 8
N_EXPERTS = N_GROUPS * E_PER_GROUP
D_EXPERT = 128
DN_ALPHA = (2.0 * DEPTH) ** 0.25
EPS = 1e-5

LANES = 128
SUBLANES = 8
VMEM_LIMIT = 48 * 1024 * 1024

S_ROWS = 8
N_MAIN = BATCH * SEQ
ROW_S = N_MAIN
N_SAMP = DEC_BATCH * S_ROWS
ROW_M = ROW_S + N_SAMP
TM = 512
N_TOK = -(-(ROW_M + N_META) // TM) * TM

HEAD_PAD = LANES
ATT_W = A_HEADS * HEAD_PAD
ROPE_LANE0 = A_NOPE
PROJ_E_W = 3072
GATE_COL = 640
MI_LANE = ROPE_LANE0 + A_ROPE
MF_LANE = MI_LANE + B_HEADS

TQ = 256


def _cparams(sem):
    return pltpu.CompilerParams(dimension_semantics=sem, vmem_limit_bytes=VMEM_LIMIT)


def _bdot(a, b):
    return jnp.dot(a.astype(BF16), b.astype(BF16), preferred_element_type=F32)


def _bdot_nt(a, b):
    return lax.dot_general(a.astype(BF16), b.astype(BF16), (((1,), (1,)), ((), ())),
                           preferred_element_type=F32)


def _bdot_tn(a, b):
    return lax.dot_general(a.astype(BF16), b.astype(BF16), (((0,), (0,)), ((), ())),
                           preferred_element_type=F32)


def _sigmoid(x):
    return 1.0 / (1.0 + jnp.exp(-x))


def _layer_norm(x, g, b):
    mu = jnp.mean(x, axis=-1, keepdims=True)
    xc = x - mu
    var = jnp.mean(xc * xc, axis=-1, keepdims=True)
    return xc * lax.rsqrt(var + EPS) * g + b


def _head_norm(x, g):
    mu = jnp.mean(x, axis=-1, keepdims=True)
    xc = x - mu
    var = jnp.mean(xc * xc, axis=-1, keepdims=True)
    return xc * lax.rsqrt(var + EPS) * g


def _mm_kernel(x_ref, w_ref, o_ref):
    o_ref[...] = _bdot(x_ref[...], w_ref[...])


def _matmul(x, w):
    n, k = x.shape
    m = w.shape[1]
    return pl.pallas_call(
        _mm_kernel,
        grid=(n // TM,),
        in_specs=[pl.BlockSpec((TM, k), lambda i: (i, 0)),
                  pl.BlockSpec((k, m), lambda i: (0, 0))],
        out_specs=pl.BlockSpec((TM, m), lambda i: (i, 0)),
        out_shape=jax.ShapeDtypeStruct((n, m), F32),
        compiler_params=_cparams(("parallel",)),
        name="proj_in",
    )(x, w)


def _rope_tile(x, c, s1, s2):
    half = A_ROPE // 2
    return x * c + pltpu.roll(x, half, 1) * s1 + pltpu.roll(x, LANES - half, 1) * s2


def _mla_prep_kernel(a_ref, c_ref, s1_ref, s2_ref, gq_ref, gkv_ref, wuq_ref, wk_ref, wv_ref,
                     q_ref, k_ref, v_ref, kv_ref):
    a = a_ref[...]
    c, s1, s2 = c_ref[...], s1_ref[...], s2_ref[...]
    cq = a[:, :A_QLORA]
    ckv = a[:, A_QLORA:A_QLORA + A_KVLORA]
    cq = cq * lax.rsqrt(jnp.mean(cq * cq, axis=-1, keepdims=True) + EPS) * gq_ref[...]
    ckv = ckv * lax.rsqrt(jnp.mean(ckv * ckv, axis=-1, keepdims=True) + EPS) * gkv_ref[...]
    kr = _rope_tile(a[:, GATE_COL:GATE_COL + LANES], c, s1, s2)
    kv_ref[:, :A_KVLORA] = ckv
    kv_ref[:, A_KVLORA:] = kr
    q = _bdot(cq, wuq_ref[...])
    kn = _bdot(ckv, wk_ref[...])
    v_ref[...] = _bdot(ckv, wv_ref[...]).astype(BF16)
    for h in range(A_HEADS):
        sl = slice(h * HEAD_PAD, (h + 1) * HEAD_PAD)
        q_ref[:, sl] = _rope_tile(q[:, sl], c, s1, s2)
        k_ref[:, sl] = (kn[:, sl] + kr).astype(BF16)


def _mla_prep(proj, tabs, gq, gkv, wuq, wk, wv):
    n = proj.shape[0]
    row = lambda w: pl.BlockSpec((TM, w), lambda i: (i, 0))
    full = lambda a: pl.BlockSpec(a.shape, lambda i: (0,) * a.ndim)
    return pl.pallas_call(
        _mla_prep_kernel,
        grid=(n // TM,),
        in_specs=[row(GATE_COL + LANES), row(LANES), row(LANES), row(LANES),
                  full(gq), full(gkv), full(wuq), full(wk), full(wv)],
        out_specs=[row(ATT_W), row(ATT_W), row(ATT_W), row(A_KVLORA + LANES)],
        out_shape=[jax.ShapeDtypeStruct((n, ATT_W), F32),
                   jax.ShapeDtypeStruct((n, ATT_W), BF16),
                   jax.ShapeDtypeStruct((n, ATT_W), BF16),
                   jax.ShapeDtypeStruct((n, A_KVLORA + LANES), F32)],
        compiler_params=_cparams(("parallel",)),
        name="mla_prep",
    )(proj, *tabs, gq, gkv, wuq, wk, wv)


ATT_SCALE = (A_NOPE + A_ROPE) ** -0.5


def _flash_kernel(q_ref, k_ref, v_ref, km_ref, vm_ref, o_ref, qs_ref, m_ref, l_ref, acc_ref):
    qi = pl.program_id(1)
    ki = pl.program_id(2)

    @pl.when(ki == 0)
    def _init():
        qs_ref[...] = q_ref[...].astype(BF16)
        for h in range(A_HEADS):
            sl = slice(h * HEAD_PAD, (h + 1) * HEAD_PAD)
            s = _bdot_nt(qs_ref[:, sl], km_ref[:, sl]) * ATT_SCALE
            m = jnp.max(s, axis=1, keepdims=True)
            p = jnp.exp(s - m)
            m_ref[h] = m
            l_ref[h] = jnp.sum(p, axis=1, keepdims=True)
            acc_ref[:, sl] = _bdot(p, vm_ref[:, sl])

    @pl.when(ki <= qi)
    def _step():
        row = lax.broadcasted_iota(jnp.int32, (TQ, TQ), 0)
        col = lax.broadcasted_iota(jnp.int32, (TQ, TQ), 1)
        visible = col <= row + (qi - ki) * TQ
        for h in range(A_HEADS):
            sl = slice(h * HEAD_PAD, (h + 1) * HEAD_PAD)
            s = _bdot_nt(qs_ref[:, sl], k_ref[:, sl]) * ATT_SCALE
            s = jnp.where(visible, s, -jnp.inf)
            m_prev = m_ref[h]
            m_new = jnp.maximum(m_prev, jnp.max(s, axis=1, keepdims=True))
            alpha = jnp.exp(m_prev - m_new)
            p = jnp.exp(s - m_new)
            l_ref[h] = alpha * l_ref[h] + jnp.sum(p, axis=1, keepdims=True)
            acc_ref[:, sl] = alpha * acc_ref[:, sl] + _bdot(p, v_ref[:, sl])
            m_ref[h] = m_new

    @pl.when(ki == pl.num_programs(2) - 1)
    def _fin():
        for h in range(A_HEADS):
            sl = slice(h * HEAD_PAD, (h + 1) * HEAD_PAD)
            o_ref[:, sl] = acc_ref[:, sl] / l_ref[h]


def _flash(q, k, v, nb, seq, meta_row):
    nq = seq // TQ
    meta_blk = meta_row // N_META
    qmap = lambda b, i, j: (b * nq + i, 0)
    kmap = lambda b, i, j: (b * nq + jnp.minimum(i, j), 0)
    mmap = lambda b, i, j: (meta_blk, 0)
    return pl.pallas_call(
        _flash_kernel,
        grid=(nb, nq, nq),
        in_specs=[pl.BlockSpec((TQ, ATT_W), qmap),
                  pl.BlockSpec((TQ, ATT_W), kmap),
                  pl.BlockSpec((TQ, ATT_W), kmap),
                  pl.BlockSpec((N_META, ATT_W), mmap),
                  pl.BlockSpec((N_META, ATT_W), mmap)],
        out_specs=pl.BlockSpec((TQ, ATT_W), qmap),
        out_shape=jax.ShapeDtypeStruct((nb * seq, ATT_W), F32),
        scratch_shapes=[pltpu.VMEM((TQ, ATT_W), BF16),
                        pltpu.VMEM((A_HEADS, TQ, 1), F32),
                        pltpu.VMEM((A_HEADS, TQ, 1), F32),
                        pltpu.VMEM((TQ, ATT_W), F32)],
        compiler_params=_cparams(("parallel", "parallel", "arbitrary")),
        name="mla_flash",
    )(q, k, v, k, v)


def _meta_attn_kernel(q_ref, k_ref, v_ref, o_ref):
    n = q_ref.shape[0]
    row = lax.broadcasted_iota(jnp.int32, (n, n), 0)
    col = lax.broadcasted_iota(jnp.int32, (n, n), 1)
    for h in range(A_HEADS):
        sl = slice(h * HEAD_PAD, (h + 1) * HEAD_PAD)
        s = _bdot_nt(q_ref[:, sl], k_ref[:, sl]) * ATT_SCALE
        s = jnp.where(col <= row, s, -jnp.inf)
        p = jnp.exp(s - jnp.max(s, axis=1, keepdims=True))
        o_ref[:, sl] = _bdot(p, v_ref[:, sl]) / jnp.sum(p, axis=1, keepdims=True)


def _meta_attn(q, k, v, meta_row):
    blk = meta_row // N_META
    spec = pl.BlockSpec((N_META, ATT_W), lambda i: (blk, 0))
    return pl.pallas_call(
        _meta_attn_kernel,
        grid=(1,),
        in_specs=[spec, spec, spec],
        out_specs=pl.BlockSpec((N_META, ATT_W), lambda i: (0, 0)),
        out_shape=jax.ShapeDtypeStruct((N_META, ATT_W), F32),
        compiler_params=_cparams(("arbitrary",)),
        name="mla_meta_attn",
    )(q, k, v)


PAGES_PER_STEP = 8
KV_ROPE0 = A_KVLORA + ROPE_LANE0


def _sattn_kernel(pt_ref, q_ref, kvn_ref, wuk_ref, wuv_ref, cache_ref, o_ref, buf_ref, sem_ref, *, layer):
    b = pl.program_id(0)
    nb = pl.num_programs(0)

    def page_copy(seq, p, slot):
        return pltpu.make_async_copy(cache_ref.at[layer, pt_ref[seq, p]], buf_ref.at[slot, p],
                                     sem_ref.at[slot])

    def start(seq, slot):
        def body(p, carry):
            page_copy(seq, p, slot).start()
            return carry
        lax.fori_loop(0, N_PAGES, body, 0)

    def wait(seq, slot):
        def body(p, carry):
            page_copy(seq, p, slot).wait()
            return carry
        lax.fori_loop(0, N_PAGES, body, 0)

    @pl.when(b == 0)
    def _first():
        start(0, 0)

    @pl.when(b + 1 < nb)
    def _next():
        start(b + 1, (b + 1) % 2)

    slot = b % 2
    wait(b, slot)

    q = q_ref[...]
    ql, qr = [], []
    for h in range(A_HEADS):
        ql.append(_bdot(q[:, h * HEAD_PAD:h * HEAD_PAD + A_NOPE], wuk_ref[h]))
        qr.append(q[:, h * HEAD_PAD + ROPE_LANE0:h * HEAD_PAD + ROPE_LANE0 + A_ROPE])
    ql = jnp.concatenate(ql, axis=0).astype(BF16)
    qr = jnp.concatenate(qr, axis=0).astype(BF16)
    rows = A_HEADS * S_ROWS
    keys = PAGES_PER_STEP * PAGE_SIZE

    def chunk(c, carry):
        m_prev, l_prev, acc = carry
        kc = buf_ref[slot, pl.ds(c * PAGES_PER_STEP, PAGES_PER_STEP)].reshape(keys, A_LAT)
        ckv = kc[:, :A_KVLORA].astype(BF16)
        kr = kc[:, A_KVLORA:].astype(BF16)
        s = (_bdot_nt(ql, ckv) + _bdot_nt(qr, kr)) * ATT_SCALE
        m_new = jnp.maximum(m_prev, jnp.max(s, axis=1, keepdims=True))
        alpha = jnp.exp(m_prev - m_new)
        p = jnp.exp(s - m_new)
        l_new = alpha * l_prev + jnp.sum(p, axis=1, keepdims=True)
        return m_new, l_new, alpha * acc + _bdot(p, ckv)

    init = (jnp.full((rows, 1), -jnp.inf, F32), jnp.zeros((rows, 1), F32), jnp.zeros((rows, A_KVLORA), F32))
    m_prev, l_prev, acc = lax.fori_loop(0, N_PAGES // PAGES_PER_STEP, chunk, init)

    kvn = kvn_ref[...]
    ckv_n = kvn[:, :A_KVLORA]
    kr_n = kvn[:, KV_ROPE0:KV_ROPE0 + A_ROPE]
    s = (_bdot_nt(ql, ckv_n) + _bdot_nt(qr, kr_n)) * ATT_SCALE
    t_row = lax.broadcasted_iota(jnp.int32, (rows, S_ROWS), 0) % S_ROWS
    t_col = lax.broadcasted_iota(jnp.int32, (rows, S_ROWS), 1)
    s = jnp.where(t_col <= jnp.minimum(t_row, DEC_SEQ - 1), s, -jnp.inf)
    m_new = jnp.maximum(m_prev, jnp.max(s, axis=1, keepdims=True))
    alpha = jnp.exp(m_prev - m_new)
    p = jnp.exp(s - m_new)
    l_new = alpha * l_prev + jnp.sum(p, axis=1, keepdims=True)
    o_lat = (alpha * acc + _bdot(p, ckv_n)) / l_new
    for h in range(A_HEADS):
        o_ref[:, h * HEAD_PAD:(h + 1) * HEAD_PAD] = _bdot(o_lat[h * S_ROWS:(h + 1) * S_ROWS], wuv_ref[h])


def _sample_attn(page_table, q, kvrow, wuk_t, wuv_p, cache, layer):
    nb = page_table.shape[0]
    blk0 = ROW_S // S_ROWS
    grid_spec = pltpu.PrefetchScalarGridSpec(
        num_scalar_prefetch=1,
        grid=(nb,),
        in_specs=[pl.BlockSpec((S_ROWS, ATT_W), lambda b, pt: (blk0 + b, 0)),
                  pl.BlockSpec((S_ROWS, A_KVLORA + LANES), lambda b, pt: (blk0 + b, 0)),
                  pl.BlockSpec(wuk_t.shape, lambda b, pt: (0, 0, 0)),
                  pl.BlockSpec(wuv_p.shape, lambda b, pt: (0, 0, 0)),
                  pl.BlockSpec(memory_space=pl.ANY)],
        out_specs=pl.BlockSpec((S_ROWS, ATT_W), lambda b, pt: (b, 0)),
        scratch_shapes=[pltpu.VMEM((2, N_PAGES, PAGE_SIZE, A_LAT), F32),
                        pltpu.SemaphoreType.DMA((2,))],
    )
    return pl.pallas_call(
        functools.partial(_sattn_kernel, layer=layer),
        grid_spec=grid_spec,
        out_shape=jax.ShapeDtypeStruct((nb * S_ROWS, ATT_W), F32),
        compiler_params=_cparams(("arbitrary",)),
        name="mla_sample_attn",
    )(page_table, q, kvrow, wuk_t, wuv_p, cache)


def _row_of(col_vec, eye):
    return jnp.sum(jnp.where(eye, col_vec, 0.0), axis=0, keepdims=True)


def _mlstm_kernel(g_ref, q_ref, k_ref, v_ref, og_ref, c0_ref, n0_ref, m0_ref, bi_ref, bf_ref, gn_ref,
                  o_ref, c_ref, n_ref, m_ref, c_scr, n_scr, m_scr, *, n_valid):
    ci = pl.program_id(1)
    L = q_ref.shape[0]

    @pl.when(ci == 0)
    def _load():
        c_scr[...] = c0_ref[0]
        n_scr[0:B_HEADS, :] = n0_ref[0]
        for h in range(B_HEADS):
            m_scr[h:h + 1, :] = jnp.broadcast_to(m0_ref[0, 0:1, h:h + 1], (1, LANES))

    row = lax.broadcasted_iota(jnp.int32, (L, L), 0)
    col = lax.broadcasted_iota(jnp.int32, (L, L), 1)
    eye = row == col
    tril = col <= row
    valid = lax.broadcasted_iota(jnp.int32, (L, 1), 0) < n_valid
    g = g_ref[...]
    lane = lax.broadcasted_iota(jnp.int32, (1, LANES), 1)
    m_out = jnp.zeros((1, LANES), F32)
    for h in range(B_HEADS):
        sl = slice(h * B_DK, (h + 1) * B_DK)
        q = q_ref[:, sl]
        k = k_ref[:, sl] * (B_DK ** -0.5)
        v = v_ref[:, sl]
        ig_col = jnp.where(valid, g[:, MI_LANE + h:MI_LANE + h + 1] + bi_ref[0:1, h:h + 1], -jnp.inf)
        fpre = g[:, MF_LANE + h:MF_LANE + h + 1] + bf_ref[0:1, h:h + 1]
        lf_col = jnp.where(valid, jnp.minimum(fpre, 0.0) - jnp.log1p(jnp.exp(-jnp.abs(fpre))), 0.0)
        ig_row = _row_of(ig_col, eye)
        lf_row = _row_of(lf_col, eye)
        b_col = jnp.sum(jnp.where(tril, lf_row, 0.0), axis=1, keepdims=True)
        b_row = jnp.sum(jnp.where(row <= col, lf_col, 0.0), axis=0, keepdims=True)
        m_prev = m_scr[h:h + 1, 0:1]
        c_prev = c_scr[h]
        n_prev = n_scr[h:h + 1, :]
        dmat = jnp.where(tril, b_col - b_row + ig_row, -jnp.inf)
        inter = b_col + m_prev
        m_t = jnp.maximum(inter, jnp.max(dmat, axis=1, keepdims=True))
        w_inter = jnp.exp(inter - m_t)
        s = _bdot_nt(q, k) * jnp.exp(dmat - m_t)
        num = w_inter * _bdot(q, c_prev) + _bdot(s, v)
        den = w_inter * jnp.sum(q * n_prev, axis=1, keepdims=True) + jnp.sum(s, axis=1, keepdims=True)
        hh = num / jnp.maximum(jnp.abs(den), jnp.exp(-m_t))
        m_new = m_t[n_valid - 1:n_valid, :]
        b_last = b_col[n_valid - 1:n_valid, :]
        w_end = jnp.exp(b_last - b_col + ig_col - m_new)
        decay = jnp.exp(b_last + m_prev - m_new)
        kw = k * w_end
        c_new = decay * c_prev + _bdot_tn(kw, v)
        n_new = decay * n_prev + jnp.sum(kw, axis=0, keepdims=True)
        c_scr[h] = c_new
        n_scr[h:h + 1, :] = n_new
        m_scr[h:h + 1, :] = jnp.broadcast_to(m_new, (1, LANES))
        m_out = jnp.where(lane == h, m_new, m_out)
        o_ref[:, sl] = _head_norm(hh, gn_ref[0:1, sl]) * _sigmoid(og_ref[:, sl])

    @pl.when(ci == pl.num_programs(1) - 1)
    def _store():
        c_ref[0] = c_scr[...]
        n_ref[0] = n_scr[0:B_HEADS, :]
        m_ref[0] = m_out


def _mlstm(proj, row0, nb, nc, L, n_valid, c0, n0, m0, bi, bf, gn, shared_state):
    blk0 = row0 // L
    rmap = lambda cb: (lambda b, c: (blk0 + b * nc + c, cb))
    smap = (lambda b, c: (0, 0, 0, 0)) if shared_state else (lambda b, c: (b, 0, 0, 0))
    smap3 = (lambda b, c: (0, 0, 0)) if shared_state else (lambda b, c: (b, 0, 0))
    w = B_HEADS * B_DK
    full2 = lambda a: pl.BlockSpec(a.shape, lambda b, c: (0, 0))
    return pl.pallas_call(
        functools.partial(_mlstm_kernel, n_valid=n_valid),
        grid=(nb, nc),
        in_specs=[pl.BlockSpec((L, LANES), rmap(GATE_COL // LANES)),
                  pl.BlockSpec((L, w), rmap(2)), pl.BlockSpec((L, w), rmap(3)),
                  pl.BlockSpec((L, w), rmap(4)), pl.BlockSpec((L, w), rmap(5)),
                  pl.BlockSpec((1, B_HEADS, B_DK, B_DV), smap),
                  pl.BlockSpec((1, B_HEADS, B_DK), smap3),
                  pl.BlockSpec((1, 1, B_HEADS), smap3),
                  full2(bi), full2(bf), full2(gn)],
        out_specs=[pl.BlockSpec((L, w), lambda b, c: (b * nc + c, 0)),
                   pl.BlockSpec((1, B_HEADS, B_DK, B_DV), lambda b, c: (b, 0, 0, 0)),
                   pl.BlockSpec((1, B_HEADS, B_DK), lambda b, c: (b, 0, 0)),
                   pl.BlockSpec((1, 1, LANES), lambda b, c: (b, 0, 0))],
        out_shape=[jax.ShapeDtypeStruct((nb * nc * L, w), F32),
                   jax.ShapeDtypeStruct((nb, B_HEADS, B_DK, B_DV), F32),
                   jax.ShapeDtypeStruct((nb, B_HEADS, B_DK), F32),
                   jax.ShapeDtypeStruct((nb, 1, LANES), F32)],
        scratch_shapes=[pltpu.VMEM((B_HEADS, B_DK, B_DV), F32),
                        pltpu.VMEM((SUBLANES, B_DK), F32),
                        pltpu.VMEM((SUBLANES, LANES), F32)],
        compiler_params=_cparams(("parallel", "arbitrary")),
        name="mlstm",
    )(proj, proj, proj, proj, proj, c0, n0, m0, bi, bf, gn)


RET_LOG_GAMMA = tuple(math.log1p(-(2.0 ** (-5.0 - h))) for h in range(D_HEADS))


def _ret_kernel(q_ref, k_ref, v_ref, g_ref, cr_ref, sr_ref, s0_ref, gn_ref, o_ref, s_ref, s_scr, *, n_valid):
    ci = pl.program_id(1)
    L = q_ref.shape[0]

    @pl.when(ci == 0)
    def _load():
        s_scr[...] = s0_ref[0]

    row = lax.broadcasted_iota(jnp.int32, (L, L), 0)
    col = lax.broadcasted_iota(jnp.int32, (L, L), 1)
    rel = (row - col).astype(F32)
    t_col = lax.broadcasted_iota(jnp.int32, (L, 1), 0)
    tf = t_col.astype(F32)
    cr, sr = cr_ref[...], sr_ref[...]
    half = D_DK // 2
    for h in range(D_HEADS):
        lg = RET_LOG_GAMMA[h]
        sl = slice(h * D_DK, (h + 1) * D_DK)
        q = q_ref[:, sl]
        k = k_ref[:, sl]
        q = q * cr + pltpu.roll(q, half, 1) * sr
        k = (k * cr + pltpu.roll(k, half, 1) * sr) * (D_DK ** -0.5)
        v = v_ref[:, sl]
        s_prev = s_scr[h]
        dec = jnp.where(rel >= 0, jnp.exp(lg * jnp.maximum(rel, 0.0)), 0.0)
        inner = _bdot_nt(q, k) * dec
        o = _bdot(inner, v) + jnp.exp((tf + 1.0) * lg) * _bdot(q, s_prev)
        w_end = jnp.where(t_col < n_valid, jnp.exp((n_valid - 1.0 - tf) * lg), 0.0)
        s_scr[h] = math.exp(n_valid * lg) * s_prev + _bdot_tn(k * w_end, v)
        gate = g_ref[:, sl]
        o_ref[:, sl] = _head_norm(o, gn_ref[0:1, sl]) * (gate * _sigmoid(gate))

    @pl.when(ci == pl.num_programs(1) - 1)
    def _store():
        s_ref[0] = s_scr[...]


def _retention(proj, tabs, row0, nb, nc, L, n_valid, s0, gn, shared_state):
    blk0 = row0 // L
    rmap = lambda cb: (lambda b, c: (blk0 + b * nc + c, cb))
    smap = (lambda b, c: (0, 0, 0, 0)) if shared_state else (lambda b, c: (b, 0, 0, 0))
    w = D_HEADS * D_DK
    return pl.pallas_call(
        functools.partial(_ret_kernel, n_valid=n_valid),
        grid=(nb, nc),
        in_specs=[pl.BlockSpec((L, w), rmap(2)), pl.BlockSpec((L, w), rmap(3)),
                  pl.BlockSpec((L, w), rmap(4)), pl.BlockSpec((L, w), rmap(5)),
                  pl.BlockSpec((L, LANES), rmap(0)), pl.BlockSpec((L, LANES), rmap(0)),
                  pl.BlockSpec((1, D_HEADS, D_DK, D_DV), smap),
                  pl.BlockSpec(gn.shape, lambda b, c: (0, 0))],
        out_specs=[pl.BlockSpec((L, w), lambda b, c: (b * nc + c, 0)),
                   pl.BlockSpec((1, D_HEADS, D_DK, D_DV), lambda b, c: (b, 0, 0, 0))],
        out_shape=[jax.ShapeDtypeStruct((nb * nc * L, w), F32),
                   jax.ShapeDtypeStruct((nb, D_HEADS, D_DK, D_DV), F32)],
        scratch_shapes=[pltpu.VMEM((D_HEADS, D_DK, D_DV), F32)],
        compiler_params=_cparams(("parallel", "arbitrary")),
        name="retention",
    )(proj, proj, proj, proj, tabs[0], tabs[1], s0, gn)


def _rglru_kernel(x_ref, gr_ref, h0_ref, cv0_ref, cw_ref, cb_ref, wa_ref, ba_ref, wi_ref, bi_ref, lam_ref,
                  o_ref, h_ref, cv_ref, xp_scr, h_scr, *, n_valid):
    ci = pl.program_id(1)
    L = x_ref.shape[0]
    tail = C_CONV - 1

    @pl.when(ci == 0)
    def _load():
        xp_scr[SUBLANES - tail:SUBLANES, :] = cv0_ref[0]
        h_scr[0:1, :] = h0_ref[0]

    xp_scr[SUBLANES:SUBLANES + L, :] = x_ref[...]
    u = cb_ref[...] + sum(xp_scr[SUBLANES - tail + j:SUBLANES - tail + j + L, :] * cw_ref[j:j + 1, :]
                          for j in range(C_CONV))
    new_tail = xp_scr[SUBLANES + n_valid - tail:SUBLANES + n_valid, :]
    xp_scr[SUBLANES - tail:SUBLANES, :] = new_tail
    r = _sigmoid(_bdot(u, wa_ref[...]) + ba_ref[...])
    i = _sigmoid(_bdot(u, wi_ref[...]) + bi_ref[...])
    nlam = -lam_ref[...]
    softplus = jnp.maximum(nlam, 0.0) + jnp.log1p(jnp.exp(-jnp.abs(nlam)))
    log_a = -C_POW * r * softplus
    a = jnp.exp(log_a)
    b = jnp.sqrt(-jnp.tanh(log_a) * (a * a + 1.0)) * (i * u)
    t_idx = lax.broadcasted_iota(jnp.int32, (L, 1), 0)
    step = 1
    while step < L:
        keep = t_idx >= step
        a_sh = jnp.where(keep, pltpu.roll(a, step, 0), 1.0)
        b_sh = jnp.where(keep, pltpu.roll(b, step, 0), 0.0)
        b = a * b_sh + b
        a = a * a_sh
        step *= 2
    hseq = a * h_scr[0:1, :] + b
    h_last = hseq[n_valid - 1:n_valid, :]
    h_scr[0:1, :] = h_last
    gr = gr_ref[...]
    gelu = 0.5 * gr * (1.0 + jnp.tanh(math.sqrt(2.0 / math.pi) * (gr + 0.044715 * (gr * gr * gr))))
    o_ref[...] = hseq * gelu

    @pl.when(ci == pl.num_programs(1) - 1)
    def _store():
        h_ref[0] = h_last
        cv_ref[0] = new_tail


def _rglru(proj, row0, nb, nc, L, n_valid, h0, cv0, cw, cb, wa, ba, wi, bi, lam, shared_state):
    blk0 = row0 // L
    rmap = lambda cb_: (lambda b, c: (blk0 + b * nc + c, cb_))
    smap = (lambda b, c: (0, 0, 0)) if shared_state else (lambda b, c: (b, 0, 0))
    full2 = lambda a: pl.BlockSpec(a.shape, lambda b, c: (0, 0))
    w = C_WIDTH
    return pl.pallas_call(
        functools.partial(_rglru_kernel, n_valid=n_valid),
        grid=(nb, nc),
        in_specs=[pl.BlockSpec((L, w), rmap(0)), pl.BlockSpec((L, w), rmap(1)),
                  pl.BlockSpec((1, 1, w), smap), pl.BlockSpec((1, C_CONV - 1, w), smap),
                  full2(cw), full2(cb), full2(wa), full2(ba), full2(wi), full2(bi), full2(lam)],
        out_specs=[pl.BlockSpec((L, w), lambda b, c: (b * nc + c, 0)),
                   pl.BlockSpec((1, 1, w), lambda b, c: (b, 0, 0)),
                   pl.BlockSpec((1, C_CONV - 1, w), lambda b, c: (b, 0, 0))],
        out_shape=[jax.ShapeDtypeStruct((nb * nc * L, w), F32),
                   jax.ShapeDtypeStruct((nb, 1, w), F32),
                   jax.ShapeDtypeStruct((nb, C_CONV - 1, w), F32)],
        scratch_shapes=[pltpu.VMEM((SUBLANES + L, w), F32), pltpu.VMEM((SUBLANES, w), F32)],
        compiler_params=_cparams(("parallel", "arbitrary")),
        name="rglru",
    )(proj, proj, h0, cv0, cw, cb, wa, ba, wi, bi, lam)


def _outproj_kernel(o1_ref, o2_ref, w1_ref, w2_ref, x_ref, g_ref, b_ref, y_ref):
    mix = _bdot(o1_ref[...], w1_ref[...]) + _bdot(o2_ref[...], w2_ref[...])
    y_ref[...] = _layer_norm(DN_ALPHA * x_ref[...] + mix, g_ref[...], b_ref[...])


def _outproj_ln(o1, o2, w1, w2, x, g, b):
    n = x.shape[0]
    row = lambda a: pl.BlockSpec((TM, a.shape[1]), lambda i: (i, 0))
    full = lambda a: pl.BlockSpec(a.shape, lambda i: (0, 0))
    return pl.pallas_call(
        _outproj_kernel,
        grid=(n // TM,),
        in_specs=[row(o1), row(o2), full(w1), full(w2), row(x), full(g), full(b)],
        out_specs=row(x),
        out_shape=jax.ShapeDtypeStruct(x.shape, F32),
        compiler_params=_cparams(("parallel",)),
        name="outproj_ln",
    )(o1, o2, w1, w2, x, g, b)


ROUTER_E0 = N_GROUPS


def _moe_kernel(x_ref, wr_ref, br_ref, w1_ref, w3_ref, w2_ref, g_ref, b_ref, y_ref, xb_scr, gate_scr, acc_scr):
    j = pl.program_id(1)

    @pl.when(j == 0)
    def _route():
        x = x_ref[...]
        xb_scr[...] = x.astype(BF16)
        logits = jnp.dot(x, wr_ref[...], precision=lax.Precision.HIGHEST,
                         preferred_element_type=F32) + br_ref[...]
        lane = lax.broadcasted_iota(jnp.int32, logits.shape, 1)
        gl = jnp.where(lane < N_GROUPS, logits, -jnp.inf)
        gmax = jnp.max(gl, axis=1, keepdims=True)
        g_gate = 1.0 / jnp.sum(jnp.exp(gl - gmax), axis=1, keepdims=True)
        g_idx = jnp.min(jnp.where(gl == gmax, lane, LANES), axis=1, keepdims=True)
        e_lo = ROUTER_E0 + g_idx * E_PER_GROUP
        el = jnp.where(jnp.logical_and(lane >= e_lo, lane < e_lo + E_PER_GROUP), logits, -jnp.inf)
        e1 = jnp.max(el, axis=1, keepdims=True)
        i1 = jnp.min(jnp.where(el == e1, lane, LANES), axis=1, keepdims=True)
        el2 = jnp.where(lane == i1, -jnp.inf, el)
        e2 = jnp.max(el2, axis=1, keepdims=True)
        i2 = jnp.min(jnp.where(el2 == e2, lane, LANES), axis=1, keepdims=True)
        r = jnp.exp(e2 - e1)
        w1 = g_gate / (1.0 + r)
        gate = jnp.where(lane == i1, w1, 0.0) + jnp.where(lane == i2, w1 * r, 0.0)
        for gi in range(N_GROUPS):
            lo = ROUTER_E0 + gi * E_PER_GROUP
            gate_scr[gi] = gate[:, lo:lo + E_PER_GROUP]
        acc_scr[...] = jnp.zeros_like(acc_scr)

    xb = xb_scr[...]
    h1 = jnp.dot(xb, w1_ref[0], preferred_element_type=F32)
    h3 = jnp.dot(xb, w3_ref[0], preferred_element_type=F32)
    hg = h1 * _sigmoid(h1) * h3
    gate = gate_scr[j]
    parts = []
    for e in range(E_PER_GROUP):
        sl = slice(e * D_EXPERT, (e + 1) * D_EXPERT)
        parts.append((hg[:, sl] * gate[:, e:e + 1]).astype(BF16))
    acc_scr[...] += jnp.dot(jnp.concatenate(parts, axis=1), w2_ref[0], preferred_element_type=F32)

    @pl.when(j == pl.num_programs(1) - 1)
    def _fin():
        y_ref[...] = _layer_norm(DN_ALPHA * x_ref[...] + acc_scr[...], g_ref[...], b_ref[...])


def _moe_ln(x, wr, br, w1, w3, w2, g, b):
    n = x.shape[0]
    gw = E_PER_GROUP * D_EXPERT
    row = pl.BlockSpec((TM, D_MODEL), lambda i, j: (i, 0))
    full = lambda a: pl.BlockSpec(a.shape, lambda i, j: (0, 0))
    return pl.pallas_call(
        _moe_kernel,
        grid=(n // TM, N_GROUPS),
        in_specs=[row, full(wr), full(br),
                  pl.BlockSpec((1, D_MODEL, gw), lambda i, j: (j, 0, 0)),
                  pl.BlockSpec((1, D_MODEL, gw), lambda i, j: (j, 0, 0)),
                  pl.BlockSpec((1, gw, D_MODEL), lambda i, j: (j, 0, 0)),
                  full(g), full(b)],
        out_specs=row,
        out_shape=jax.ShapeDtypeStruct(x.shape, F32),
        scratch_shapes=[pltpu.VMEM((TM, D_MODEL), BF16),
                        pltpu.VMEM((N_GROUPS, TM, E_PER_GROUP), F32),
                        pltpu.VMEM((TM, D_MODEL), F32)],
        compiler_params=_cparams(("parallel", "arbitrary")),
        name="moe_ln",
    )(x, wr, br, w1, w3, w2, g, b)


def _positions():
    pos = jnp.zeros((N_TOK,), F32)
    main = (N_META + jnp.arange(SEQ, dtype=jnp.int32)).astype(F32)
    pos = pos.at[:N_MAIN].set(jnp.tile(main, BATCH))
    samp = (PAST_LEN + jnp.minimum(jnp.arange(S_ROWS, dtype=jnp.int32), DEC_SEQ - 1)).astype(F32)
    pos = pos.at[ROW_S:ROW_M].set(jnp.tile(samp, DEC_BATCH))
    pos = pos.at[ROW_M:ROW_M + N_META].set(jnp.arange(N_META, dtype=jnp.int32).astype(F32))
    return pos


def _rope_tables_mla(pos):
    half = A_ROPE // 2
    freqs = ROPE_THETA ** (-jnp.arange(half, dtype=F32) / half)
    ang = pos[:, None] * freqs
    cos, sin = jnp.cos(ang), jnp.sin(ang)
    n = pos.shape[0]
    zeros = lambda w: jnp.zeros((n, w), F32)
    rest = LANES - ROPE_LANE0 - A_ROPE
    c = jnp.concatenate([jnp.ones((n, ROPE_LANE0), F32), cos, cos, zeros(rest)], axis=1)
    s1 = jnp.concatenate([zeros(ROPE_LANE0 + half), sin, zeros(rest)], axis=1)
    s2 = jnp.concatenate([zeros(ROPE_LANE0), -sin, zeros(half + rest)], axis=1)
    return c, s1, s2


def _rope_tables_ret(pos):
    half = D_DK // 2
    freqs = ROPE_THETA ** (-jnp.arange(half, dtype=F32) / half)
    ang = pos[:, None] * freqs
    cos, sin = jnp.cos(ang), jnp.sin(ang)
    return jnp.concatenate([cos, cos], axis=1), jnp.concatenate([-sin, sin], axis=1)


def _pad_cols(a, w):
    return jnp.pad(a, ((0, 0), (0, w - a.shape[1])))


def _even_in_weight(w):
    cuts = [0, A_QLORA, A_QLORA + A_KVLORA, A_QLORA + A_KVLORA + A_ROPE]
    c_q, c_kv, k_r = (w[:, cuts[i]:cuts[i + 1]] for i in range(3))
    m0 = cuts[3]
    mw = B_HEADS * B_DK
    big = w[:, m0:m0 + 4 * mw]
    gates = w[:, m0 + 4 * mw:]
    head = jnp.concatenate([c_q, c_kv, jnp.zeros((D_MODEL, ROPE_LANE0), w.dtype), k_r, gates], axis=1)
    return jnp.concatenate([_pad_cols(head, 1024), big], axis=1).astype(BF16)


def _head_pad_cols(a, width):
    k = a.shape[0]
    return jnp.pad(a, ((0, 0), (0, 0), (0, HEAD_PAD - width))).reshape(k, A_HEADS * HEAD_PAD)


def _block_diag(w):
    eye = jnp.eye(C_BLOCKS, dtype=w.dtype)
    return (w[:, :, None, :] * eye[:, None, :, None]).reshape(C_WIDTH, C_WIDTH)


def _assemble(main, samp, meta):
    w = main.shape[1]
    pad = jnp.zeros((N_TOK - ROW_M - N_META, w), main.dtype)
    return jnp.concatenate([main, samp, meta, pad], axis=0)


def kernel(x_prompt, x_sample, cache_mla, page_table, state_mlstm_c, state_mlstm_n, state_mlstm_m,
           state_rglru_h, state_rglru_conv, state_ret, meta_tokens, w_in_e, g_cq, g_ckv, w_uq, w_uk, w_uv,
           b_mi, b_mf, g_mlstm, w_out_e, w_in_o, conv_w, conv_b, w_ra, b_ra, w_ri, b_ri, lam, g_ret, w_out_o,
           ln_mix_g, ln_mix_b, ln_ffn_g, ln_ffn_b, w_gr, b_gr, w_er, b_er, w_e1, w_e3, w_e2):
    nc_main = SEQ // CHUNK
    xs = jnp.pad(x_sample, ((0, 0), (0, S_ROWS - DEC_SEQ), (0, 0))).reshape(N_SAMP, D_MODEL)
    x = _assemble(x_prompt.reshape(N_MAIN, D_MODEL), xs, meta_tokens)

    pos = _positions()
    tabs_mla = _rope_tables_mla(pos)
    tabs_ret = _rope_tables_ret(pos)
    row2 = lambda a: a.reshape(1, -1)

    kv_p, kv_s, c_p, c_s, n_p, n_s, m_p, m_s = [], [], [], [], [], [], [], []
    h_p, h_s, cv_p, cv_s, s_p, s_s = [], [], [], [], [], []

    for l in range(DEPTH):
        j = l // 2
        if l % 2 == 0:
            proj = _matmul(x, _even_in_weight(w_in_e[j]))
            wuq = _head_pad_cols(w_uq[j].reshape(A_QLORA, A_HEADS, A_NOPE + A_ROPE), A_NOPE + A_ROPE).astype(BF16)
            wk = _head_pad_cols(w_uk[j], A_NOPE).astype(BF16)
            wv = _head_pad_cols(w_uv[j], A_V).astype(BF16)
            q, kp, vp, kvrow = _mla_prep(proj, tabs_mla, row2(g_cq[j]), row2(g_ckv[j]), wuq, wk, wv)
            o_main = _flash(q, kp, vp, BATCH, SEQ, ROW_M)
            o_meta = _meta_attn(q, kp, vp, ROW_M)
            wuk_t = jnp.transpose(w_uk[j], (1, 2, 0)).astype(BF16)
            wuv_p = jnp.pad(jnp.transpose(w_uv[j], (1, 0, 2)),
                            ((0, 0), (0, 0), (0, HEAD_PAD - A_V))).astype(BF16)
            o_samp = _sample_attn(page_table, q, kvrow, wuk_t, wuv_p, cache_mla, j)
            o1 = _assemble(o_main, o_samp, o_meta)

            bi, bf, gn = row2(b_mi[j]), row2(b_mf[j]), row2(g_mlstm[j])
            zc = jnp.zeros((1, B_HEADS, B_DK, B_DV), F32)
            zn = jnp.zeros((1, B_HEADS, B_DK), F32)
            zm = jnp.zeros((1, 1, B_HEADS), F32)
            ob_meta, c_m, n_m, m_m = _mlstm(proj, ROW_M, 1, 1, N_META, N_META, zc, zn, zm, bi, bf, gn, True)
            ob_main, c_pm, n_pm, m_pm = _mlstm(proj, 0, BATCH, nc_main, CHUNK, CHUNK, c_m, n_m,
                                               m_m[:, :, :B_HEADS], bi, bf, gn, True)
            ob_samp, c_sm, n_sm, m_sm = _mlstm(proj, ROW_S, DEC_BATCH, 1, S_ROWS, DEC_SEQ, state_mlstm_c[j],
                                               state_mlstm_n[j], state_mlstm_m[j].reshape(DEC_BATCH, 1, B_HEADS),
                                               bi, bf, gn, False)
            o2 = _assemble(ob_main, ob_samp, ob_meta)

            w1 = jnp.pad(w_out_e[j][:A_HEADS * A_V].reshape(A_HEADS, A_V, D_MODEL),
                         ((0, 0), (0, HEAD_PAD - A_V), (0, 0))).reshape(ATT_W, D_MODEL).astype(BF16)
            w2 = w_out_e[j][A_HEADS * A_V:].astype(BF16)

            kv_main = kvrow[:N_MAIN].reshape(BATCH, SEQ, -1)
            kv_meta = jnp.broadcast_to(kvrow[ROW_M:ROW_M + N_META][None], (BATCH, N_META, kvrow.shape[1]))
            kv_all = jnp.concatenate([kv_meta, kv_main], axis=1)
            kv_p.append(jnp.concatenate([kv_all[..., :A_KVLORA], kv_all[..., KV_ROPE0:KV_ROPE0 + A_ROPE]], -1))
            kv_sr = kvrow[ROW_S:ROW_M].reshape(DEC_BATCH, S_ROWS, -1)[:, :DEC_SEQ]
            kv_s.append(jnp.concatenate([kv_sr[..., :A_KVLORA], kv_sr[..., KV_ROPE0:KV_ROPE0 + A_ROPE]], -1))
            c_p.append(c_pm); n_p.append(n_pm); m_p.append(m_pm[:, 0, :B_HEADS])
            c_s.append(c_sm); n_s.append(n_sm); m_s.append(m_sm[:, 0, :B_HEADS])
        else:
            proj = _matmul(x, w_in_o[j].astype(BF16))
            cw, cb = conv_w[j], row2(conv_b[j])
            wa, wi = _block_diag(w_ra[j]).astype(BF16), _block_diag(w_ri[j]).astype(BF16)
            ba, bi_, lam_ = row2(b_ra[j]), row2(b_ri[j]), row2(lam[j])
            zh = jnp.zeros((1, 1, C_WIDTH), F32)
            zcv = jnp.zeros((1, C_CONV - 1, C_WIDTH), F32)
            args = (cw, cb, wa, ba, wi, bi_, lam_)
            oc_meta, h_m, cv_m = _rglru(proj, ROW_M, 1, 1, N_META, N_META, zh, zcv, *args, True)
            oc_main, h_pm, cv_pm = _rglru(proj, 0, BATCH, nc_main, CHUNK, CHUNK, h_m, cv_m, *args, True)
            oc_samp, h_sm, cv_sm = _rglru(proj, ROW_S, DEC_BATCH, 1, S_ROWS, DEC_SEQ,
                                          state_rglru_h[j].reshape(DEC_BATCH, 1, C_WIDTH), state_rglru_conv[j],
                                          *args, False)
            o1 = _assemble(oc_main, oc_samp, oc_meta)

            gn = row2(g_ret[j])
            zs = jnp.zeros((1, D_HEADS, D_DK, D_DV), F32)
            od_meta, s_m = _retention(proj, tabs_ret, ROW_M, 1, 1, N_META, N_META, zs, gn, True)
            od_main, s_pm = _retention(proj, tabs_ret, 0, BATCH, nc_main, CHUNK, CHUNK, s_m, gn, True)
            od_samp, s_sm = _retention(proj, tabs_ret, ROW_S, DEC_BATCH, 1, S_ROWS, DEC_SEQ, state_ret[j], gn, False)
            o2 = _assemble(od_main, od_samp, od_meta)

            w1 = w_out_o[j][:C_WIDTH].astype(BF16)
            w2 = w_out_o[j][C_WIDTH:].astype(BF16)
            h_p.append(h_pm[:, 0]); cv_p.append(cv_pm); s_p.append(s_pm)
            h_s.append(h_sm[:, 0]); cv_s.append(cv_sm); s_s.append(s_sm)

        x = _outproj_ln(o1, o2, w1, w2, x, row2(ln_mix_g[l]), row2(ln_mix_b[l]))

        wr = jnp.concatenate([w_gr[l], jnp.transpose(w_er[l], (1, 0, 2)).reshape(D_MODEL, N_EXPERTS)], axis=1)
        br = jnp.concatenate([b_gr[l], b_er[l].reshape(N_EXPERTS)])
        gw = E_PER_GROUP * D_EXPERT
        w1e = jnp.transpose(w_e1[l].reshape(N_GROUPS, E_PER_GROUP, D_MODEL, D_EXPERT),
                            (0, 2, 1, 3)).reshape(N_GROUPS, D_MODEL, gw).astype(BF16)
        w3e = jnp.transpose(w_e3[l].reshape(N_GROUPS, E_PER_GROUP, D_MODEL, D_EXPERT),
                            (0, 2, 1, 3)).reshape(N_GROUPS, D_MODEL, gw).astype(BF16)
        w2e = w_e2[l].reshape(N_GROUPS, gw, D_MODEL).astype(BF16)
        x = _moe_ln(x, _pad_cols(wr, LANES), _pad_cols(row2(br), LANES), w1e, w3e, w2e,
                    row2(ln_ffn_g[l]), row2(ln_ffn_b[l]))

    y_prompt = x[:N_MAIN].reshape(BATCH, SEQ, D_MODEL)
    y_sample = x[ROW_S:ROW_M].reshape(DEC_BATCH, S_ROWS, D_MODEL)[:, :DEC_SEQ]
    st = jnp.stack
    return (y_prompt, y_sample, st(kv_p), st(kv_s), st(c_p), st(c_s), st(n_p), st(n_s), st(m_p), st(m_s),
            st(h_p), st(h_s), st(cv_p), st(cv_s), st(s_p), st(s_s))
```

```python
import functools
import math

import jax
import jax.numpy as jnp
from jax import lax
from jax.experimental import pallas as pl
from jax.experimental.pallas import tpu as pltpu

F32 = jnp.float32
BF16 = jnp.bfloat16

D_MODEL = 1024
BATCH = 4
SEQ = 4096
DEPTH = 4
DEC_BATCH = 128
DEC_SEQ = 4
PAST_LEN = 8192
PAGE_SIZE = 128
N_PAGES = PAST_LEN // PAGE_SIZE
N_META = 16
N_EVEN = (DEPTH + 1) // 2
N_ODD = DEPTH // 2
CHUNK = 128
ROPE_THETA = 10000.0
A_HEADS = 8
A_NOPE = 64
A_ROPE = 32
A_V = 64
A_QLORA = 384
A_KVLORA = 256
A_LAT = A_KVLORA + A_ROPE
B_HEADS = 4
B_DK = 128
B_DV = 128
C_WIDTH = 512
C_BLOCKS = 8
C_BW = C_WIDTH // C_BLOCKS
C_CONV = 4
C_POW = 8.0
D_HEADS = 4
D_DK = 128
D_DV = 128
N_GROUPS = 4
E_PER_GROUP = 8
N_EXPERTS = N_GROUPS * E_PER_GROUP
D_EXPERT = 128
DN_ALPHA = (2.0 * DEPTH) ** 0.25
EPS = 1e-5

LANES = 128
SUBLANES = 8
VMEM_LIMIT = 48 * 1024 * 1024

S_ROWS = 8
N_MAIN = BATCH * SEQ
ROW_S = N_MAIN
N_SAMP = DEC_BATCH * S_ROWS
ROW_M = ROW_S + N_SAMP
TM = 512
N_TOK = -(-(ROW_M + N_META) // TM) * TM

HEAD_PAD = LANES
ATT_W = A_HEADS * HEAD_PAD
ROPE_LANE0 = A_NOPE
PROJ_E_W = 3072
GATE_COL = 640
MI_LANE = ROPE_LANE0 + A_ROPE
MF_LANE = MI_LANE + B_HEADS

TQ = 256
TK = 512


def _cparams(sem):
    return pltpu.CompilerParams(dimension_semantics=sem, vmem_limit_bytes=VMEM_LIMIT)


def _bdot(a, b):
    return jnp.dot(a.astype(BF16), b.astype(BF16), preferred_element_type=F32)


def _bdot_nt(a, b):
    return lax.dot_general(a.astype(BF16), b.astype(BF16), (((1,), (1,)), ((), ())),
                           preferred_element_type=F32)


def _bdot_tn(a, b):
    return lax.dot_general(a.astype(BF16), b.astype(BF16), (((0,), (0,)), ((), ())),
                           preferred_element_type=F32)


def _sigmoid(x):
    return 1.0 / (1.0 + jnp.exp(-x))


def _layer_norm(x, g, b):
    mu = jnp.mean(x, axis=-1, keepdims=True)
    xc = x - mu
    var = jnp.mean(xc * xc, axis=-1, keepdims=True)
    return xc * lax.rsqrt(var + EPS) * g + b


def _head_norm(x, g):
    mu = jnp.mean(x, axis=-1, keepdims=True)
    xc = x - mu
    var = jnp.mean(xc * xc, axis=-1, keepdims=True)
    return xc * lax.rsqrt(var + EPS) * g


def _mm_kernel(x_ref, w_ref, o_ref):
    o_ref[...] = _bdot(x_ref[...], w_ref[...])


def _matmul(x, w):
    n, k = x.shape
    m = w.shape[1]
    return pl.pallas_call(
        _mm_kernel,
        grid=(n // TM,),
        in_specs=[pl.BlockSpec((TM, k), lambda i: (i, 0)),
                  pl.BlockSpec((k, m), lambda i: (0, 0))],
        out_specs=pl.BlockSpec((TM, m), lambda i: (i, 0)),
        out_shape=jax.ShapeDtypeStruct((n, m), F32),
        compiler_params=_cparams(("parallel",)),
        name="proj_in",
    )(x, w)


def _rope_tile(x, c, s1, s2):
    half = A_ROPE // 2
    return x * c + pltpu.roll(x, half, 1) * s1 + pltpu.roll(x, LANES - half, 1) * s2


def _mla_prep_kernel(a_ref, c_ref, s1_ref, s2_ref, gq_ref, gkv_ref, wuq_ref, wk_ref, wv_ref,
                     q_ref, k_ref, v_ref, kv_ref):
    a = a_ref[...]
    c, s1, s2 = c_ref[...], s1_ref[...], s2_ref[...]
    cq = a[:, :A_QLORA]
    ckv = a[:, A_QLORA:A_QLORA + A_KVLORA]
    cq = cq * lax.rsqrt(jnp.mean(cq * cq, axis=-1, keepdims=True) + EPS) * gq_ref[...]
    ckv = ckv * lax.rsqrt(jnp.mean(ckv * ckv, axis=-1, keepdims=True) + EPS) * gkv_ref[...]
    kr = _rope_tile(a[:, GATE_COL:GATE_COL + LANES], c, s1, s2)
    kv_ref[:, :A_KVLORA] = ckv
    kv_ref[:, A_KVLORA:] = kr
    q = _bdot(cq, wuq_ref[...])
    kn = _bdot(ckv, wk_ref[...])
    lane = lax.broadcasted_iota(jnp.int32, (1, ATT_W), 1)
    ones_col = jnp.where(lane % HEAD_PAD == A_V, 1.0, 0.0)
    v_ref[...] = (_bdot(ckv, wv_ref[...]) + ones_col).astype(BF16)
    for h in range(A_HEADS):
        sl = slice(h * HEAD_PAD, (h + 1) * HEAD_PAD)
        q_ref[:, sl] = _rope_tile(q[:, sl], c, s1, s2)
        k_ref[:, sl] = (kn[:, sl] + kr).astype(BF16)


def _mla_prep(proj, tabs, gq, gkv, wuq, wk, wv):
    n = proj.shape[0]
    row = lambda w: pl.BlockSpec((TM, w), lambda i: (i, 0))
    full = lambda a: pl.BlockSpec(a.shape, lambda i: (0,) * a.ndim)
    return pl.pallas_call(
        _mla_prep_kernel,
        grid=(n // TM,),
        in_specs=[row(GATE_COL + LANES), row(LANES), row(LANES), row(LANES),
                  full(gq), full(gkv), full(wuq), full(wk), full(wv)],
        out_specs=[row(ATT_W), row(ATT_W), row(ATT_W), row(A_KVLORA + LANES)],
        out_shape=[jax.ShapeDtypeStruct((n, ATT_W), F32),
                   jax.ShapeDtypeStruct((n, ATT_W), BF16),
                   jax.ShapeDtypeStruct((n, ATT_W), BF16),
                   jax.ShapeDtypeStruct((n, A_KVLORA + LANES), F32)],
        compiler_params=_cparams(("parallel",)),
        name="mla_prep",
    )(proj, *tabs, gq, gkv, wuq, wk, wv)


ATT_SCALE = (A_NOPE + A_ROPE) ** -0.5


QK_SCALE2 = ATT_SCALE * math.log2(math.e)


def _flash_kernel(q_ref, k_ref, v_ref, km_ref, vm_ref, buf_ref, o_ref, qs_ref, m_ref, acc_ref):
    del buf_ref
    qi = pl.program_id(1)
    ki = pl.program_id(2)
    kdiag = (qi * TQ) // TK

    @pl.when(ki == 0)
    def _init():
        qs_ref[...] = (q_ref[...] * QK_SCALE2).astype(BF16)
        for h in range(A_HEADS):
            sl = slice(h * HEAD_PAD, (h + 1) * HEAD_PAD)
            s = _bdot_nt(qs_ref[:, sl], km_ref[:, sl])
            m = jnp.max(s, axis=1, keepdims=True)
            m_ref[h] = m
            acc_ref[:, sl] = _bdot(jnp.exp2(s - m), vm_ref[:, sl])

    def step(masked):
        if masked:
            row = lax.broadcasted_iota(jnp.int32, (TQ, TK), 0)
            col = lax.broadcasted_iota(jnp.int32, (TQ, TK), 1)
            visible = col <= row + (qi * TQ - ki * TK)
        for h in range(A_HEADS):
            sl = slice(h * HEAD_PAD, (h + 1) * HEAD_PAD)
            s = _bdot_nt(qs_ref[:, sl], k_ref[:, sl])
            if masked:
                s = jnp.where(visible, s, -jnp.inf)
            m_prev = m_ref[h]
            m_new = jnp.maximum(m_prev, jnp.max(s, axis=1, keepdims=True))
            alpha = jnp.exp2(m_prev - m_new)
            p = jnp.exp2(s - m_new)
            acc_ref[:, sl] = alpha * acc_ref[:, sl] + _bdot(p, v_ref[:, sl])
            m_ref[h] = m_new

    pl.when(ki < kdiag)(functools.partial(step, False))
    pl.when(ki == kdiag)(functools.partial(step, True))

    @pl.when(ki == pl.num_programs(2) - 1)
    def _fin():
        for h in range(A_HEADS):
            sl = slice(h * HEAD_PAD, (h + 1) * HEAD_PAD)
            a = acc_ref[:, sl]
            o_ref[:, sl] = a / a[:, A_V:A_V + 1]


def _flash(q, k, v, obuf, nb, seq, meta_row):
    nq = seq // TQ
    nk = seq // TK
    meta_blk = meta_row // N_META
    qmap = lambda b, i, j: (b * nq + i, 0)
    kmap = lambda b, i, j: (b * nk + jnp.minimum(j, (i * TQ) // TK), 0)
    mmap = lambda b, i, j: (meta_blk, 0)
    return pl.pallas_call(
        _flash_kernel,
        grid=(nb, nq, nk),
        in_specs=[pl.BlockSpec((TQ, ATT_W), qmap),
                  pl.BlockSpec((TK, ATT_W), kmap),
                  pl.BlockSpec((TK, ATT_W), kmap),
                  pl.BlockSpec((N_META, ATT_W), mmap),
                  pl.BlockSpec((N_META, ATT_W), mmap),
                  pl.BlockSpec(memory_space=pl.ANY)],
        out_specs=pl.BlockSpec((TQ, ATT_W), qmap),
        out_shape=jax.ShapeDtypeStruct(obuf.shape, F32),
        scratch_shapes=[pltpu.VMEM((TQ, ATT_W), BF16),
                        pltpu.VMEM((A_HEADS, TQ, 1), F32),
                        pltpu.VMEM((TQ, ATT_W), F32)],
        input_output_aliases={5: 0},
        compiler_params=_cparams(("parallel", "parallel", "arbitrary")),
        name="mla_flash",
    )(q, k, v, k, v, obuf)


def _meta_attn_kernel(q_ref, k_ref, v_ref, buf_ref, o_ref):
    del buf_ref
    n = q_ref.shape[0]
    row = lax.broadcasted_iota(jnp.int32, (n, n), 0)
    col = lax.broadcasted_iota(jnp.int32, (n, n), 1)
    for h in range(A_HEADS):
        sl = slice(h * HEAD_PAD, (h + 1) * HEAD_PAD)
        s = _bdot_nt(q_ref[:, sl], k_ref[:, sl]) * ATT_SCALE
        s = jnp.where(col <= row, s, -jnp.inf)
        p = jnp.exp(s - jnp.max(s, axis=1, keepdims=True))
        o_ref[:, sl] = _bdot(p, v_ref[:, sl]) / jnp.sum(p, axis=1, keepdims=True)


def _meta_attn(q, k, v, buf, meta_row):
    blk = meta_row // N_META
    spec = pl.BlockSpec((N_META, ATT_W), lambda i: (blk, 0))
    return pl.pallas_call(
        _meta_attn_kernel,
        grid=(1,),
        in_specs=[spec, spec, spec, pl.BlockSpec(memory_space=pl.ANY)],
        out_specs=spec,
        out_shape=jax.ShapeDtypeStruct(buf.shape, F32),
        input_output_aliases={3: 0},
        compiler_params=_cparams(("arbitrary",)),
        name="mla_meta_attn",
    )(q, k, v, buf)


PAGES_PER_STEP = 8
KV_ROPE0 = A_KVLORA + ROPE_LANE0


def _sattn_kernel(pt_ref, q_ref, kvn_ref, wuk_ref, wuv_ref, cache_ref, obuf_ref, o_ref, buf_ref, sem_ref, *, layer):
    del obuf_ref
    b = pl.program_id(0)
    nb = pl.num_programs(0)

    def page_copy(seq, p, slot):
        col = pl.multiple_of(p * PAGE_SIZE, PAGE_SIZE)
        return pltpu.make_async_copy(cache_ref.at[layer, pt_ref[seq, p]],
                                     buf_ref.at[slot, :, pl.ds(col, PAGE_SIZE)], sem_ref.at[slot])

    def start(seq, slot):
        def body(p, carry):
            page_copy(seq, p, slot).start()
            return carry
        lax.fori_loop(0, N_PAGES, body, 0)

    def wait(seq, slot):
        def body(p, carry):
            page_copy(seq, p, slot).wait()
            return carry
        lax.fori_loop(0, N_PAGES, body, 0)

    @pl.when(b == 0)
    def _first():
        start(0, 0)

    @pl.when(b + 1 < nb)
    def _next():
        start(b + 1, (b + 1) % 2)

    slot = b % 2
    wait(b, slot)

    q = q_ref[...]
    ql, qr = [], []
    for h in range(A_HEADS):
        ql.append(_bdot(q[:, h * HEAD_PAD:h * HEAD_PAD + A_NOPE], wuk_ref[h]))
        qr.append(q[:, h * HEAD_PAD + ROPE_LANE0:h * HEAD_PAD + ROPE_LANE0 + A_ROPE])
    ql = (jnp.concatenate(ql, axis=0) * QK_SCALE2).astype(BF16)
    qr = (jnp.concatenate(qr, axis=0) * QK_SCALE2).astype(BF16)
    rows = A_HEADS * S_ROWS
    keys = PAGES_PER_STEP * PAGE_SIZE

    m_prev = jnp.full((rows, 1), -jnp.inf, F32)
    l_prev = jnp.zeros((rows, 1), F32)
    acc = jnp.zeros((rows, A_KVLORA), F32)
    for c in range(N_PAGES // PAGES_PER_STEP):
        kc = buf_ref[slot, :, c * keys:(c + 1) * keys]
        ckv = kc[:A_KVLORA].astype(BF16)
        kr = kc[A_KVLORA:].astype(BF16)
        s = _bdot(ql, ckv) + _bdot(qr, kr)
        m_new = jnp.maximum(m_prev, jnp.max(s, axis=1, keepdims=True))
        alpha = jnp.exp2(m_prev - m_new)
        p = jnp.exp2(s - m_new)
        l_prev = alpha * l_prev + jnp.sum(p, axis=1, keepdims=True)
        acc = alpha * acc + _bdot_nt(p, ckv)
        m_prev = m_new

    kvn = kvn_ref[...]
    ckv_n = kvn[:, :A_KVLORA]
    kr_n = kvn[:, KV_ROPE0:KV_ROPE0 + A_ROPE]
    s = _bdot_nt(ql, ckv_n) + _bdot_nt(qr, kr_n)
    t_row = lax.broadcasted_iota(jnp.int32, (rows, S_ROWS), 0) % S_ROWS
    t_col = lax.broadcasted_iota(jnp.int32, (rows, S_ROWS), 1)
    s = jnp.where(t_col <= jnp.minimum(t_row, DEC_SEQ - 1), s, -jnp.inf)
    m_new = jnp.maximum(m_prev, jnp.max(s, axis=1, keepdims=True))
    alpha = jnp.exp2(m_prev - m_new)
    p = jnp.exp2(s - m_new)
    l_new = alpha * l_prev + jnp.sum(p, axis=1, keepdims=True)
    o_lat = (alpha * acc + _bdot(p, ckv_n)) / l_new
    for h in range(A_HEADS):
        o_ref[:, h * HEAD_PAD:(h + 1) * HEAD_PAD] = _bdot(o_lat[h * S_ROWS:(h + 1) * S_ROWS], wuv_ref[h])


def _sample_attn(page_table, q, kvrow, wuk_t, wuv_p, cache_t, obuf, layer):
    nb = page_table.shape[0]
    blk0 = ROW_S // S_ROWS
    grid_spec = pltpu.PrefetchScalarGridSpec(
        num_scalar_prefetch=1,
        grid=(nb,),
        in_specs=[pl.BlockSpec((S_ROWS, ATT_W), lambda b, pt: (blk0 + b, 0)),
                  pl.BlockSpec((S_ROWS, A_KVLORA + LANES), lambda b, pt: (blk0 + b, 0)),
                  pl.BlockSpec(wuk_t.shape, lambda b, pt: (0, 0, 0)),
                  pl.BlockSpec(wuv_p.shape, lambda b, pt: (0, 0, 0)),
                  pl.BlockSpec(memory_space=pl.ANY),
                  pl.BlockSpec(memory_space=pl.ANY)],
        out_specs=pl.BlockSpec((S_ROWS, ATT_W), lambda b, pt: (blk0 + b, 0)),
        scratch_shapes=[pltpu.VMEM((2, A_LAT, PAST_LEN), F32),
                        pltpu.SemaphoreType.DMA((2,))],
    )
    return pl.pallas_call(
        functools.partial(_sattn_kernel, layer=layer),
        grid_spec=grid_spec,
        out_shape=jax.ShapeDtypeStruct(obuf.shape, F32),
        input_output_aliases={6: 0},
        compiler_params=_cparams(("arbitrary",)),
        name="mla_sample_attn",
    )(page_table, q, kvrow, wuk_t, wuv_p, cache_t, obuf)


def _row_of(col_vec, eye):
    return jnp.sum(jnp.where(eye, col_vec, 0.0), axis=0, keepdims=True)


def _mlstm_kernel(g_ref, q_ref, k_ref, v_ref, og_ref, c0_ref, n0_ref, m0_ref, bi_ref, bf_ref, gn_ref, buf_ref,
                  o_ref, c_ref, n_ref, m_ref, c_scr, n_scr, m_scr, *, n_valid):
    del buf_ref
    ci = pl.program_id(1)
    L = q_ref.shape[0]

    @pl.when(ci == 0)
    def _load():
        c_scr[...] = c0_ref[0]
        n_scr[0:B_HEADS, :] = n0_ref[0]
        for h in range(B_HEADS):
            m_scr[h:h + 1, :] = jnp.broadcast_to(m0_ref[0, 0:1, h:h + 1], (1, LANES))

    row = lax.broadcasted_iota(jnp.int32, (L, L), 0)
    col = lax.broadcasted_iota(jnp.int32, (L, L), 1)
    eye = row == col
    tril = col <= row
    valid = lax.broadcasted_iota(jnp.int32, (L, 1), 0) < n_valid
    g = g_ref[...]
    lane = lax.broadcasted_iota(jnp.int32, (1, LANES), 1)
    m_out = jnp.zeros((1, LANES), F32)
    for h in range(B_HEADS):
        sl = slice(h * B_DK, (h + 1) * B_DK)
        q = q_ref[:, sl]
        k = k_ref[:, sl] * (B_DK ** -0.5)
        v = v_ref[:, sl]
        ig_col = jnp.where(valid, g[:, MI_LANE + h:MI_LANE + h + 1] + bi_ref[0:1, h:h + 1], -jnp.inf)
        fpre = g[:, MF_LANE + h:MF_LANE + h + 1] + bf_ref[0:1, h:h + 1]
        lf_col = jnp.where(valid, jnp.minimum(fpre, 0.0) - jnp.log1p(jnp.exp(-jnp.abs(fpre))), 0.0)
        ig_row = _row_of(ig_col, eye)
        lf_row = _row_of(lf_col, eye)
        b_col = jnp.sum(jnp.where(tril, lf_row, 0.0), axis=1, keepdims=True)
        b_row = jnp.sum(jnp.where(row <= col, lf_col, 0.0), axis=0, keepdims=True)
        m_prev = m_scr[h:h + 1, 0:1]
        c_prev = c_scr[h]
        n_prev = n_scr[h:h + 1, :]
        dmat = jnp.where(tril, b_col - b_row + ig_row, -jnp.inf)
        inter = b_col + m_prev
        m_t = jnp.maximum(inter, jnp.max(dmat, axis=1, keepdims=True))
        w_inter = jnp.exp(inter - m_t)
        s = _bdot_nt(q, k) * jnp.exp(dmat - m_t)
        num = w_inter * _bdot(q, c_prev) + _bdot(s, v)
        den = w_inter * jnp.sum(q * n_prev, axis=1, keepdims=True) + jnp.sum(s, axis=1, keepdims=True)
        hh = num / jnp.maximum(jnp.abs(den), jnp.exp(-m_t))
        m_new = m_t[n_valid - 1:n_valid, :]
        b_last = b_col[n_valid - 1:n_valid, :]
        w_end = jnp.exp(b_last - b_col + ig_col - m_new)
        decay = jnp.exp(b_last + m_prev - m_new)
        kw = k * w_end
        c_new = decay * c_prev + _bdot_tn(kw, v)
        n_new = decay * n_prev + jnp.sum(kw, axis=0, keepdims=True)
        c_scr[h] = c_new
        n_scr[h:h + 1, :] = n_new
        m_scr[h:h + 1, :] = jnp.broadcast_to(m_new, (1, LANES))
        m_out = jnp.where(lane == h, m_new, m_out)
        o_ref[:, sl] = _head_norm(hh, gn_ref[0:1, sl]) * _sigmoid(og_ref[:, sl])

    @pl.when(ci == pl.num_programs(1) - 1)
    def _store():
        c_ref[0] = c_scr[...]
        n_ref[0] = n_scr[0:B_HEADS, :]
        m_ref[0] = m_out


def _stream_out(obuf, n_in, w, row0, nc, L):
    blk0 = row0 // L
    spec = pl.BlockSpec((L, w), lambda b, c: (blk0 + b * nc + c, 0))
    return spec, [pl.BlockSpec(memory_space=pl.ANY)], [obuf], {n_in: 0}


def _mlstm(proj, row0, nb, nc, L, n_valid, c0, n0, m0, bi, bf, gn, shared_state, obuf):
    blk0 = row0 // L
    rmap = lambda cb: (lambda b, c: (blk0 + b * nc + c, cb))
    smap = (lambda b, c: (0, 0, 0, 0)) if shared_state else (lambda b, c: (b, 0, 0, 0))
    smap3 = (lambda b, c: (0, 0, 0)) if shared_state else (lambda b, c: (b, 0, 0))
    w = B_HEADS * B_DK
    full2 = lambda a: pl.BlockSpec(a.shape, lambda b, c: (0, 0))
    ospec, xspec, xarg, alias = _stream_out(obuf, 11, w, row0, nc, L)
    return pl.pallas_call(
        functools.partial(_mlstm_kernel, n_valid=n_valid),
        grid=(nb, nc),
        in_specs=[pl.BlockSpec((L, LANES), rmap(GATE_COL // LANES)),
                  pl.BlockSpec((L, w), rmap(2)), pl.BlockSpec((L, w), rmap(3)),
                  pl.BlockSpec((L, w), rmap(4)), pl.BlockSpec((L, w), rmap(5)),
                  pl.BlockSpec((1, B_HEADS, B_DK, B_DV), smap),
                  pl.BlockSpec((1, B_HEADS, B_DK), smap3),
                  pl.BlockSpec((1, 1, B_HEADS), smap3),
                  full2(bi), full2(bf), full2(gn)] + xspec,
        out_specs=[ospec,
                   pl.BlockSpec((1, B_HEADS, B_DK, B_DV), lambda b, c: (b, 0, 0, 0)),
                   pl.BlockSpec((1, B_HEADS, B_DK), lambda b, c: (b, 0, 0)),
                   pl.BlockSpec((1, 1, LANES), lambda b, c: (b, 0, 0))],
        out_shape=[jax.ShapeDtypeStruct((proj.shape[0], w), F32),
                   jax.ShapeDtypeStruct((nb, B_HEADS, B_DK, B_DV), F32),
                   jax.ShapeDtypeStruct((nb, B_HEADS, B_DK), F32),
                   jax.ShapeDtypeStruct((nb, 1, LANES), F32)],
        scratch_shapes=[pltpu.VMEM((B_HEADS, B_DK, B_DV), F32),
                        pltpu.VMEM((SUBLANES, B_DK), F32),
                        pltpu.VMEM((SUBLANES, LANES), F32)],
        input_output_aliases=alias,
        compiler_params=_cparams(("parallel", "arbitrary")),
        name="mlstm",
    )(proj, proj, proj, proj, proj, c0, n0, m0, bi, bf, gn, *xarg)


RET_LOG_GAMMA = tuple(math.log1p(-(2.0 ** (-5.0 - h))) for h in range(D_HEADS))


def _ret_kernel(q_ref, k_ref, v_ref, g_ref, cr_ref, sr_ref, s0_ref, gn_ref, buf_ref, o_ref, s_ref, s_scr, *,
                n_valid):
    del buf_ref
    ci = pl.program_id(1)
    L = q_ref.shape[0]

    @pl.when(ci == 0)
    def _load():
        s_scr[...] = s0_ref[0]

    row = lax.broadcasted_iota(jnp.int32, (L, L), 0)
    col = lax.broadcasted_iota(jnp.int32, (L, L), 1)
    rel = (row - col).astype(F32)
    t_col = lax.broadcasted_iota(jnp.int32, (L, 1), 0)
    tf = t_col.astype(F32)
    cr, sr = cr_ref[...], sr_ref[...]
    half = D_DK // 2
    for h in range(D_HEADS):
        lg = RET_LOG_GAMMA[h]
        sl = slice(h * D_DK, (h + 1) * D_DK)
        q = q_ref[:, sl]
        k = k_ref[:, sl]
        q = q * cr + pltpu.roll(q, half, 1) * sr
        k = (k * cr + pltpu.roll(k, half, 1) * sr) * (D_DK ** -0.5)
        v = v_ref[:, sl]
        s_prev = s_scr[h]
        dec = jnp.where(rel >= 0, jnp.exp(lg * jnp.maximum(rel, 0.0)), 0.0)
        inner = _bdot_nt(q, k) * dec
        o = _bdot(inner, v) + jnp.exp((tf + 1.0) * lg) * _bdot(q, s_prev)
        w_end = jnp.where(t_col < n_valid, jnp.exp((n_valid - 1.0 - tf) * lg), 0.0)
        s_scr[h] = math.exp(n_valid * lg) * s_prev + _bdot_tn(k * w_end, v)
        gate = g_ref[:, sl]
        o_ref[:, sl] = _head_norm(o, gn_ref[0:1, sl]) * (gate * _sigmoid(gate))

    @pl.when(ci == pl.num_programs(1) - 1)
    def _store():
        s_ref[0] = s_scr[...]


def _retention(proj, tabs, row0, nb, nc, L, n_valid, s0, gn, shared_state, obuf):
    blk0 = row0 // L
    rmap = lambda cb: (lambda b, c: (blk0 + b * nc + c, cb))
    smap = (lambda b, c: (0, 0, 0, 0)) if shared_state else (lambda b, c: (b, 0, 0, 0))
    w = D_HEADS * D_DK
    ospec, xspec, xarg, alias = _stream_out(obuf, 8, w, row0, nc, L)
    return pl.pallas_call(
        functools.partial(_ret_kernel, n_valid=n_valid),
        grid=(nb, nc),
        in_specs=[pl.BlockSpec((L, w), rmap(2)), pl.BlockSpec((L, w), rmap(3)),
                  pl.BlockSpec((L, w), rmap(4)), pl.BlockSpec((L, w), rmap(5)),
                  pl.BlockSpec((L, LANES), rmap(0)), pl.BlockSpec((L, LANES), rmap(0)),
                  pl.BlockSpec((1, D_HEADS, D_DK, D_DV), smap),
                  pl.BlockSpec(gn.shape, lambda b, c: (0, 0))] + xspec,
        out_specs=[ospec,
                   pl.BlockSpec((1, D_HEADS, D_DK, D_DV), lambda b, c: (b, 0, 0, 0))],
        out_shape=[jax.ShapeDtypeStruct((proj.shape[0], w), F32),
                   jax.ShapeDtypeStruct((nb, D_HEADS, D_DK, D_DV), F32)],
        scratch_shapes=[pltpu.VMEM((D_HEADS, D_DK, D_DV), F32)],
        input_output_aliases=alias,
        compiler_params=_cparams(("parallel", "arbitrary")),
        name="retention",
    )(proj, proj, proj, proj, tabs[0], tabs[1], s0, gn, *xarg)


def _rglru_kernel(x_ref, gr_ref, h0_ref, cv0_ref, cw_ref, cb_ref, wa_ref, ba_ref, wi_ref, bi_ref, lam_ref, buf_ref,
                  o_ref, h_ref, cv_ref, xp_scr, h_scr, *, n_valid):
    del buf_ref
    ci = pl.program_id(1)
    L = x_ref.shape[0]
    tail = C_CONV - 1

    @pl.when(ci == 0)
    def _load():
        xp_scr[SUBLANES - tail:SUBLANES, :] = cv0_ref[0]
        h_scr[0:1, :] = h0_ref[0]

    xp_scr[SUBLANES:SUBLANES + L, :] = x_ref[...]
    u = cb_ref[...] + sum(xp_scr[SUBLANES - tail + j:SUBLANES - tail + j + L, :] * cw_ref[j:j + 1, :]
                          for j in range(C_CONV))
    new_tail = xp_scr[SUBLANES + n_valid - tail:SUBLANES + n_valid, :]
    xp_scr[SUBLANES - tail:SUBLANES, :] = new_tail
    r = _sigmoid(_bdot(u, wa_ref[...]) + ba_ref[...])
    i = _sigmoid(_bdot(u, wi_ref[...]) + bi_ref[...])
    nlam = -lam_ref[...]
    softplus = jnp.maximum(nlam, 0.0) + jnp.log1p(jnp.exp(-jnp.abs(nlam)))
    log_a = -C_POW * r * softplus
    a = jnp.exp(log_a)
    b = jnp.sqrt(-jnp.tanh(log_a) * (a * a + 1.0)) * (i * u)
    t_idx = lax.broadcasted_iota(jnp.int32, (L, 1), 0)
    step = 1
    while step < L:
        keep = t_idx >= step
        a_sh = jnp.where(keep, pltpu.roll(a, step, 0), 1.0)
        b_sh = jnp.where(keep, pltpu.roll(b, step, 0), 0.0)
        b = a * b_sh + b
        a = a * a_sh
        step *= 2
    hseq = a * h_scr[0:1, :] + b
    h_last = hseq[n_valid - 1:n_valid, :]
    h_scr[0:1, :] = h_last
    gr = gr_ref[...]
    gelu = 0.5 * gr * (1.0 + jnp.tanh(math.sqrt(2.0 / math.pi) * (gr + 0.044715 * (gr * gr * gr))))
    o_ref[...] = hseq * gelu

    @pl.when(ci == pl.num_programs(1) - 1)
    def _store():
        h_ref[0] = h_last
        cv_ref[0] = new_tail


def _rglru(proj, row0, nb, nc, L, n_valid, h0, cv0, cw, cb, wa, ba, wi, bi, lam, shared_state, obuf):
    blk0 = row0 // L
    rmap = lambda cb_: (lambda b, c: (blk0 + b * nc + c, cb_))
    smap = (lambda b, c: (0, 0, 0)) if shared_state else (lambda b, c: (b, 0, 0))
    full2 = lambda a: pl.BlockSpec(a.shape, lambda b, c: (0, 0))
    w = C_WIDTH
    ospec, xspec, xarg, alias = _stream_out(obuf, 11, w, row0, nc, L)
    return pl.pallas_call(
        functools.partial(_rglru_kernel, n_valid=n_valid),
        grid=(nb, nc),
        in_specs=[pl.BlockSpec((L, w), rmap(0)), pl.BlockSpec((L, w), rmap(1)),
                  pl.BlockSpec((1, 1, w), smap), pl.BlockSpec((1, C_CONV - 1, w), smap),
                  full2(cw), full2(cb), full2(wa), full2(ba), full2(wi), full2(bi), full2(lam)] + xspec,
        out_specs=[ospec,
                   pl.BlockSpec((1, 1, w), lambda b, c: (b, 0, 0)),
                   pl.BlockSpec((1, C_CONV - 1, w), lambda b, c: (b, 0, 0))],
        out_shape=[jax.ShapeDtypeStruct((proj.shape[0], w), F32),
                   jax.ShapeDtypeStruct((nb, 1, w), F32),
                   jax.ShapeDtypeStruct((nb, C_CONV - 1, w), F32)],
        scratch_shapes=[pltpu.VMEM((SUBLANES + L, w), F32), pltpu.VMEM((SUBLANES, w), F32)],
        input_output_aliases=alias,
        compiler_params=_cparams(("parallel", "arbitrary")),
        name="rglru",
    )(proj, proj, h0, cv0, cw, cb, wa, ba, wi, bi, lam, *xarg)


def _outproj_kernel(o1_ref, o2_ref, w1_ref, w2_ref, x_ref, g_ref, b_ref, y_ref):
    mix = _bdot(o1_ref[...], w1_ref[...]) + _bdot(o2_ref[...], w2_ref[...])
    y_ref[...] = _layer_norm(DN_ALPHA * x_ref[...] + mix, g_ref[...], b_ref[...])


def _outproj_ln(o1, o2, w1, w2, x, g, b):
    n = x.shape[0]
    row = lambda a: pl.BlockSpec((TM, a.shape[1]), lambda i: (i, 0))
    full = lambda a: pl.BlockSpec(a.shape, lambda i: (0, 0))
    return pl.pallas_call(
        _outproj_kernel,
        grid=(n // TM,),
        in_specs=[row(o1), row(o2), full(w1), full(w2), row(x), full(g), full(b)],
        out_specs=row(x),
        out_shape=jax.ShapeDtypeStruct(x.shape, F32),
        compiler_params=_cparams(("parallel",)),
        name="outproj_ln",
    )(o1, o2, w1, w2, x, g, b)


ROUTER_E0 = N_GROUPS


def _moe_kernel(x_ref, wr_ref, br_ref, w1_ref, w3_ref, w2_ref, g_ref, b_ref, y_ref, xb_scr, gate_scr, acc_scr):
    j = pl.program_id(1)

    @pl.when(j == 0)
    def _route():
        x = x_ref[...]
        xb_scr[...] = x.astype(BF16)
        logits = jnp.dot(x, wr_ref[...], precision=lax.Precision.HIGHEST,
                         preferred_element_type=F32) + br_ref[...]
        lane = lax.broadcasted_iota(jnp.int32, logits.shape, 1)
        gl = jnp.where(lane < N_GROUPS, logits, -jnp.inf)
        gmax = jnp.max(gl, axis=1, keepdims=True)
        g_gate = 1.0 / jnp.sum(jnp.exp(gl - gmax), axis=1, keepdims=True)
        g_idx = jnp.min(jnp.where(gl == gmax, lane, LANES), axis=1, keepdims=True)
        e_lo = ROUTER_E0 + g_idx * E_PER_GROUP
        el = jnp.where(jnp.logical_and(lane >= e_lo, lane < e_lo + E_PER_GROUP), logits, -jnp.inf)
        e1 = jnp.max(el, axis=1, keepdims=True)
        i1 = jnp.min(jnp.where(el == e1, lane, LANES), axis=1, keepdims=True)
        el2 = jnp.where(lane == i1, -jnp.inf, el)
        e2 = jnp.max(el2, axis=1, keepdims=True)
        i2 = jnp.min(jnp.where(el2 == e2, lane, LANES), axis=1, keepdims=True)
        r = jnp.exp(e2 - e1)
        w1 = g_gate / (1.0 + r)
        gate = jnp.where(lane == i1, w1, 0.0) + jnp.where(lane == i2, w1 * r, 0.0)
        for gi in range(N_GROUPS):
            lo = ROUTER_E0 + gi * E_PER_GROUP
            gate_scr[gi] = gate[:, lo:lo + E_PER_GROUP]
        acc_scr[...] = jnp.zeros_like(acc_scr)

    xb = xb_scr[...]
    h1 = jnp.dot(xb, w1_ref[0], preferred_element_type=F32)
    h3 = jnp.dot(xb, w3_ref[0], preferred_element_type=F32)
    hg = h1 * _sigmoid(h1) * h3
    gate = gate_scr[j]
    parts = []
    for e in range(E_PER_GROUP):
        sl = slice(e * D_EXPERT, (e + 1) * D_EXPERT)
        parts.append((hg[:, sl] * gate[:, e:e + 1]).astype(BF16))
    acc_scr[...] += jnp.dot(jnp.concatenate(parts, axis=1), w2_ref[0].astype(BF16), preferred_element_type=F32)

    @pl.when(j == pl.num_programs(1) - 1)
    def _fin():
        y_ref[...] = _layer_norm(DN_ALPHA * x_ref[...] + acc_scr[...], g_ref[...], b_ref[...])


def _moe_ln(x, wr, br, w1, w3, w2, g, b):
    n = x.shape[0]
    gw = E_PER_GROUP * D_EXPERT
    row = pl.BlockSpec((TM, D_MODEL), lambda i, j: (i, 0))
    full = lambda a: pl.BlockSpec(a.shape, lambda i, j: (0, 0))
    return pl.pallas_call(
        _moe_kernel,
        grid=(n // TM, N_GROUPS),
        in_specs=[row, full(wr), full(br),
                  pl.BlockSpec((1, D_MODEL, gw), lambda i, j: (j, 0, 0)),
                  pl.BlockSpec((1, D_MODEL, gw), lambda i, j: (j, 0, 0)),
                  pl.BlockSpec((1, gw, D_MODEL), lambda i, j: (j, 0, 0)),
                  full(g), full(b)],
        out_specs=row,
        out_shape=jax.ShapeDtypeStruct(x.shape, F32),
        scratch_shapes=[pltpu.VMEM((TM, D_MODEL), BF16),
                        pltpu.VMEM((N_GROUPS, TM, E_PER_GROUP), F32),
                        pltpu.VMEM((TM, D_MODEL), F32)],
        compiler_params=_cparams(("parallel", "arbitrary")),
        name="moe_ln",
    )(x, wr, br, w1, w3, w2, g, b)


def _positions():
    pos = jnp.zeros((N_TOK,), F32)
    main = (N_META + jnp.arange(SEQ, dtype=jnp.int32)).astype(F32)
    pos = pos.at[:N_MAIN].set(jnp.tile(main, BATCH))
    samp = (PAST_LEN + jnp.minimum(jnp.arange(S_ROWS, dtype=jnp.int32), DEC_SEQ - 1)).astype(F32)
    pos = pos.at[ROW_S:ROW_M].set(jnp.tile(samp, DEC_BATCH))
    pos = pos.at[ROW_M:ROW_M + N_META].set(jnp.arange(N_META, dtype=jnp.int32).astype(F32))
    return pos


def _rope_tables_mla(pos):
    half = A_ROPE // 2
    freqs = ROPE_THETA ** (-jnp.arange(half, dtype=F32) / half)
    ang = pos[:, None] * freqs
    cos, sin = jnp.cos(ang), jnp.sin(ang)
    n = pos.shape[0]
    zeros = lambda w: jnp.zeros((n, w), F32)
    rest = LANES - ROPE_LANE0 - A_ROPE
    c = jnp.concatenate([jnp.ones((n, ROPE_LANE0), F32), cos, cos, zeros(rest)], axis=1)
    s1 = jnp.concatenate([zeros(ROPE_LANE0 + half), sin, zeros(rest)], axis=1)
    s2 = jnp.concatenate([zeros(ROPE_LANE0), -sin, zeros(half + rest)], axis=1)
    return c, s1, s2


def _rope_tables_ret(pos):
    half = D_DK // 2
    freqs = ROPE_THETA ** (-jnp.arange(half, dtype=F32) / half)
    ang = pos[:, None] * freqs
    cos, sin = jnp.cos(ang), jnp.sin(ang)
    return jnp.concatenate([cos, cos], axis=1), jnp.concatenate([-sin, sin], axis=1)


def _pad_cols(a, w):
    return jnp.pad(a, ((0, 0), (0, w - a.shape[1])))


def _even_in_weight(w):
    cuts = [0, A_QLORA, A_QLORA + A_KVLORA, A_QLORA + A_KVLORA + A_ROPE]
    c_q, c_kv, k_r = (w[:, cuts[i]:cuts[i + 1]] for i in range(3))
    m0 = cuts[3]
    mw = B_HEADS * B_DK
    big = w[:, m0:m0 + 4 * mw]
    gates = w[:, m0 + 4 * mw:]
    head = jnp.concatenate([c_q, c_kv, jnp.zeros((D_MODEL, ROPE_LANE0), w.dtype), k_r, gates], axis=1)
    return jnp.concatenate([_pad_cols(head, 1024), big], axis=1).astype(BF16)


def _head_pad_cols(a, width):
    k = a.shape[0]
    return jnp.pad(a, ((0, 0), (0, 0), (0, HEAD_PAD - width))).reshape(k, A_HEADS * HEAD_PAD)


def _block_diag(w):
    eye = jnp.eye(C_BLOCKS, dtype=w.dtype)
    return (w[:, :, None, :] * eye[:, None, :, None]).reshape(C_WIDTH, C_WIDTH)


def _assemble(main, samp, meta):
    w = main.shape[1]
    pad = jnp.zeros((N_TOK - ROW_M - N_META, w), main.dtype)
    return jnp.concatenate([main, samp, meta, pad], axis=0)


def kernel(x_prompt, x_sample, cache_mla, page_table, state_mlstm_c, state_mlstm_n, state_mlstm_m,
           state_rglru_h, state_rglru_conv, state_ret, meta_tokens, w_in_e, g_cq, g_ckv, w_uq, w_uk, w_uv,
           b_mi, b_mf, g_mlstm, w_out_e, w_in_o, conv_w, conv_b, w_ra, b_ra, w_ri, b_ri, lam, g_ret, w_out_o,
           ln_mix_g, ln_mix_b, ln_ffn_g, ln_ffn_b, w_gr, b_gr, w_er, b_er, w_e1, w_e3, w_e2):
    nc_main = SEQ // CHUNK
    xs = jnp.pad(x_sample, ((0, 0), (0, S_ROWS - DEC_SEQ), (0, 0))).reshape(N_SAMP, D_MODEL)
    x = _assemble(x_prompt.reshape(N_MAIN, D_MODEL), xs, meta_tokens)

    cache_t = jnp.swapaxes(cache_mla, 2, 3)

    pos = _positions()
    tabs_mla = _rope_tables_mla(pos)
    tabs_ret = _rope_tables_ret(pos)
    row2 = lambda a: a.reshape(1, -1)
    zbuf = lambda w: jnp.zeros((N_TOK, w), F32)

    kv_p, kv_s, c_p, c_s, n_p, n_s, m_p, m_s = [], [], [], [], [], [], [], []
    h_p, h_s, cv_p, cv_s, s_p, s_s = [], [], [], [], [], []

    for l in range(DEPTH):
        j = l // 2
        if l % 2 == 0:
            proj = _matmul(x, _even_in_weight(w_in_e[j]))
            wuq = _head_pad_cols(w_uq[j].reshape(A_QLORA, A_HEADS, A_NOPE + A_ROPE), A_NOPE + A_ROPE).astype(BF16)
            wk = _head_pad_cols(w_uk[j], A_NOPE).astype(BF16)
            wv = _head_pad_cols(w_uv[j], A_V).astype(BF16)
            q, kp, vp, kvrow = _mla_prep(proj, tabs_mla, row2(g_cq[j]), row2(g_ckv[j]), wuq, wk, wv)
            wuk_t = jnp.transpose(w_uk[j], (1, 2, 0)).astype(BF16)
            wuv_p = jnp.pad(jnp.transpose(w_uv[j], (1, 0, 2)),
                            ((0, 0), (0, 0), (0, HEAD_PAD - A_V))).astype(BF16)
            o1 = _flash(q, kp, vp, zbuf(ATT_W), BATCH, SEQ, ROW_M)
            o1 = _sample_attn(page_table, q, kvrow, wuk_t, wuv_p, cache_t, o1, j)
            o1 = _meta_attn(q, kp, vp, o1, ROW_M)

            bi, bf, gn = row2(b_mi[j]), row2(b_mf[j]), row2(g_mlstm[j])
            zc = jnp.zeros((1, B_HEADS, B_DK, B_DV), F32)
            zn = jnp.zeros((1, B_HEADS, B_DK), F32)
            zm = jnp.zeros((1, 1, B_HEADS), F32)
            o2, c_m, n_m, m_m = _mlstm(proj, ROW_M, 1, 1, N_META, N_META, zc, zn, zm, bi, bf, gn, True,
                                       zbuf(B_HEADS * B_DV))
            o2, c_pm, n_pm, m_pm = _mlstm(proj, 0, BATCH, nc_main, CHUNK, CHUNK, c_m, n_m,
                                          m_m[:, :, :B_HEADS], bi, bf, gn, True, o2)
            o2, c_sm, n_sm, m_sm = _mlstm(proj, ROW_S, DEC_BATCH, 1, S_ROWS, DEC_SEQ, state_mlstm_c[j],
                                          state_mlstm_n[j], state_mlstm_m[j].reshape(DEC_BATCH, 1, B_HEADS),
                                          bi, bf, gn, False, o2)

            w1 = jnp.pad(w_out_e[j][:A_HEADS * A_V].reshape(A_HEADS, A_V, D_MODEL),
                         ((0, 0), (0, HEAD_PAD - A_V), (0, 0))).reshape(ATT_W, D_MODEL).astype(BF16)
            w2 = w_out_e[j][A_HEADS * A_V:].astype(BF16)

            kv_main = kvrow[:N_MAIN].reshape(BATCH, SEQ, -1)
            kv_meta = jnp.broadcast_to(kvrow[ROW_M:ROW_M + N_META][None], (BATCH, N_META, kvrow.shape[1]))
            kv_all = jnp.concatenate([kv_meta, kv_main], axis=1)
            kv_p.append(jnp.concatenate([kv_all[..., :A_KVLORA], kv_all[..., KV_ROPE0:KV_ROPE0 + A_ROPE]], -1))
            kv_sr = kvrow[ROW_S:ROW_M].reshape(DEC_BATCH, S_ROWS, -1)[:, :DEC_SEQ]
            kv_s.append(jnp.concatenate([kv_sr[..., :A_KVLORA], kv_sr[..., KV_ROPE0:KV_ROPE0 + A_ROPE]], -1))
            c_p.append(c_pm); n_p.append(n_pm); m_p.append(m_pm[:, 0, :B_HEADS])
            c_s.append(c_sm); n_s.append(n_sm); m_s.append(m_sm[:, 0, :B_HEADS])
        else:
            proj = _matmul(x, w_in_o[j].astype(BF16))
            cw, cb = conv_w[j], row2(conv_b[j])
            wa, wi = _block_diag(w_ra[j]).astype(BF16), _block_diag(w_ri[j]).astype(BF16)
            ba, bi_, lam_ = row2(b_ra[j]), row2(b_ri[j]), row2(lam[j])
            zh = jnp.zeros((1, 1, C_WIDTH), F32)
            zcv = jnp.zeros((1, C_CONV - 1, C_WIDTH), F32)
            args = (cw, cb, wa, ba, wi, bi_, lam_)
            o1, h_m, cv_m = _rglru(proj, ROW_M, 1, 1, N_META, N_META, zh, zcv, *args, True, zbuf(C_WIDTH))
            o1, h_pm, cv_pm = _rglru(proj, 0, BATCH, nc_main, CHUNK, CHUNK, h_m, cv_m, *args, True, o1)
            o1, h_sm, cv_sm = _rglru(proj, ROW_S, DEC_BATCH, 1, S_ROWS, DEC_SEQ,
                                     state_rglru_h[j].reshape(DEC_BATCH, 1, C_WIDTH), state_rglru_conv[j],
                                     *args, False, o1)

            gn = row2(g_ret[j])
            zs = jnp.zeros((1, D_HEADS, D_DK, D_DV), F32)
            o2, s_m = _retention(proj, tabs_ret, ROW_M, 1, 1, N_META, N_META, zs, gn, True, zbuf(D_HEADS * D_DV))
            o2, s_pm = _retention(proj, tabs_ret, 0, BATCH, nc_main, CHUNK, CHUNK, s_m, gn, True, o2)
            o2, s_sm = _retention(proj, tabs_ret, ROW_S, DEC_BATCH, 1, S_ROWS, DEC_SEQ, state_ret[j], gn, False, o2)

            w1 = w_out_o[j][:C_WIDTH].astype(BF16)
            w2 = w_out_o[j][C_WIDTH:].astype(BF16)
            h_p.append(h_pm[:, 0]); cv_p.append(cv_pm); s_p.append(s_pm)
            h_s.append(h_sm[:, 0]); cv_s.append(cv_sm); s_s.append(s_sm)

        x = _outproj_ln(o1, o2, w1, w2, x, row2(ln_mix_g[l]), row2(ln_mix_b[l]))

        wr = jnp.concatenate([w_gr[l], jnp.transpose(w_er[l], (1, 0, 2)).reshape(D_MODEL, N_EXPERTS)], axis=1)
        br = jnp.concatenate([b_gr[l], b_er[l].reshape(N_EXPERTS)])
        gw = E_PER_GROUP * D_EXPERT
        w1e = jnp.transpose(w_e1[l].reshape(N_GROUPS, E_PER_GROUP, D_MODEL, D_EXPERT),
                            (0, 2, 1, 3)).reshape(N_GROUPS, D_MODEL, gw).astype(BF16)
        w3e = jnp.transpose(w_e3[l].reshape(N_GROUPS, E_PER_GROUP, D_MODEL, D_EXPERT),
                            (0, 2, 1, 3)).reshape(N_GROUPS, D_MODEL, gw).astype(BF16)
        w2e = w_e2[l].reshape(N_GROUPS, gw, D_MODEL)
        x = _moe_ln(x, _pad_cols(wr, LANES), _pad_cols(row2(br), LANES), w1e, w3e, w2e,
                    row2(ln_ffn_g[l]), row2(ln_ffn_b[l]))

    y_prompt = x[:N_MAIN].reshape(BATCH, SEQ, D_MODEL)
    y_sample = x[ROW_S:ROW_M].reshape(DEC_BATCH, S_ROWS, D_MODEL)[:, :DEC_SEQ]
    st = jnp.stack
    return (y_prompt, y_sample, st(kv_p), st(kv_s), st(c_p), st(c_s), st(n_p), st(n_s), st(m_p), st(m_s),
            st(h_p), st(h_s), st(cv_p), st(cv_s), st(s_p), st(s_s))
```

```python
import functools
import math

import jax
import jax.numpy as jnp
from jax import lax
from jax.experimental import pallas as pl
from jax.experimental.pallas import tpu as pltpu

F32 = jnp.float32
BF16 = jnp.bfloat16

D_MODEL = 1024
BATCH = 4
SEQ = 4096
DEPTH = 4
DEC_BATCH = 128
DEC_SEQ = 4
PAST_LEN = 8192
PAGE_SIZE = 128
N_PAGES = PAST_LEN // PAGE_SIZE
N_META = 16
N_EVEN = (DEPTH + 1) // 2
N_ODD = DEPTH // 2
CHUNK = 128
ROPE_THETA = 10000.0
A_HEADS = 8
A_NOPE = 64
A_ROPE = 32
A_V = 64
A_QLORA = 384
A_KVLORA = 256
A_LAT = A_KVLORA + A_ROPE
B_HEADS = 4
B_DK = 128
B_DV = 128
C_WIDTH = 512
C_BLOCKS = 8
C_BW = C_WIDTH // C_BLOCKS
C_CONV = 4
C_POW = 8.0
D_HEADS = 4
D_DK = 128
D_DV = 128
N_GROUPS = 4
E_PER_GROUP = 8
N_EXPERTS = N_GROUPS * E_PER_GROUP
D_EXPERT = 128
DN_ALPHA = (2.0 * DEPTH) ** 0.25
EPS = 1e-5

LANES = 128
SUBLANES = 8
VMEM_LIMIT = 48 * 1024 * 1024

S_ROWS = 8
N_MAIN = BATCH * SEQ
ROW_S = N_MAIN
N_SAMP = DEC_BATCH * S_ROWS
ROW_M = ROW_S + N_SAMP
TM = 512
N_TOK = -(-(ROW_M + N_META) // TM) * TM

HEAD_PAD = LANES
ATT_W = A_HEADS * HEAD_PAD
ROPE_LANE0 = A_NOPE
PROJ_E_W = 3072
GATE_COL = 640
MI_LANE = ROPE_LANE0 + A_ROPE
MF_LANE = MI_LANE + B_HEADS

TQ = 256
TK = 512
SEQS_PER_STEP = 4
SEQS_PER_STEP_RGLRU = 16


def _cparams(sem):
    return pltpu.CompilerParams(dimension_semantics=sem, vmem_limit_bytes=VMEM_LIMIT)


def _bdot(a, b):
    return jnp.dot(a.astype(BF16), b.astype(BF16), preferred_element_type=F32)


def _bdot_nt(a, b):
    return lax.dot_general(a.astype(BF16), b.astype(BF16), (((1,), (1,)), ((), ())),
                           preferred_element_type=F32)


def _bdot_tn(a, b):
    return lax.dot_general(a.astype(BF16), b.astype(BF16), (((0,), (0,)), ((), ())),
                           preferred_element_type=F32)


def _sigmoid(x):
    return 1.0 / (1.0 + jnp.exp(-x))


def _layer_norm(x, g, b):
    mu = jnp.mean(x, axis=-1, keepdims=True)
    xc = x - mu
    var = jnp.mean(xc * xc, axis=-1, keepdims=True)
    return xc * lax.rsqrt(var + EPS) * g + b


def _head_norm(x, g):
    mu = jnp.mean(x, axis=-1, keepdims=True)
    xc = x - mu
    var = jnp.mean(xc * xc, axis=-1, keepdims=True)
    return xc * lax.rsqrt(var + EPS) * g


def _mm_kernel(x_ref, w_ref, o_ref):
    o_ref[...] = _bdot(x_ref[...], w_ref[...])


def _matmul(x, w):
    n, k = x.shape
    m = w.shape[1]
    return pl.pallas_call(
        _mm_kernel,
        grid=(n // TM,),
        in_specs=[pl.BlockSpec((TM, k), lambda i: (i, 0)),
                  pl.BlockSpec((k, m), lambda i: (0, 0))],
        out_specs=pl.BlockSpec((TM, m), lambda i: (i, 0)),
        out_shape=jax.ShapeDtypeStruct((n, m), F32),
        compiler_params=_cparams(("parallel",)),
        name="proj_in",
    )(x, w)


def _rope_tile(x, c, s1, s2):
    half = A_ROPE // 2
    return x * c + pltpu.roll(x, half, 1) * s1 + pltpu.roll(x, LANES - half, 1) * s2


def _mla_prep_kernel(a_ref, c_ref, s1_ref, s2_ref, gq_ref, gkv_ref, wuq_ref, wk_ref, wv_ref,
                     q_ref, k_ref, v_ref, kv_ref):
    a = a_ref[...]
    c, s1, s2 = c_ref[...], s1_ref[...], s2_ref[...]
    cq = a[:, :A_QLORA]
    ckv = a[:, A_QLORA:A_QLORA + A_KVLORA]
    cq = cq * lax.rsqrt(jnp.mean(cq * cq, axis=-1, keepdims=True) + EPS) * gq_ref[...]
    ckv = ckv * lax.rsqrt(jnp.mean(ckv * ckv, axis=-1, keepdims=True) + EPS) * gkv_ref[...]
    kr = _rope_tile(a[:, GATE_COL:GATE_COL + LANES], c, s1, s2)
    kv_ref[:, :A_KVLORA] = ckv
    kv_ref[:, A_KVLORA:] = kr
    q = _bdot(cq, wuq_ref[...])
    kn = _bdot(ckv, wk_ref[...])
    lane = lax.broadcasted_iota(jnp.int32, (1, ATT_W), 1)
    ones_col = jnp.where(lane % HEAD_PAD == A_V, 1.0, 0.0)
    v_ref[...] = (_bdot(ckv, wv_ref[...]) + ones_col).astype(BF16)
    for h in range(A_HEADS):
        sl = slice(h * HEAD_PAD, (h + 1) * HEAD_PAD)
        q_ref[:, sl] = _rope_tile(q[:, sl], c, s1, s2)
        k_ref[:, sl] = (kn[:, sl] + kr).astype(BF16)


def _mla_prep(proj, tabs, gq, gkv, wuq, wk, wv):
    n = proj.shape[0]
    row = lambda w: pl.BlockSpec((TM, w), lambda i: (i, 0))
    full = lambda a: pl.BlockSpec(a.shape, lambda i: (0,) * a.ndim)
    return pl.pallas_call(
        _mla_prep_kernel,
        grid=(n // TM,),
        in_specs=[row(GATE_COL + LANES), row(LANES), row(LANES), row(LANES),
                  full(gq), full(gkv), full(wuq), full(wk), full(wv)],
        out_specs=[row(ATT_W), row(ATT_W), row(ATT_W), row(A_KVLORA + LANES)],
        out_shape=[jax.ShapeDtypeStruct((n, ATT_W), F32),
                   jax.ShapeDtypeStruct((n, ATT_W), BF16),
                   jax.ShapeDtypeStruct((n, ATT_W), BF16),
                   jax.ShapeDtypeStruct((n, A_KVLORA + LANES), F32)],
        compiler_params=_cparams(("parallel",)),
        name="mla_prep",
    )(proj, *tabs, gq, gkv, wuq, wk, wv)


ATT_SCALE = (A_NOPE + A_ROPE) ** -0.5


QK_SCALE2 = ATT_SCALE * math.log2(math.e)


def _flash_kernel(q_ref, k_ref, v_ref, km_ref, vm_ref, buf_ref, o_ref, qs_ref, *head_scr):
    del buf_ref
    m_refs, acc_refs = head_scr[:A_HEADS], head_scr[A_HEADS:]
    qi = pl.program_id(1)
    ki = pl.program_id(2)
    kdiag = (qi * TQ) // TK

    @pl.when(ki == 0)
    def _init():
        qs_ref[...] = (q_ref[...] * QK_SCALE2).astype(BF16)
        for h in range(A_HEADS):
            sl = slice(h * HEAD_PAD, (h + 1) * HEAD_PAD)
            s = _bdot_nt(qs_ref[:, sl], km_ref[:, sl])
            m = jnp.max(s, axis=1, keepdims=True)
            m_refs[h][...] = m
            acc_refs[h][...] = _bdot(jnp.exp2(s - m), vm_ref[:, sl])

    def step(masked):
        if masked:
            row = lax.broadcasted_iota(jnp.int32, (TQ, TK), 0)
            col = lax.broadcasted_iota(jnp.int32, (TQ, TK), 1)
            visible = col <= row + (qi * TQ - ki * TK)
        def scores(h):
            sl = slice(h * HEAD_PAD, (h + 1) * HEAD_PAD)
            return _bdot_nt(qs_ref[:, sl], k_ref[:, sl])

        s_next = scores(0)
        for h in range(A_HEADS):
            sl = slice(h * HEAD_PAD, (h + 1) * HEAD_PAD)
            s = s_next
            if h + 1 < A_HEADS:
                s_next = scores(h + 1)
            if masked:
                s = jnp.where(visible, s, -jnp.inf)
            m_prev = m_refs[h][...]
            m_new = jnp.maximum(m_prev, jnp.max(s, axis=1, keepdims=True))
            alpha = jnp.exp2(m_prev - m_new)
            p = jnp.exp2(s - m_new)
            acc_refs[h][...] = alpha * acc_refs[h][...] + _bdot(p, v_ref[:, sl])
            m_refs[h][...] = m_new

    pl.when(ki < kdiag)(functools.partial(step, False))
    pl.when(ki == kdiag)(functools.partial(step, True))

    @pl.when(ki == pl.num_programs(2) - 1)
    def _fin():
        for h in range(A_HEADS):
            a = acc_refs[h][...]
            o_ref[:, h * HEAD_PAD:(h + 1) * HEAD_PAD] = a / a[:, A_V:A_V + 1]


def _flash(q, k, v, obuf, nb, seq, meta_row):
    nq = seq // TQ
    nk = seq // TK
    meta_blk = meta_row // N_META
    qmap = lambda b, i, j: (b * nq + i, 0)
    kmap = lambda b, i, j: (b * nk + jnp.minimum(j, (i * TQ) // TK), 0)
    mmap = lambda b, i, j: (meta_blk, 0)
    return pl.pallas_call(
        _flash_kernel,
        grid=(nb, nq, nk),
        in_specs=[pl.BlockSpec((TQ, ATT_W), qmap),
                  pl.BlockSpec((TK, ATT_W), kmap),
                  pl.BlockSpec((TK, ATT_W), kmap),
                  pl.BlockSpec((N_META, ATT_W), mmap),
                  pl.BlockSpec((N_META, ATT_W), mmap),
                  pl.BlockSpec(memory_space=pl.ANY)],
        out_specs=pl.BlockSpec((TQ, ATT_W), qmap),
        out_shape=jax.ShapeDtypeStruct(obuf.shape, F32),
        scratch_shapes=([pltpu.VMEM((TQ, ATT_W), BF16)]
                        + [pltpu.VMEM((TQ, 1), F32)] * A_HEADS
                        + [pltpu.VMEM((TQ, HEAD_PAD), F32)] * A_HEADS),
        input_output_aliases={5: 0},
        compiler_params=_cparams(("parallel", "parallel", "arbitrary")),
        name="mla_flash",
    )(q, k, v, k, v, obuf)


def _meta_attn_kernel(q_ref, k_ref, v_ref, buf_ref, o_ref):
    del buf_ref
    n = q_ref.shape[0]
    row = lax.broadcasted_iota(jnp.int32, (n, n), 0)
    col = lax.broadcasted_iota(jnp.int32, (n, n), 1)
    for h in range(A_HEADS):
        sl = slice(h * HEAD_PAD, (h + 1) * HEAD_PAD)
        s = _bdot_nt(q_ref[:, sl], k_ref[:, sl]) * ATT_SCALE
        s = jnp.where(col <= row, s, -jnp.inf)
        p = jnp.exp(s - jnp.max(s, axis=1, keepdims=True))
        o_ref[:, sl] = _bdot(p, v_ref[:, sl]) / jnp.sum(p, axis=1, keepdims=True)


def _meta_attn(q, k, v, buf, meta_row):
    blk = meta_row // N_META
    spec = pl.BlockSpec((N_META, ATT_W), lambda i: (blk, 0))
    return pl.pallas_call(
        _meta_attn_kernel,
        grid=(1,),
        in_specs=[spec, spec, spec, pl.BlockSpec(memory_space=pl.ANY)],
        out_specs=spec,
        out_shape=jax.ShapeDtypeStruct(buf.shape, F32),
        input_output_aliases={3: 0},
        compiler_params=_cparams(("arbitrary",)),
        name="mla_meta_attn",
    )(q, k, v, buf)


PAGES_PER_STEP = 8
KV_ROPE0 = A_KVLORA + ROPE_LANE0


def _sattn_kernel(pt_ref, q_ref, kvn_ref, wuk_ref, wuv_ref, cache_ref, obuf_ref, o_ref, buf_ref, sem_ref, *, layer):
    del obuf_ref
    b = pl.program_id(0)
    nb = pl.num_programs(0)

    def page_copy(seq, p, slot):
        col = pl.multiple_of(p * PAGE_SIZE, PAGE_SIZE)
        return pltpu.make_async_copy(cache_ref.at[layer, pt_ref[seq, p]],
                                     buf_ref.at[slot, :, pl.ds(col, PAGE_SIZE)], sem_ref.at[slot])

    def start(seq, slot):
        def body(p, carry):
            page_copy(seq, p, slot).start()
            return carry
        lax.fori_loop(0, N_PAGES, body, 0)

    def wait(seq, slot):
        def body(p, carry):
            page_copy(seq, p, slot).wait()
            return carry
        lax.fori_loop(0, N_PAGES, body, 0)

    @pl.when(b == 0)
    def _first():
        start(0, 0)

    @pl.when(b + 1 < nb)
    def _next():
        start(b + 1, (b + 1) % 2)

    slot = b % 2
    wait(b, slot)

    q = q_ref[...]
    ql, qr = [], []
    for h in range(A_HEADS):
        ql.append(_bdot(q[:, h * HEAD_PAD:h * HEAD_PAD + A_NOPE], wuk_ref[h]))
        qr.append(q[:, h * HEAD_PAD + ROPE_LANE0:h * HEAD_PAD + ROPE_LANE0 + A_ROPE])
    ql = (jnp.concatenate(ql, axis=0) * QK_SCALE2).astype(BF16)
    qr = (jnp.concatenate(qr, axis=0) * QK_SCALE2).astype(BF16)
    rows = A_HEADS * S_ROWS
    keys = PAGES_PER_STEP * PAGE_SIZE

    m_prev = jnp.full((rows, 1), -jnp.inf, F32)
    l_prev = jnp.zeros((rows, 1), F32)
    acc = jnp.zeros((rows, A_KVLORA), F32)
    def scores(c):
        kc = buf_ref[slot, :, c * keys:(c + 1) * keys]
        ckv = kc[:A_KVLORA].astype(BF16)
        kr = kc[A_KVLORA:].astype(BF16)
        return ckv, _bdot(ql, ckv) + _bdot(qr, kr)

    n_chunks = N_PAGES // PAGES_PER_STEP
    nxt = scores(0)
    for c in range(n_chunks):
        ckv, s = nxt
        if c + 1 < n_chunks:
            nxt = scores(c + 1)
        m_new = jnp.maximum(m_prev, jnp.max(s, axis=1, keepdims=True))
        alpha = jnp.exp2(m_prev - m_new)
        p = jnp.exp2(s - m_new)
        l_prev = alpha * l_prev + jnp.sum(p, axis=1, keepdims=True)
        acc = alpha * acc + _bdot_nt(p, ckv)
        m_prev = m_new

    kvn = kvn_ref[...]
    ckv_n = kvn[:, :A_KVLORA]
    kr_n = kvn[:, KV_ROPE0:KV_ROPE0 + A_ROPE]
    s = _bdot_nt(ql, ckv_n) + _bdot_nt(qr, kr_n)
    t_row = lax.broadcasted_iota(jnp.int32, (rows, S_ROWS), 0) % S_ROWS
    t_col = lax.broadcasted_iota(jnp.int32, (rows, S_ROWS), 1)
    s = jnp.where(t_col <= jnp.minimum(t_row, DEC_SEQ - 1), s, -jnp.inf)
    m_new = jnp.maximum(m_prev, jnp.max(s, axis=1, keepdims=True))
    alpha = jnp.exp2(m_prev - m_new)
    p = jnp.exp2(s - m_new)
    l_new = alpha * l_prev + jnp.sum(p, axis=1, keepdims=True)
    o_lat = (alpha * acc + _bdot(p, ckv_n)) / l_new
    for h in range(A_HEADS):
        o_ref[:, h * HEAD_PAD:(h + 1) * HEAD_PAD] = _bdot(o_lat[h * S_ROWS:(h + 1) * S_ROWS], wuv_ref[h])


def _sample_attn(page_table, q, kvrow, wuk_t, wuv_p, cache_t, obuf, layer):
    nb = page_table.shape[0]
    blk0 = ROW_S // S_ROWS
    grid_spec = pltpu.PrefetchScalarGridSpec(
        num_scalar_prefetch=1,
        grid=(nb,),
        in_specs=[pl.BlockSpec((S_ROWS, ATT_W), lambda b, pt: (blk0 + b, 0)),
                  pl.BlockSpec((S_ROWS, A_KVLORA + LANES), lambda b, pt: (blk0 + b, 0)),
                  pl.BlockSpec(wuk_t.shape, lambda b, pt: (0, 0, 0)),
                  pl.BlockSpec(wuv_p.shape, lambda b, pt: (0, 0, 0)),
                  pl.BlockSpec(memory_space=pl.ANY),
                  pl.BlockSpec(memory_space=pl.ANY)],
        out_specs=pl.BlockSpec((S_ROWS, ATT_W), lambda b, pt: (blk0 + b, 0)),
        scratch_shapes=[pltpu.VMEM((2, A_LAT, PAST_LEN), F32),
                        pltpu.SemaphoreType.DMA((2,))],
    )
    return pl.pallas_call(
        functools.partial(_sattn_kernel, layer=layer),
        grid_spec=grid_spec,
        out_shape=jax.ShapeDtypeStruct(obuf.shape, F32),
        input_output_aliases={6: 0},
        compiler_params=_cparams(("arbitrary",)),
        name="mla_sample_attn",
    )(page_table, q, kvrow, wuk_t, wuv_p, cache_t, obuf)


def _row_of(col_vec, eye):
    return jnp.sum(jnp.where(eye, col_vec, 0.0), axis=0, keepdims=True)


def _mlstm_kernel(g_ref, q_ref, k_ref, v_ref, og_ref, c0_ref, n0_ref, m0_ref, bi_ref, bf_ref, gn_ref, buf_ref,
                  o_ref, c_ref, n_ref, m_ref, c_scr, n_scr, m_scr, *, n_valid, seqs):
    del buf_ref
    ci = pl.program_id(1)
    L = q_ref.shape[0] // seqs

    @pl.when(ci == 0)
    def _load():
        c_scr[...] = c0_ref[...]
        for s in range(seqs):
            n_scr[s, 0:B_HEADS, :] = n0_ref[s]
            for h in range(B_HEADS):
                m_scr[s, h:h + 1, :] = jnp.broadcast_to(m0_ref[s, 0:1, h:h + 1], (1, LANES))

    row = lax.broadcasted_iota(jnp.int32, (L, L), 0)
    col = lax.broadcasted_iota(jnp.int32, (L, L), 1)
    eye = row == col
    tril = col <= row
    valid = lax.broadcasted_iota(jnp.int32, (L, 1), 0) < n_valid
    lane = lax.broadcasted_iota(jnp.int32, (1, LANES), 1)
    m_outs = []
    for s in range(seqs):
        rs = slice(s * L, (s + 1) * L)
        g = g_ref[rs, :]
        qs, ks, vs, qk, qc = [], [], [], [], []
        for h in range(B_HEADS):
            sl = slice(h * B_DK, (h + 1) * B_DK)
            qs.append(q_ref[rs, sl])
            ks.append(k_ref[rs, sl] * (B_DK ** -0.5))
            vs.append(v_ref[rs, sl])
            qk.append(_bdot_nt(qs[h], ks[h]))
            qc.append(_bdot(qs[h], c_scr[s, h]))
        m_out = jnp.zeros((1, LANES), F32)
        for h in range(B_HEADS):
            sl = slice(h * B_DK, (h + 1) * B_DK)
            q, k, v = qs[h], ks[h], vs[h]
            ig_col = jnp.where(valid, g[:, MI_LANE + h:MI_LANE + h + 1] + bi_ref[0:1, h:h + 1], -jnp.inf)
            fpre = g[:, MF_LANE + h:MF_LANE + h + 1] + bf_ref[0:1, h:h + 1]
            lf_col = jnp.where(valid, jnp.minimum(fpre, 0.0) - jnp.log1p(jnp.exp(-jnp.abs(fpre))), 0.0)
            ig_row = _row_of(ig_col, eye)
            lf_row = _row_of(lf_col, eye)
            b_col = jnp.sum(jnp.where(tril, lf_row, 0.0), axis=1, keepdims=True)
            b_row = jnp.sum(jnp.where(row <= col, lf_col, 0.0), axis=0, keepdims=True)
            m_prev = m_scr[s, h:h + 1, 0:1]
            c_prev = c_scr[s, h]
            n_prev = n_scr[s, h:h + 1, :]
            dmat = jnp.where(tril, b_col - b_row + ig_row, -jnp.inf)
            inter = b_col + m_prev
            m_t = jnp.maximum(inter, jnp.max(dmat, axis=1, keepdims=True))
            w_inter = jnp.exp(inter - m_t)
            sc = qk[h] * jnp.exp(dmat - m_t)
            num = w_inter * qc[h] + _bdot(sc, v)
            den = w_inter * jnp.sum(q * n_prev, axis=1, keepdims=True) + jnp.sum(sc, axis=1, keepdims=True)
            hh = num / jnp.maximum(jnp.abs(den), jnp.exp(-m_t))
            m_new = m_t[n_valid - 1:n_valid, :]
            b_last = b_col[n_valid - 1:n_valid, :]
            w_end = jnp.exp(b_last - b_col + ig_col - m_new)
            decay = jnp.exp(b_last + m_prev - m_new)
            kw = k * w_end
            c_scr[s, h] = decay * c_prev + _bdot_tn(kw, v)
            n_scr[s, h:h + 1, :] = decay * n_prev + jnp.sum(kw, axis=0, keepdims=True)
            m_scr[s, h:h + 1, :] = jnp.broadcast_to(m_new, (1, LANES))
            m_out = jnp.where(lane == h, m_new, m_out)
            o_ref[rs, sl] = _head_norm(hh, gn_ref[0:1, sl]) * _sigmoid(og_ref[rs, sl])
        m_outs.append(m_out)

    @pl.when(ci == pl.num_programs(1) - 1)
    def _store():
        c_ref[...] = c_scr[...]
        for s in range(seqs):
            n_ref[s] = n_scr[s, 0:B_HEADS, :]
            m_ref[s] = m_outs[s]


def _stream_out(obuf, n_in, w, row0, nc, L):
    blk0 = row0 // L
    spec = pl.BlockSpec((L, w), lambda b, c: (blk0 + b * nc + c, 0))
    return spec, [pl.BlockSpec(memory_space=pl.ANY)], [obuf], {n_in: 0}


def _mlstm(proj, row0, nb, nc, L, n_valid, c0, n0, m0, bi, bf, gn, shared_state, obuf, seqs=1):
    assert seqs == 1 or (nc == 1 and not shared_state and nb % seqs == 0)
    R = L * seqs
    blk0 = row0 // R
    rmap = lambda cb: (lambda b, c: (blk0 + b * nc + c, cb))
    smap = (lambda b, c: (0, 0, 0, 0)) if shared_state else (lambda b, c: (b, 0, 0, 0))
    smap3 = (lambda b, c: (0, 0, 0)) if shared_state else (lambda b, c: (b, 0, 0))
    w = B_HEADS * B_DK
    full2 = lambda a: pl.BlockSpec(a.shape, lambda b, c: (0, 0))
    ospec, xspec, xarg, alias = _stream_out(obuf, 11, w, row0, nc, R)
    return pl.pallas_call(
        functools.partial(_mlstm_kernel, n_valid=n_valid, seqs=seqs),
        grid=(nb // seqs, nc),
        in_specs=[pl.BlockSpec((R, LANES), rmap(GATE_COL // LANES)),
                  pl.BlockSpec((R, w), rmap(2)), pl.BlockSpec((R, w), rmap(3)),
                  pl.BlockSpec((R, w), rmap(4)), pl.BlockSpec((R, w), rmap(5)),
                  pl.BlockSpec((seqs, B_HEADS, B_DK, B_DV), smap),
                  pl.BlockSpec((seqs, B_HEADS, B_DK), smap3),
                  pl.BlockSpec((seqs, 1, B_HEADS), smap3),
                  full2(bi), full2(bf), full2(gn)] + xspec,
        out_specs=[ospec,
                   pl.BlockSpec((seqs, B_HEADS, B_DK, B_DV), lambda b, c: (b, 0, 0, 0)),
                   pl.BlockSpec((seqs, B_HEADS, B_DK), lambda b, c: (b, 0, 0)),
                   pl.BlockSpec((seqs, 1, LANES), lambda b, c: (b, 0, 0))],
        out_shape=[jax.ShapeDtypeStruct((proj.shape[0], w), F32),
                   jax.ShapeDtypeStruct((nb, B_HEADS, B_DK, B_DV), F32),
                   jax.ShapeDtypeStruct((nb, B_HEADS, B_DK), F32),
                   jax.ShapeDtypeStruct((nb, 1, LANES), F32)],
        scratch_shapes=[pltpu.VMEM((seqs, B_HEADS, B_DK, B_DV), F32),
                        pltpu.VMEM((seqs, SUBLANES, B_DK), F32),
                        pltpu.VMEM((seqs, SUBLANES, LANES), F32)],
        input_output_aliases=alias,
        compiler_params=_cparams(("parallel", "arbitrary")),
        name="mlstm",
    )(proj, proj, proj, proj, proj, c0, n0, m0, bi, bf, gn, *xarg)


RET_LOG_GAMMA = tuple(math.log1p(-(2.0 ** (-5.0 - h))) for h in range(D_HEADS))


def _ret_kernel(q_ref, k_ref, v_ref, g_ref, cr_ref, sr_ref, s0_ref, gn_ref, buf_ref, o_ref, s_ref, s_scr, *,
                n_valid, seqs):
    del buf_ref
    ci = pl.program_id(1)
    L = q_ref.shape[0] // seqs

    @pl.when(ci == 0)
    def _load():
        s_scr[...] = s0_ref[...]

    row = lax.broadcasted_iota(jnp.int32, (L, L), 0)
    col = lax.broadcasted_iota(jnp.int32, (L, L), 1)
    rel = (row - col).astype(F32)
    t_col = lax.broadcasted_iota(jnp.int32, (L, 1), 0)
    tf = t_col.astype(F32)
    half = D_DK // 2
    for s in range(seqs):
        rs = slice(s * L, (s + 1) * L)
        cr, sr = cr_ref[rs, :], sr_ref[rs, :]
        qs, ks, vs, qk, qs_state = [], [], [], [], []
        for h in range(D_HEADS):
            sl = slice(h * D_DK, (h + 1) * D_DK)
            q = q_ref[rs, sl]
            k = k_ref[rs, sl]
            qs.append(q * cr + pltpu.roll(q, half, 1) * sr)
            ks.append((k * cr + pltpu.roll(k, half, 1) * sr) * (D_DK ** -0.5))
            vs.append(v_ref[rs, sl])
            qk.append(_bdot_nt(qs[h], ks[h]))
            qs_state.append(_bdot(qs[h], s_scr[s, h]))
        for h in range(D_HEADS):
            lg = RET_LOG_GAMMA[h]
            sl = slice(h * D_DK, (h + 1) * D_DK)
            k, v = ks[h], vs[h]
            dec = jnp.where(rel >= 0, jnp.exp(lg * jnp.maximum(rel, 0.0)), 0.0)
            inner = qk[h] * dec
            o = _bdot(inner, v) + jnp.exp((tf + 1.0) * lg) * qs_state[h]
            w_end = jnp.where(t_col < n_valid, jnp.exp((n_valid - 1.0 - tf) * lg), 0.0)
            s_scr[s, h] = math.exp(n_valid * lg) * s_scr[s, h] + _bdot_tn(k * w_end, v)
            gate = g_ref[rs, sl]
            o_ref[rs, sl] = _head_norm(o, gn_ref[0:1, sl]) * (gate * _sigmoid(gate))

    @pl.when(ci == pl.num_programs(1) - 1)
    def _store():
        s_ref[...] = s_scr[...]


def _retention(proj, tabs, row0, nb, nc, L, n_valid, s0, gn, shared_state, obuf, seqs=1):
    assert seqs == 1 or (nc == 1 and not shared_state and nb % seqs == 0)
    R = L * seqs
    blk0 = row0 // R
    rmap = lambda cb: (lambda b, c: (blk0 + b * nc + c, cb))
    smap = (lambda b, c: (0, 0, 0, 0)) if shared_state else (lambda b, c: (b, 0, 0, 0))
    w = D_HEADS * D_DK
    ospec, xspec, xarg, alias = _stream_out(obuf, 8, w, row0, nc, R)
    return pl.pallas_call(
        functools.partial(_ret_kernel, n_valid=n_valid, seqs=seqs),
        grid=(nb // seqs, nc),
        in_specs=[pl.BlockSpec((R, w), rmap(2)), pl.BlockSpec((R, w), rmap(3)),
                  pl.BlockSpec((R, w), rmap(4)), pl.BlockSpec((R, w), rmap(5)),
                  pl.BlockSpec((R, LANES), rmap(0)), pl.BlockSpec((R, LANES), rmap(0)),
                  pl.BlockSpec((seqs, D_HEADS, D_DK, D_DV), smap),
                  pl.BlockSpec(gn.shape, lambda b, c: (0, 0))] + xspec,
        out_specs=[ospec,
                   pl.BlockSpec((seqs, D_HEADS, D_DK, D_DV), lambda b, c: (b, 0, 0, 0))],
        out_shape=[jax.ShapeDtypeStruct((proj.shape[0], w), F32),
                   jax.ShapeDtypeStruct((nb, D_HEADS, D_DK, D_DV), F32)],
        scratch_shapes=[pltpu.VMEM((seqs, D_HEADS, D_DK, D_DV), F32)],
        input_output_aliases=alias,
        compiler_params=_cparams(("parallel", "arbitrary")),
        name="retention",
    )(proj, proj, proj, proj, tabs[0], tabs[1], s0, gn, *xarg)


def _rglru_kernel(x_ref, gr_ref, h0_ref, cv0_ref, cw_ref, cb_ref, wa_ref, ba_ref, wi_ref, bi_ref, lam_ref, buf_ref,
                  o_ref, h_ref, cv_ref, xp_scr, h_scr, *, n_valid, seqs):
    del buf_ref
    ci = pl.program_id(1)
    R = x_ref.shape[0]
    L = R // seqs
    tail = C_CONV - 1

    @pl.when(ci == 0)
    def _load():
        for s in range(seqs):
            xp_scr[s, SUBLANES - tail:SUBLANES, :] = cv0_ref[s]
            h_scr[s, 0:1, :] = h0_ref[s]

    us, h0s, new_tails = [], [], []
    for s in range(seqs):
        xp_scr[s, SUBLANES:SUBLANES + L, :] = x_ref[s * L:(s + 1) * L, :]
        us.append(sum(xp_scr[s, SUBLANES - tail + j:SUBLANES - tail + j + L, :] * cw_ref[j:j + 1, :]
                      for j in range(C_CONV)))
        new_tails.append(xp_scr[s, SUBLANES + n_valid - tail:SUBLANES + n_valid, :])
        xp_scr[s, SUBLANES - tail:SUBLANES, :] = new_tails[s]
        h0s.append(jnp.broadcast_to(h_scr[s, 0:1, :], (L, C_WIDTH)))
    u = cb_ref[...] + (us[0] if seqs == 1 else jnp.concatenate(us, axis=0))
    h_prev = h0s[0] if seqs == 1 else jnp.concatenate(h0s, axis=0)
    r = _sigmoid(_bdot(u, wa_ref[...]) + ba_ref[...])
    i = _sigmoid(_bdot(u, wi_ref[...]) + bi_ref[...])
    nlam = -lam_ref[...]
    softplus = jnp.maximum(nlam, 0.0) + jnp.log1p(jnp.exp(-jnp.abs(nlam)))
    log_a = -C_POW * r * softplus
    a = jnp.exp(log_a)
    b = jnp.sqrt(-jnp.tanh(log_a) * (a * a + 1.0)) * (i * u)
    t_idx = lax.broadcasted_iota(jnp.int32, (R, 1), 0) % L
    step = 1
    while step < L:
        keep = t_idx >= step
        a_sh = jnp.where(keep, pltpu.roll(a, step, 0), 1.0)
        b_sh = jnp.where(keep, pltpu.roll(b, step, 0), 0.0)
        b = a * b_sh + b
        a = a * a_sh
        step *= 2
    hseq = a * h_prev + b
    h_lasts = [hseq[s * L + n_valid - 1:s * L + n_valid, :] for s in range(seqs)]
    for s in range(seqs):
        h_scr[s, 0:1, :] = h_lasts[s]
    gr = gr_ref[...]
    gelu = 0.5 * gr * (1.0 + jnp.tanh(math.sqrt(2.0 / math.pi) * (gr + 0.044715 * (gr * gr * gr))))
    o_ref[...] = hseq * gelu

    @pl.when(ci == pl.num_programs(1) - 1)
    def _store():
        for s in range(seqs):
            h_ref[s] = h_lasts[s]
            cv_ref[s] = new_tails[s]


def _rglru(proj, row0, nb, nc, L, n_valid, h0, cv0, cw, cb, wa, ba, wi, bi, lam, shared_state, obuf, seqs=1):
    assert seqs == 1 or (nc == 1 and not shared_state and nb % seqs == 0)
    R = L * seqs
    blk0 = row0 // R
    rmap = lambda cb_: (lambda b, c: (blk0 + b * nc + c, cb_))
    smap = (lambda b, c: (0, 0, 0)) if shared_state else (lambda b, c: (b, 0, 0))
    full2 = lambda a: pl.BlockSpec(a.shape, lambda b, c: (0, 0))
    w = C_WIDTH
    ospec, xspec, xarg, alias = _stream_out(obuf, 11, w, row0, nc, R)
    return pl.pallas_call(
        functools.partial(_rglru_kernel, n_valid=n_valid, seqs=seqs),
        grid=(nb // seqs, nc),
        in_specs=[pl.BlockSpec((R, w), rmap(0)), pl.BlockSpec((R, w), rmap(1)),
                  pl.BlockSpec((seqs, 1, w), smap), pl.BlockSpec((seqs, C_CONV - 1, w), smap),
                  full2(cw), full2(cb), full2(wa), full2(ba), full2(wi), full2(bi), full2(lam)] + xspec,
        out_specs=[ospec,
                   pl.BlockSpec((seqs, 1, w), lambda b, c: (b, 0, 0)),
                   pl.BlockSpec((seqs, C_CONV - 1, w), lambda b, c: (b, 0, 0))],
        out_shape=[jax.ShapeDtypeStruct((proj.shape[0], w), F32),
                   jax.ShapeDtypeStruct((nb, 1, w), F32),
                   jax.ShapeDtypeStruct((nb, C_CONV - 1, w), F32)],
        scratch_shapes=[pltpu.VMEM((seqs, SUBLANES + L, w), F32), pltpu.VMEM((seqs, SUBLANES, w), F32)],
        input_output_aliases=alias,
        compiler_params=_cparams(("parallel", "arbitrary")),
        name="rglru",
    )(proj, proj, h0, cv0, cw, cb, wa, ba, wi, bi, lam, *xarg)


def _outproj_kernel(o1_ref, o2_ref, w1_ref, w2_ref, x_ref, g_ref, b_ref, y_ref):
    mix = _bdot(o1_ref[...], w1_ref[...]) + _bdot(o2_ref[...], w2_ref[...])
    y_ref[...] = _layer_norm(DN_ALPHA * x_ref[...] + mix, g_ref[...], b_ref[...])


def _outproj_ln(o1, o2, w1, w2, x, g, b):
    n = x.shape[0]
    row = lambda a: pl.BlockSpec((TM, a.shape[1]), lambda i: (i, 0))
    full = lambda a: pl.BlockSpec(a.shape, lambda i: (0, 0))
    return pl.pallas_call(
        _outproj_kernel,
        grid=(n // TM,),
        in_specs=[row(o1), row(o2), full(w1), full(w2), row(x), full(g), full(b)],
        out_specs=row(x),
        out_shape=jax.ShapeDtypeStruct(x.shape, F32),
        compiler_params=_cparams(("parallel",)),
        name="outproj_ln",
    )(o1, o2, w1, w2, x, g, b)


ROUTER_E0 = N_GROUPS
MOE_EXPERTS_PER_CHUNK = 2


def _moe_kernel(x_ref, wr_ref, br_ref, w1_ref, w3_ref, w2_ref, g_ref, b_ref, y_ref, xb_scr, gate_scr, acc_scr):
    j = pl.program_id(1)

    @pl.when(j == 0)
    def _route():
        x = x_ref[...]
        xh = x.astype(BF16)
        xb_scr[...] = xh
        xl = (x - xh.astype(F32)).astype(BF16)
        wr = wr_ref[...]
        wh = wr.astype(BF16)
        wl = (wr - wh.astype(F32)).astype(BF16)
        dot = lambda a, b: jnp.dot(a, b, preferred_element_type=F32)
        logits = dot(xh, wh) + (dot(xh, wl) + dot(xl, wh)) + br_ref[...]
        lane = lax.broadcasted_iota(jnp.int32, logits.shape, 1)
        gl = jnp.where(lane < N_GROUPS, logits, -jnp.inf)
        gmax = jnp.max(gl, axis=1, keepdims=True)
        g_gate = 1.0 / jnp.sum(jnp.exp(gl - gmax), axis=1, keepdims=True)
        g_idx = jnp.min(jnp.where(gl == gmax, lane, LANES), axis=1, keepdims=True)
        e_lo = ROUTER_E0 + g_idx * E_PER_GROUP
        el = jnp.where(jnp.logical_and(lane >= e_lo, lane < e_lo + E_PER_GROUP), logits, -jnp.inf)
        e1 = jnp.max(el, axis=1, keepdims=True)
        i1 = jnp.min(jnp.where(el == e1, lane, LANES), axis=1, keepdims=True)
        el2 = jnp.where(lane == i1, -jnp.inf, el)
        e2 = jnp.max(el2, axis=1, keepdims=True)
        i2 = jnp.min(jnp.where(el2 == e2, lane, LANES), axis=1, keepdims=True)
        r = jnp.exp(e2 - e1)
        w1 = g_gate / (1.0 + r)
        gate = jnp.where(lane == i1, w1, 0.0) + jnp.where(lane == i2, w1 * r, 0.0)
        for gi in range(N_GROUPS):
            lo = ROUTER_E0 + gi * E_PER_GROUP
            gate_scr[gi] = gate[:, lo:lo + E_PER_GROUP]
        acc_scr[...] = jnp.zeros_like(acc_scr)

    xb = xb_scr[...]
    gate = gate_scr[j]
    cw = MOE_EXPERTS_PER_CHUNK * D_EXPERT

    def up(c):
        cs = slice(c * cw, (c + 1) * cw)
        return (jnp.dot(xb, w1_ref[0, :, cs], preferred_element_type=F32),
                jnp.dot(xb, w3_ref[0, :, cs], preferred_element_type=F32))

    parts = []
    nxt = up(0)
    for c in range(E_PER_GROUP // MOE_EXPERTS_PER_CHUNK):
        h1, h3 = nxt
        if (c + 1) * MOE_EXPERTS_PER_CHUNK < E_PER_GROUP:
            nxt = up(c + 1)
        hg = h1 * _sigmoid(h1) * h3
        for ee in range(MOE_EXPERTS_PER_CHUNK):
            e = c * MOE_EXPERTS_PER_CHUNK + ee
            parts.append((hg[:, ee * D_EXPERT:(ee + 1) * D_EXPERT] * gate[:, e:e + 1]).astype(BF16))
    acc_scr[...] += jnp.dot(jnp.concatenate(parts, axis=1), w2_ref[0].astype(BF16), preferred_element_type=F32)

    @pl.when(j == pl.num_programs(1) - 1)
    def _fin():
        y_ref[...] = _layer_norm(DN_ALPHA * x_ref[...] + acc_scr[...], g_ref[...], b_ref[...])


def _moe_ln(x, wr, br, w1, w3, w2, g, b):
    n = x.shape[0]
    gw = E_PER_GROUP * D_EXPERT
    row = pl.BlockSpec((TM, D_MODEL), lambda i, j: (i, 0))
    full = lambda a: pl.BlockSpec(a.shape, lambda i, j: (0, 0))
    return pl.pallas_call(
        _moe_kernel,
        grid=(n // TM, N_GROUPS),
        in_specs=[row, full(wr), full(br),
                  pl.BlockSpec((1, D_MODEL, gw), lambda i, j: (j, 0, 0)),
                  pl.BlockSpec((1, D_MODEL, gw), lambda i, j: (j, 0, 0)),
                  pl.BlockSpec((1, gw, D_MODEL), lambda i, j: (j, 0, 0)),
                  full(g), full(b)],
        out_specs=row,
        out_shape=jax.ShapeDtypeStruct(x.shape, F32),
        scratch_shapes=[pltpu.VMEM((TM, D_MODEL), BF16),
                        pltpu.VMEM((N_GROUPS, TM, E_PER_GROUP), F32),
                        pltpu.VMEM((TM, D_MODEL), F32)],
        compiler_params=_cparams(("parallel", "arbitrary")),
        name="moe_ln",
    )(x, wr, br, w1, w3, w2, g, b)


def _positions():
    pos = jnp.zeros((N_TOK,), F32)
    main = (N_META + jnp.arange(SEQ, dtype=jnp.int32)).astype(F32)
    pos = pos.at[:N_MAIN].set(jnp.tile(main, BATCH))
    samp = (PAST_LEN + jnp.minimum(jnp.arange(S_ROWS, dtype=jnp.int32), DEC_SEQ - 1)).astype(F32)
    pos = pos.at[ROW_S:ROW_M].set(jnp.tile(samp, DEC_BATCH))
    pos = pos.at[ROW_M:ROW_M + N_META].set(jnp.arange(N_META, dtype=jnp.int32).astype(F32))
    return pos


def _rope_tables_mla(pos):
    half = A_ROPE // 2
    freqs = ROPE_THETA ** (-jnp.arange(half, dtype=F32) / half)
    ang = pos[:, None] * freqs
    cos, sin = jnp.cos(ang), jnp.sin(ang)
    n = pos.shape[0]
    zeros = lambda w: jnp.zeros((n, w), F32)
    rest = LANES - ROPE_LANE0 - A_ROPE
    c = jnp.concatenate([jnp.ones((n, ROPE_LANE0), F32), cos, cos, zeros(rest)], axis=1)
    s1 = jnp.concatenate([zeros(ROPE_LANE0 + half), sin, zeros(rest)], axis=1)
    s2 = jnp.concatenate([zeros(ROPE_LANE0), -sin, zeros(half + rest)], axis=1)
    return c, s1, s2


def _rope_tables_ret(pos):
    half = D_DK // 2
    freqs = ROPE_THETA ** (-jnp.arange(half, dtype=F32) / half)
    ang = pos[:, None] * freqs
    cos, sin = jnp.cos(ang), jnp.sin(ang)
    return jnp.concatenate([cos, cos], axis=1), jnp.concatenate([-sin, sin], axis=1)


def _pad_cols(a, w):
    return jnp.pad(a, ((0, 0), (0, w - a.shape[1])))


def _even_in_weight(w):
    cuts = [0, A_QLORA, A_QLORA + A_KVLORA, A_QLORA + A_KVLORA + A_ROPE]
    c_q, c_kv, k_r = (w[:, cuts[i]:cuts[i + 1]] for i in range(3))
    m0 = cuts[3]
    mw = B_HEADS * B_DK
    big = w[:, m0:m0 + 4 * mw]
    gates = w[:, m0 + 4 * mw:]
    head = jnp.concatenate([c_q, c_kv, jnp.zeros((D_MODEL, ROPE_LANE0), w.dtype), k_r, gates], axis=1)
    return jnp.concatenate([_pad_cols(head, 1024), big], axis=1).astype(BF16)


def _head_pad_cols(a, width):
    k = a.shape[0]
    return jnp.pad(a, ((0, 0), (0, 0), (0, HEAD_PAD - width))).reshape(k, A_HEADS * HEAD_PAD)


def _block_diag(w):
    eye = jnp.eye(C_BLOCKS, dtype=w.dtype)
    return (w[:, :, None, :] * eye[:, None, :, None]).reshape(C_WIDTH, C_WIDTH)


def _assemble(main, samp, meta):
    w = main.shape[1]
    pad = jnp.zeros((N_TOK - ROW_M - N_META, w), main.dtype)
    return jnp.concatenate([main, samp, meta, pad], axis=0)


def kernel(x_prompt, x_sample, cache_mla, page_table, state_mlstm_c, state_mlstm_n, state_mlstm_m,
           state_rglru_h, state_rglru_conv, state_ret, meta_tokens, w_in_e, g_cq, g_ckv, w_uq, w_uk, w_uv,
           b_mi, b_mf, g_mlstm, w_out_e, w_in_o, conv_w, conv_b, w_ra, b_ra, w_ri, b_ri, lam, g_ret, w_out_o,
           ln_mix_g, ln_mix_b, ln_ffn_g, ln_ffn_b, w_gr, b_gr, w_er, b_er, w_e1, w_e3, w_e2):
    nc_main = SEQ // CHUNK
    xs = jnp.pad(x_sample, ((0, 0), (0, S_ROWS - DEC_SEQ), (0, 0))).reshape(N_SAMP, D_MODEL)
    x = _assemble(x_prompt.reshape(N_MAIN, D_MODEL), xs, meta_tokens)

    cache_t = jnp.swapaxes(cache_mla, 2, 3)

    pos = _positions()
    tabs_mla = _rope_tables_mla(pos)
    tabs_ret = _rope_tables_ret(pos)
    row2 = lambda a: a.reshape(1, -1)
    spare = {}

    def zbuf(w):
        return spare[w].pop() if spare.get(w) else jnp.zeros((N_TOK, w), F32)

    kv_p, kv_s, c_p, c_s, n_p, n_s, m_p, m_s = [], [], [], [], [], [], [], []
    h_p, h_s, cv_p, cv_s, s_p, s_s = [], [], [], [], [], []

    for l in range(DEPTH):
        j = l // 2
        if l % 2 == 0:
            proj = _matmul(x, _even_in_weight(w_in_e[j]))
            wuq = _head_pad_cols(w_uq[j].reshape(A_QLORA, A_HEADS, A_NOPE + A_ROPE), A_NOPE + A_ROPE).astype(BF16)
            wk = _head_pad_cols(w_uk[j], A_NOPE).astype(BF16)
            wv = _head_pad_cols(w_uv[j], A_V).astype(BF16)
            q, kp, vp, kvrow = _mla_prep(proj, tabs_mla, row2(g_cq[j]), row2(g_ckv[j]), wuq, wk, wv)
            wuk_t = jnp.transpose(w_uk[j], (1, 2, 0)).astype(BF16)
            wuv_p = jnp.pad(jnp.transpose(w_uv[j], (1, 0, 2)),
                            ((0, 0), (0, 0), (0, HEAD_PAD - A_V))).astype(BF16)
            o1 = _flash(q, kp, vp, zbuf(ATT_W), BATCH, SEQ, ROW_M)
            o1 = _sample_attn(page_table, q, kvrow, wuk_t, wuv_p, cache_t, o1, j)
            o1 = _meta_attn(q, kp, vp, o1, ROW_M)

            bi, bf, gn = row2(b_mi[j]), row2(b_mf[j]), row2(g_mlstm[j])
            zc = jnp.zeros((1, B_HEADS, B_DK, B_DV), F32)
            zn = jnp.zeros((1, B_HEADS, B_DK), F32)
            zm = jnp.zeros((1, 1, B_HEADS), F32)
            o2, c_m, n_m, m_m = _mlstm(proj, ROW_M, 1, 1, N_META, N_META, zc, zn, zm, bi, bf, gn, True,
                                       zbuf(B_HEADS * B_DV))
            o2, c_pm, n_pm, m_pm = _mlstm(proj, 0, BATCH, nc_main, CHUNK, CHUNK, c_m, n_m,
                                          m_m[:, :, :B_HEADS], bi, bf, gn, True, o2)
            o2, c_sm, n_sm, m_sm = _mlstm(proj, ROW_S, DEC_BATCH, 1, S_ROWS, DEC_SEQ, state_mlstm_c[j],
                                          state_mlstm_n[j], state_mlstm_m[j].reshape(DEC_BATCH, 1, B_HEADS),
                                          bi, bf, gn, False, o2, SEQS_PER_STEP)

            w1 = jnp.pad(w_out_e[j][:A_HEADS * A_V].reshape(A_HEADS, A_V, D_MODEL),
                         ((0, 0), (0, HEAD_PAD - A_V), (0, 0))).reshape(ATT_W, D_MODEL).astype(BF16)
            w2 = w_out_e[j][A_HEADS * A_V:].astype(BF16)

            kv_main = kvrow[:N_MAIN].reshape(BATCH, SEQ, -1)
            kv_meta = jnp.broadcast_to(kvrow[ROW_M:ROW_M + N_META][None], (BATCH, N_META, kvrow.shape[1]))
            kv_all = jnp.concatenate([kv_meta, kv_main], axis=1)
            kv_p.append(jnp.concatenate([kv_all[..., :A_KVLORA], kv_all[..., KV_ROPE0:KV_ROPE0 + A_ROPE]], -1))
            kv_sr = kvrow[ROW_S:ROW_M].reshape(DEC_BATCH, S_ROWS, -1)[:, :DEC_SEQ]
            kv_s.append(jnp.concatenate([kv_sr[..., :A_KVLORA], kv_sr[..., KV_ROPE0:KV_ROPE0 + A_ROPE]], -1))
            c_p.append(c_pm); n_p.append(n_pm); m_p.append(m_pm[:, 0, :B_HEADS])
            c_s.append(c_sm); n_s.append(n_sm); m_s.append(m_sm[:, 0, :B_HEADS])
        else:
            proj = _matmul(x, w_in_o[j].astype(BF16))
            cw, cb = conv_w[j], row2(conv_b[j])
            wa, wi = _block_diag(w_ra[j]).astype(BF16), _block_diag(w_ri[j]).astype(BF16)
            ba, bi_, lam_ = row2(b_ra[j]), row2(b_ri[j]), row2(lam[j])
            zh = jnp.zeros((1, 1, C_WIDTH), F32)
            zcv = jnp.zeros((1, C_CONV - 1, C_WIDTH), F32)
            args = (cw, cb, wa, ba, wi, bi_, lam_)
            o1, h_m, cv_m = _rglru(proj, ROW_M, 1, 1, N_META, N_META, zh, zcv, *args, True, zbuf(C_WIDTH))
            o1, h_pm, cv_pm = _rglru(proj, 0, BATCH, nc_main, CHUNK, CHUNK, h_m, cv_m, *args, True, o1)
            o1, h_sm, cv_sm = _rglru(proj, ROW_S, DEC_BATCH, 1, S_ROWS, DEC_SEQ,
                                     state_rglru_h[j].reshape(DEC_BATCH, 1, C_WIDTH), state_rglru_conv[j],
                                     *args, False, o1, SEQS_PER_STEP_RGLRU)

            gn = row2(g_ret[j])
            zs = jnp.zeros((1, D_HEADS, D_DK, D_DV), F32)
            o2, s_m = _retention(proj, tabs_ret, ROW_M, 1, 1, N_META, N_META, zs, gn, True, zbuf(D_HEADS * D_DV))
            o2, s_pm = _retention(proj, tabs_ret, 0, BATCH, nc_main, CHUNK, CHUNK, s_m, gn, True, o2)
            o2, s_sm = _retention(proj, tabs_ret, ROW_S, DEC_BATCH, 1, S_ROWS, DEC_SEQ, state_ret[j], gn, False, o2,
                                  SEQS_PER_STEP)

            w1 = w_out_o[j][:C_WIDTH].astype(BF16)
            w2 = w_out_o[j][C_WIDTH:].astype(BF16)
            h_p.append(h_pm[:, 0]); cv_p.append(cv_pm); s_p.append(s_pm)
            h_s.append(h_sm[:, 0]); cv_s.append(cv_sm); s_s.append(s_sm)

        x = _outproj_ln(o1, o2, w1, w2, x, row2(ln_mix_g[l]), row2(ln_mix_b[l]))
        for dead in (o1, o2):
            spare.setdefault(dead.shape[1], []).append(dead)

        wr = jnp.concatenate([w_gr[l], jnp.transpose(w_er[l], (1, 0, 2)).reshape(D_MODEL, N_EXPERTS)], axis=1)
        br = jnp.concatenate([b_gr[l], b_er[l].reshape(N_EXPERTS)])
        gw = E_PER_GROUP * D_EXPERT
        w1e = jnp.transpose(w_e1[l].reshape(N_GROUPS, E_PER_GROUP, D_MODEL, D_EXPERT),
                            (0, 2, 1, 3)).reshape(N_GROUPS, D_MODEL, gw).astype(BF16)
        w3e = jnp.transpose(w_e3[l].reshape(N_GROUPS, E_PER_GROUP, D_MODEL, D_EXPERT),
                            (0, 2, 1, 3)).reshape(N_GROUPS, D_MODEL, gw).astype(BF16)
        w2e = w_e2[l].reshape(N_GROUPS, gw, D_MODEL)
        x = _moe_ln(x, _pad_cols(wr, LANES), _pad_cols(row2(br), LANES), w1e, w3e, w2e,
                    row2(ln_ffn_g[l]), row2(ln_ffn_b[l]))

    y_prompt = x[:N_MAIN].reshape(BATCH, SEQ, D_MODEL)
    y_sample = x[ROW_S:ROW_M].reshape(DEC_BATCH, S_ROWS, D_MODEL)[:, :DEC_SEQ]
    st = jnp.stack
    return (y_prompt, y_sample, st(kv_p), st(kv_s), st(c_p), st(c_s), st(n_p), st(n_s), st(m_p), st(m_s),
            st(h_p), st(h_s), st(cv_p), st(cv_s), st(s_p), st(s_s))
```

```python
import functools
import math

import jax
import jax.numpy as jnp
from jax import lax
from jax.experimental import pallas as pl
from jax.experimental.pallas import tpu as pltpu

F32 = jnp.float32
BF16 = jnp.bfloat16

D_MODEL = 1024
BATCH = 4
SEQ = 4096
DEPTH = 4
DEC_BATCH = 128
DEC_SEQ = 4
PAST_LEN = 8192
PAGE_SIZE = 128
N_PAGES = PAST_LEN // PAGE_SIZE
N_META = 16
N_EVEN = (DEPTH + 1) // 2
N_ODD = DEPTH // 2
CHUNK = 128
ROPE_THETA = 10000.0
A_HEADS = 8
A_NOPE = 64
A_ROPE = 32
A_V = 64
A_QLORA = 384
A_KVLORA = 256
A_LAT = A_KVLORA + A_ROPE
B_HEADS = 4
B_DK = 128
B_DV = 128
C_WIDTH = 512
C_BLOCKS = 8
C_BW = C_WIDTH // C_BLOCKS
C_CONV = 4
C_POW = 8.0
D_HEADS = 4
D_DK = 128
D_DV = 128
N_GROUPS = 4
E_PER_GROUP = 8
N_EXPERTS = N_GROUPS * E_PER_GROUP
D_EXPERT = 128
DN_ALPHA = (2.0 * DEPTH) ** 0.25
EPS = 1e-5

LANES = 128
SUBLANES = 8
VMEM_LIMIT = 48 * 1024 * 1024

S_ROWS = 8
N_MAIN = BATCH * SEQ
ROW_S = N_MAIN
N_SAMP = DEC_BATCH * S_ROWS
ROW_M = ROW_S + N_SAMP
TM = 512
N_TOK = -(-(ROW_M + N_META) // TM) * TM

HEAD_PAD = LANES
ATT_W = A_HEADS * HEAD_PAD
ROPE_LANE0 = A_NOPE
PROJ_E_W = 3072
GATE_COL = 640
MI_LANE = ROPE_LANE0 + A_ROPE
MF_LANE = MI_LANE + B_HEADS

TQ = 256
TK = 512
SEQS_PER_STEP = 4
SEQS_PER_STEP_RGLRU = 16


def _cparams(sem):
    return pltpu.CompilerParams(dimension_semantics=sem, vmem_limit_bytes=VMEM_LIMIT)


def _bdot(a, b):
    return jnp.dot(a.astype(BF16), b.astype(BF16), preferred_element_type=F32)


def _bdot_nt(a, b):
    return lax.dot_general(a.astype(BF16), b.astype(BF16), (((1,), (1,)), ((), ())),
                           preferred_element_type=F32)


def _bdot_tn(a, b):
    return lax.dot_general(a.astype(BF16), b.astype(BF16), (((0,), (0,)), ((), ())),
                           preferred_element_type=F32)


def _sigmoid(x):
    return 1.0 / (1.0 + jnp.exp(-x))


def _layer_norm(x, g, b):
    mu = jnp.mean(x, axis=-1, keepdims=True)
    xc = x - mu
    var = jnp.mean(xc * xc, axis=-1, keepdims=True)
    return xc * lax.rsqrt(var + EPS) * g + b


def _head_norm(x, g):
    mu = jnp.mean(x, axis=-1, keepdims=True)
    xc = x - mu
    var = jnp.mean(xc * xc, axis=-1, keepdims=True)
    return xc * lax.rsqrt(var + EPS) * g


def _mm_kernel(x_ref, w_ref, o_ref):
    o_ref[...] = _bdot(x_ref[...], w_ref[...])


def _matmul(x, w):
    n, k = x.shape
    m = w.shape[1]
    return pl.pallas_call(
        _mm_kernel,
        grid=(n // TM,),
        in_specs=[pl.BlockSpec((TM, k), lambda i: (i, 0)),
                  pl.BlockSpec((k, m), lambda i: (0, 0))],
        out_specs=pl.BlockSpec((TM, m), lambda i: (i, 0)),
        out_shape=jax.ShapeDtypeStruct((n, m), F32),
        compiler_params=_cparams(("parallel",)),
        name="proj_in",
    )(x, w)


def _rope_tile(x, c, s1, s2):
    half = A_ROPE // 2
    return x * c + pltpu.roll(x, half, 1) * s1 + pltpu.roll(x, LANES - half, 1) * s2


def _mla_prep_kernel(a_ref, c_ref, s1_ref, s2_ref, gq_ref, gkv_ref, wuq_ref, wk_ref, wv_ref,
                     q_ref, k_ref, v_ref, kv_ref, kvt_ref):
    a = a_ref[...]
    c, s1, s2 = c_ref[...], s1_ref[...], s2_ref[...]
    cq = a[:, :A_QLORA]
    ckv = a[:, A_QLORA:A_QLORA + A_KVLORA]
    cq = cq * lax.rsqrt(jnp.mean(cq * cq, axis=-1, keepdims=True) + EPS) * gq_ref[...]
    ckv = ckv * lax.rsqrt(jnp.mean(ckv * ckv, axis=-1, keepdims=True) + EPS) * gkv_ref[...]
    kr = _rope_tile(a[:, GATE_COL:GATE_COL + LANES], c, s1, s2)
    kv_ref[:, :A_KVLORA] = ckv
    kv_ref[:, A_KVLORA:] = kr
    kvt_ref[:A_KVLORA, :] = ckv.T
    kvt_ref[A_KVLORA:, :] = kr.T
    q = _bdot(cq, wuq_ref[...])
    kn = _bdot(ckv, wk_ref[...])
    lane = lax.broadcasted_iota(jnp.int32, (1, ATT_W), 1)
    ones_col = jnp.where(lane % HEAD_PAD == A_V, 1.0, 0.0)
    v_ref[...] = (_bdot(ckv, wv_ref[...]) + ones_col).astype(BF16)
    for h in range(A_HEADS):
        sl = slice(h * HEAD_PAD, (h + 1) * HEAD_PAD)
        q_ref[:, sl] = _rope_tile(q[:, sl], c, s1, s2)
        k_ref[:, sl] = (kn[:, sl] + kr).astype(BF16)


def _mla_prep(proj, tabs, gq, gkv, wuq, wk, wv):
    n = proj.shape[0]
    row = lambda w: pl.BlockSpec((TM, w), lambda i: (i, 0))
    full = lambda a: pl.BlockSpec(a.shape, lambda i: (0,) * a.ndim)
    return pl.pallas_call(
        _mla_prep_kernel,
        grid=(n // TM,),
        in_specs=[row(GATE_COL + LANES), row(LANES), row(LANES), row(LANES),
                  full(gq), full(gkv), full(wuq), full(wk), full(wv)],
        out_specs=[row(ATT_W), row(ATT_W), row(ATT_W), row(A_KVLORA + LANES),
                   pl.BlockSpec((A_KVLORA + LANES, TM), lambda i: (0, i))],
        out_shape=[jax.ShapeDtypeStruct((n, ATT_W), F32),
                   jax.ShapeDtypeStruct((n, ATT_W), BF16),
                   jax.ShapeDtypeStruct((n, ATT_W), BF16),
                   jax.ShapeDtypeStruct((n, A_KVLORA + LANES), F32),
                   jax.ShapeDtypeStruct((A_KVLORA + LANES, n), F32)],
        compiler_params=_cparams(("parallel",)),
        name="mla_prep",
    )(proj, *tabs, gq, gkv, wuq, wk, wv)


ATT_SCALE = (A_NOPE + A_ROPE) ** -0.5


QK_SCALE2 = ATT_SCALE * math.log2(math.e)


def _flash_kernel(q_ref, k_ref, v_ref, km_ref, vm_ref, buf_ref, o_ref, qs_ref, *head_scr):
    del buf_ref
    m_refs, acc_refs = head_scr[:A_HEADS], head_scr[A_HEADS:]
    qi = pl.program_id(1)
    ki = pl.program_id(2)
    kdiag = (qi * TQ) // TK

    @pl.when(ki == 0)
    def _init():
        qs_ref[...] = (q_ref[...] * QK_SCALE2).astype(BF16)
        for h in range(A_HEADS):
            sl = slice(h * HEAD_PAD, (h + 1) * HEAD_PAD)
            s = _bdot_nt(qs_ref[:, sl], km_ref[:, sl])
            m = jnp.max(s, axis=1, keepdims=True)
            m_refs[h][...] = m
            acc_refs[h][...] = _bdot(jnp.exp2(s - m), vm_ref[:, sl])

    def step(masked):
        if masked:
            row = lax.broadcasted_iota(jnp.int32, (TQ, TK), 0)
            col = lax.broadcasted_iota(jnp.int32, (TQ, TK), 1)
            visible = col <= row + (qi * TQ - ki * TK)
        def scores(h):
            sl = slice(h * HEAD_PAD, (h + 1) * HEAD_PAD)
            return _bdot_nt(qs_ref[:, sl], k_ref[:, sl])

        s_next = scores(0)
        for h in range(A_HEADS):
            sl = slice(h * HEAD_PAD, (h + 1) * HEAD_PAD)
            s = s_next
            if h + 1 < A_HEADS:
                s_next = scores(h + 1)
            if masked:
                s = jnp.where(visible, s, -jnp.inf)
            m_prev = m_refs[h][...]
            m_new = jnp.maximum(m_prev, jnp.max(s, axis=1, keepdims=True))
            alpha = jnp.exp2(m_prev - m_new)
            p = jnp.exp2(s - m_new)
            acc_refs[h][...] = alpha * acc_refs[h][...] + _bdot(p, v_ref[:, sl])
            m_refs[h][...] = m_new

    pl.when(ki < kdiag)(functools.partial(step, False))
    pl.when(ki == kdiag)(functools.partial(step, True))

    @pl.when(ki == pl.num_programs(2) - 1)
    def _fin():
        for h in range(A_HEADS):
            a = acc_refs[h][...]
            o_ref[:, h * HEAD_PAD:(h + 1) * HEAD_PAD] = a / a[:, A_V:A_V + 1]


def _flash(q, k, v, obuf, nb, seq, meta_row):
    nq = seq // TQ
    nk = seq // TK
    meta_blk = meta_row // N_META
    qmap = lambda b, i, j: (b * nq + i, 0)
    kmap = lambda b, i, j: (b * nk + jnp.minimum(j, (i * TQ) // TK), 0)
    mmap = lambda b, i, j: (meta_blk, 0)
    return pl.pallas_call(
        _flash_kernel,
        grid=(nb, nq, nk),
        in_specs=[pl.BlockSpec((TQ, ATT_W), qmap),
                  pl.BlockSpec((TK, ATT_W), kmap),
                  pl.BlockSpec((TK, ATT_W), kmap),
                  pl.BlockSpec((N_META, ATT_W), mmap),
                  pl.BlockSpec((N_META, ATT_W), mmap),
                  pl.BlockSpec(memory_space=pl.ANY)],
        out_specs=pl.BlockSpec((TQ, ATT_W), qmap),
        out_shape=jax.ShapeDtypeStruct(obuf.shape, F32),
        scratch_shapes=([pltpu.VMEM((TQ, ATT_W), BF16)]
                        + [pltpu.VMEM((TQ, 1), F32)] * A_HEADS
                        + [pltpu.VMEM((TQ, HEAD_PAD), F32)] * A_HEADS),
        input_output_aliases={5: 0},
        compiler_params=_cparams(("parallel", "parallel", "arbitrary")),
        name="mla_flash",
    )(q, k, v, k, v, obuf)


def _meta_attn_kernel(q_ref, k_ref, v_ref, buf_ref, o_ref):
    del buf_ref
    n = q_ref.shape[0]
    row = lax.broadcasted_iota(jnp.int32, (n, n), 0)
    col = lax.broadcasted_iota(jnp.int32, (n, n), 1)
    for h in range(A_HEADS):
        sl = slice(h * HEAD_PAD, (h + 1) * HEAD_PAD)
        s = _bdot_nt(q_ref[:, sl], k_ref[:, sl]) * ATT_SCALE
        s = jnp.where(col <= row, s, -jnp.inf)
        p = jnp.exp(s - jnp.max(s, axis=1, keepdims=True))
        o_ref[:, sl] = _bdot(p, v_ref[:, sl]) / jnp.sum(p, axis=1, keepdims=True)


def _meta_attn(q, k, v, buf, meta_row):
    blk = meta_row // N_META
    spec = pl.BlockSpec((N_META, ATT_W), lambda i: (blk, 0))
    return pl.pallas_call(
        _meta_attn_kernel,
        grid=(1,),
        in_specs=[spec, spec, spec, pl.BlockSpec(memory_space=pl.ANY)],
        out_specs=spec,
        out_shape=jax.ShapeDtypeStruct(buf.shape, F32),
        input_output_aliases={3: 0},
        compiler_params=_cparams(("arbitrary",)),
        name="mla_meta_attn",
    )(q, k, v, buf)


PAGES_PER_STEP = 8
KV_ROPE0 = A_KVLORA + ROPE_LANE0


def _sattn_kernel(pt_ref, q_ref, kvn_ref, wuk_ref, wuv_ref, cache_ref, obuf_ref, o_ref, buf_ref, sem_ref, *, layer):
    del obuf_ref
    b = pl.program_id(0)
    nb = pl.num_programs(0)

    def page_copy(seq, p, slot):
        return pltpu.make_async_copy(cache_ref.at[layer, pt_ref[seq, p]], buf_ref.at[slot, p], sem_ref.at[slot])

    def start(seq, slot):
        def body(p, carry):
            page_copy(seq, p, slot).start()
            return carry
        lax.fori_loop(0, N_PAGES, body, 0)

    def wait(seq, slot):
        def body(p, carry):
            page_copy(seq, p, slot).wait()
            return carry
        lax.fori_loop(0, N_PAGES, body, 0)

    @pl.when(b == 0)
    def _first():
        start(0, 0)

    @pl.when(b + 1 < nb)
    def _next():
        start(b + 1, (b + 1) % 2)

    slot = b % 2
    wait(b, slot)

    q = q_ref[...]
    ql, qr = [], []
    for h in range(A_HEADS):
        ql.append(_bdot(q[:, h * HEAD_PAD:h * HEAD_PAD + A_NOPE], wuk_ref[h]))
        qr.append(q[:, h * HEAD_PAD + ROPE_LANE0:h * HEAD_PAD + ROPE_LANE0 + A_ROPE])
    ql = (jnp.concatenate(ql, axis=0) * QK_SCALE2).astype(BF16)
    qr = (jnp.concatenate(qr, axis=0) * QK_SCALE2).astype(BF16)
    rows = A_HEADS * S_ROWS
    keys = PAGES_PER_STEP * PAGE_SIZE

    m_prev = jnp.full((rows, 1), -jnp.inf, F32)
    l_prev = jnp.zeros((rows, 1), F32)
    acc = jnp.zeros((rows, A_KVLORA), F32)
    def scores(c):
        pages = [buf_ref[slot, c * PAGES_PER_STEP + i] for i in range(PAGES_PER_STEP)]
        ckv = jnp.concatenate([pg[:A_KVLORA].astype(BF16) for pg in pages], axis=1)
        kr = jnp.concatenate([pg[A_KVLORA:].astype(BF16) for pg in pages], axis=1)
        return ckv, _bdot(ql, ckv) + _bdot(qr, kr)

    n_chunks = N_PAGES // PAGES_PER_STEP
    nxt = scores(0)
    for c in range(n_chunks):
        ckv, s = nxt
        if c + 1 < n_chunks:
            nxt = scores(c + 1)
        m_new = jnp.maximum(m_prev, jnp.max(s, axis=1, keepdims=True))
        alpha = jnp.exp2(m_prev - m_new)
        p = jnp.exp2(s - m_new)
        l_prev = alpha * l_prev + jnp.sum(p, axis=1, keepdims=True)
        acc = alpha * acc + _bdot_nt(p, ckv)
        m_prev = m_new

    kvn = kvn_ref[...]
    ckv_n = kvn[:, :A_KVLORA]
    kr_n = kvn[:, KV_ROPE0:KV_ROPE0 + A_ROPE]
    s = _bdot_nt(ql, ckv_n) + _bdot_nt(qr, kr_n)
    t_row = lax.broadcasted_iota(jnp.int32, (rows, S_ROWS), 0) % S_ROWS
    t_col = lax.broadcasted_iota(jnp.int32, (rows, S_ROWS), 1)
    s = jnp.where(t_col <= jnp.minimum(t_row, DEC_SEQ - 1), s, -jnp.inf)
    m_new = jnp.maximum(m_prev, jnp.max(s, axis=1, keepdims=True))
    alpha = jnp.exp2(m_prev - m_new)
    p = jnp.exp2(s - m_new)
    l_new = alpha * l_prev + jnp.sum(p, axis=1, keepdims=True)
    o_lat = (alpha * acc + _bdot(p, ckv_n)) / l_new
    for h in range(A_HEADS):
        o_ref[:, h * HEAD_PAD:(h + 1) * HEAD_PAD] = _bdot(o_lat[h * S_ROWS:(h + 1) * S_ROWS], wuv_ref[h])


def _sample_attn(page_table, q, kvrow, wuk_t, wuv_p, cache_t, obuf, layer):
    nb = page_table.shape[0]
    blk0 = ROW_S // S_ROWS
    grid_spec = pltpu.PrefetchScalarGridSpec(
        num_scalar_prefetch=1,
        grid=(nb,),
        in_specs=[pl.BlockSpec((S_ROWS, ATT_W), lambda b, pt: (blk0 + b, 0)),
                  pl.BlockSpec((S_ROWS, A_KVLORA + LANES), lambda b, pt: (blk0 + b, 0)),
                  pl.BlockSpec(wuk_t.shape, lambda b, pt: (0, 0, 0)),
                  pl.BlockSpec(wuv_p.shape, lambda b, pt: (0, 0, 0)),
                  pl.BlockSpec(memory_space=pl.ANY),
                  pl.BlockSpec(memory_space=pl.ANY)],
        out_specs=pl.BlockSpec((S_ROWS, ATT_W), lambda b, pt: (blk0 + b, 0)),
        scratch_shapes=[pltpu.VMEM((2, N_PAGES, A_LAT, PAGE_SIZE), F32),
                        pltpu.SemaphoreType.DMA((2,))],
    )
    return pl.pallas_call(
        functools.partial(_sattn_kernel, layer=layer),
        grid_spec=grid_spec,
        out_shape=jax.ShapeDtypeStruct(obuf.shape, F32),
        input_output_aliases={6: 0},
        compiler_params=_cparams(("arbitrary",)),
        name="mla_sample_attn",
    )(page_table, q, kvrow, wuk_t, wuv_p, cache_t, obuf)


def _row_of(col_vec, eye):
    return jnp.sum(jnp.where(eye, col_vec, 0.0), axis=0, keepdims=True)


def _mlstm_kernel(g_ref, q_ref, k_ref, v_ref, og_ref, c0_ref, n0_ref, m0_ref, bi_ref, bf_ref, gn_ref, buf_ref,
                  o_ref, c_ref, n_ref, m_ref, c_scr, n_scr, m_scr, *, n_valid, seqs):
    del buf_ref
    ci = pl.program_id(1)
    L = q_ref.shape[0] // seqs

    @pl.when(ci == 0)
    def _load():
        c_scr[...] = c0_ref[...]
        for s in range(seqs):
            n_scr[s, 0:B_HEADS, :] = n0_ref[s]
            for h in range(B_HEADS):
                m_scr[s, h:h + 1, :] = jnp.broadcast_to(m0_ref[s, 0:1, h:h + 1], (1, LANES))

    row = lax.broadcasted_iota(jnp.int32, (L, L), 0)
    col = lax.broadcasted_iota(jnp.int32, (L, L), 1)
    eye = row == col
    tril = col <= row
    valid = lax.broadcasted_iota(jnp.int32, (L, 1), 0) < n_valid
    lane = lax.broadcasted_iota(jnp.int32, (1, LANES), 1)
    m_outs = []
    for s in range(seqs):
        rs = slice(s * L, (s + 1) * L)
        g = g_ref[rs, :]
        qs, ks, vs, qk, qc = [], [], [], [], []
        for h in range(B_HEADS):
            sl = slice(h * B_DK, (h + 1) * B_DK)
            qs.append(q_ref[rs, sl])
            ks.append(k_ref[rs, sl] * (B_DK ** -0.5))
            vs.append(v_ref[rs, sl])
            qk.append(_bdot_nt(qs[h], ks[h]))
            qc.append(_bdot(qs[h], c_scr[s, h]))
        m_out = jnp.zeros((1, LANES), F32)
        for h in range(B_HEADS):
            sl = slice(h * B_DK, (h + 1) * B_DK)
            q, k, v = qs[h], ks[h], vs[h]
            ig_col = jnp.where(valid, g[:, MI_LANE + h:MI_LANE + h + 1] + bi_ref[0:1, h:h + 1], -jnp.inf)
            fpre = g[:, MF_LANE + h:MF_LANE + h + 1] + bf_ref[0:1, h:h + 1]
            lf_col = jnp.where(valid, jnp.minimum(fpre, 0.0) - jnp.log1p(jnp.exp(-jnp.abs(fpre))), 0.0)
            ig_row = _row_of(ig_col, eye)
            lf_row = _row_of(lf_col, eye)
            b_col = jnp.sum(jnp.where(tril, lf_row, 0.0), axis=1, keepdims=True)
            b_row = jnp.sum(jnp.where(row <= col, lf_col, 0.0), axis=0, keepdims=True)
            m_prev = m_scr[s, h:h + 1, 0:1]
            c_prev = c_scr[s, h]
            n_prev = n_scr[s, h:h + 1, :]
            dmat = jnp.where(tril, b_col - b_row + ig_row, -jnp.inf)
            inter = b_col + m_prev
            m_t = jnp.maximum(inter, jnp.max(dmat, axis=1, keepdims=True))
            w_inter = jnp.exp(inter - m_t)
            sc = qk[h] * jnp.exp(dmat - m_t)
            num = w_inter * qc[h] + _bdot(sc, v)
            den = w_inter * jnp.sum(q * n_prev, axis=1, keepdims=True) + jnp.sum(sc, axis=1, keepdims=True)
            hh = num / jnp.maximum(jnp.abs(den), jnp.exp(-m_t))
            m_new = m_t[n_valid - 1:n_valid, :]
            b_last = b_col[n_valid - 1:n_valid, :]
            w_end = jnp.exp(b_last - b_col + ig_col - m_new)
            decay = jnp.exp(b_last + m_prev - m_new)
            kw = k * w_end
            c_scr[s, h] = decay * c_prev + _bdot_tn(kw, v)
            n_scr[s, h:h + 1, :] = decay * n_prev + jnp.sum(kw, axis=0, keepdims=True)
            m_scr[s, h:h + 1, :] = jnp.broadcast_to(m_new, (1, LANES))
            m_out = jnp.where(lane == h, m_new, m_out)
            o_ref[rs, sl] = _head_norm(hh, gn_ref[0:1, sl]) * _sigmoid(og_ref[rs, sl])
        m_outs.append(m_out)

    @pl.when(ci == pl.num_programs(1) - 1)
    def _store():
        c_ref[...] = c_scr[...]
        for s in range(seqs):
            n_ref[s] = n_scr[s, 0:B_HEADS, :]
            m_ref[s] = m_outs[s]


def _stream_out(obuf, n_in, w, row0, nc, L):
    blk0 = row0 // L
    spec = pl.BlockSpec((L, w), lambda b, c: (blk0 + b * nc + c, 0))
    return spec, [pl.BlockSpec(memory_space=pl.ANY)], [obuf], {n_in: 0}


def _mlstm(proj, row0, nb, nc, L, n_valid, c0, n0, m0, bi, bf, gn, shared_state, obuf, seqs=1, sb0=0):
    assert seqs == 1 or (nc == 1 and not shared_state and nb % seqs == 0)
    R = L * seqs
    blk0 = row0 // R
    rmap = lambda cb: (lambda b, c: (blk0 + b * nc + c, cb))
    smap = (lambda b, c: (0, 0, 0, 0)) if shared_state else (lambda b, c: (sb0 + b, 0, 0, 0))
    smap3 = (lambda b, c: (0, 0, 0)) if shared_state else (lambda b, c: (sb0 + b, 0, 0))
    w = B_HEADS * B_DK
    full2 = lambda a: pl.BlockSpec(a.shape, lambda b, c: (0, 0))
    ospec, xspec, xarg, alias = _stream_out(obuf, 11, w, row0, nc, R)
    return pl.pallas_call(
        functools.partial(_mlstm_kernel, n_valid=n_valid, seqs=seqs),
        grid=(nb // seqs, nc),
        in_specs=[pl.BlockSpec((R, LANES), rmap(GATE_COL // LANES)),
                  pl.BlockSpec((R, w), rmap(2)), pl.BlockSpec((R, w), rmap(3)),
                  pl.BlockSpec((R, w), rmap(4)), pl.BlockSpec((R, w), rmap(5)),
                  pl.BlockSpec((seqs, B_HEADS, B_DK, B_DV), smap),
                  pl.BlockSpec((seqs, B_HEADS, B_DK), smap3),
                  pl.BlockSpec((seqs, 1, B_HEADS), smap3),
                  full2(bi), full2(bf), full2(gn)] + xspec,
        out_specs=[ospec,
                   pl.BlockSpec((seqs, B_HEADS, B_DK, B_DV), lambda b, c: (b, 0, 0, 0)),
                   pl.BlockSpec((seqs, B_HEADS, B_DK), lambda b, c: (b, 0, 0)),
                   pl.BlockSpec((seqs, 1, LANES), lambda b, c: (b, 0, 0))],
        out_shape=[jax.ShapeDtypeStruct((proj.shape[0], w), F32),
                   jax.ShapeDtypeStruct((nb, B_HEADS, B_DK, B_DV), F32),
                   jax.ShapeDtypeStruct((nb, B_HEADS, B_DK), F32),
                   jax.ShapeDtypeStruct((nb, 1, LANES), F32)],
        scratch_shapes=[pltpu.VMEM((seqs, B_HEADS, B_DK, B_DV), F32),
                        pltpu.VMEM((seqs, SUBLANES, B_DK), F32),
                        pltpu.VMEM((seqs, SUBLANES, LANES), F32)],
        input_output_aliases=alias,
        compiler_params=_cparams(("parallel", "arbitrary")),
        name="mlstm",
    )(proj, proj, proj, proj, proj, c0, n0, m0, bi, bf, gn, *xarg)


RET_LOG_GAMMA = tuple(math.log1p(-(2.0 ** (-5.0 - h))) for h in range(D_HEADS))


def _ret_kernel(q_ref, k_ref, v_ref, g_ref, cr_ref, sr_ref, s0_ref, gn_ref, buf_ref, o_ref, s_ref, s_scr, *,
                n_valid, seqs):
    del buf_ref
    ci = pl.program_id(1)
    L = q_ref.shape[0] // seqs

    @pl.when(ci == 0)
    def _load():
        s_scr[...] = s0_ref[...]

    row = lax.broadcasted_iota(jnp.int32, (L, L), 0)
    col = lax.broadcasted_iota(jnp.int32, (L, L), 1)
    rel = (row - col).astype(F32)
    t_col = lax.broadcasted_iota(jnp.int32, (L, 1), 0)
    tf = t_col.astype(F32)
    half = D_DK // 2
    for s in range(seqs):
        rs = slice(s * L, (s + 1) * L)
        cr, sr = cr_ref[rs, :], sr_ref[rs, :]
        qs, ks, vs, qk, qs_state = [], [], [], [], []
        for h in range(D_HEADS):
            sl = slice(h * D_DK, (h + 1) * D_DK)
            q = q_ref[rs, sl]
            k = k_ref[rs, sl]
            qs.append(q * cr + pltpu.roll(q, half, 1) * sr)
            ks.append((k * cr + pltpu.roll(k, half, 1) * sr) * (D_DK ** -0.5))
            vs.append(v_ref[rs, sl])
            qk.append(_bdot_nt(qs[h], ks[h]))
            qs_state.append(_bdot(qs[h], s_scr[s, h]))
        for h in range(D_HEADS):
            lg = RET_LOG_GAMMA[h]
            sl = slice(h * D_DK, (h + 1) * D_DK)
            k, v = ks[h], vs[h]
            dec = jnp.where(rel >= 0, jnp.exp(lg * jnp.maximum(rel, 0.0)), 0.0)
            inner = qk[h] * dec
            o = _bdot(inner, v) + jnp.exp((tf + 1.0) * lg) * qs_state[h]
            w_end = jnp.where(t_col < n_valid, jnp.exp((n_valid - 1.0 - tf) * lg), 0.0)
            s_scr[s, h] = math.exp(n_valid * lg) * s_scr[s, h] + _bdot_tn(k * w_end, v)
            gate = g_ref[rs, sl]
            o_ref[rs, sl] = _head_norm(o, gn_ref[0:1, sl]) * (gate * _sigmoid(gate))

    @pl.when(ci == pl.num_programs(1) - 1)
    def _store():
        s_ref[...] = s_scr[...]


def _retention(proj, tabs, row0, nb, nc, L, n_valid, s0, gn, shared_state, obuf, seqs=1, sb0=0):
    assert seqs == 1 or (nc == 1 and not shared_state and nb % seqs == 0)
    R = L * seqs
    blk0 = row0 // R
    rmap = lambda cb: (lambda b, c: (blk0 + b * nc + c, cb))
    smap = (lambda b, c: (0, 0, 0, 0)) if shared_state else (lambda b, c: (sb0 + b, 0, 0, 0))
    w = D_HEADS * D_DK
    ospec, xspec, xarg, alias = _stream_out(obuf, 8, w, row0, nc, R)
    return pl.pallas_call(
        functools.partial(_ret_kernel, n_valid=n_valid, seqs=seqs),
        grid=(nb // seqs, nc),
        in_specs=[pl.BlockSpec((R, w), rmap(2)), pl.BlockSpec((R, w), rmap(3)),
                  pl.BlockSpec((R, w), rmap(4)), pl.BlockSpec((R, w), rmap(5)),
                  pl.BlockSpec((R, LANES), rmap(0)), pl.BlockSpec((R, LANES), rmap(0)),
                  pl.BlockSpec((seqs, D_HEADS, D_DK, D_DV), smap),
                  pl.BlockSpec(gn.shape, lambda b, c: (0, 0))] + xspec,
        out_specs=[ospec,
                   pl.BlockSpec((seqs, D_HEADS, D_DK, D_DV), lambda b, c: (b, 0, 0, 0))],
        out_shape=[jax.ShapeDtypeStruct((proj.shape[0], w), F32),
                   jax.ShapeDtypeStruct((nb, D_HEADS, D_DK, D_DV), F32)],
        scratch_shapes=[pltpu.VMEM((seqs, D_HEADS, D_DK, D_DV), F32)],
        input_output_aliases=alias,
        compiler_params=_cparams(("parallel", "arbitrary")),
        name="retention",
    )(proj, proj, proj, proj, tabs[0], tabs[1], s0, gn, *xarg)


def _rglru_kernel(x_ref, gr_ref, h0_ref, cv0_ref, cw_ref, cb_ref, wa_ref, ba_ref, wi_ref, bi_ref, lam_ref, buf_ref,
                  o_ref, h_ref, cv_ref, xp_scr, h_scr, *, n_valid, seqs):
    del buf_ref
    ci = pl.program_id(1)
    R = x_ref.shape[0]
    L = R // seqs
    tail = C_CONV - 1

    @pl.when(ci == 0)
    def _load():
        for s in range(seqs):
            xp_scr[s, SUBLANES - tail:SUBLANES, :] = cv0_ref[s]
            h_scr[s, 0:1, :] = h0_ref[s]

    us, h0s, new_tails = [], [], []
    for s in range(seqs):
        xp_scr[s, SUBLANES:SUBLANES + L, :] = x_ref[s * L:(s + 1) * L, :]
        us.append(sum(xp_scr[s, SUBLANES - tail + j:SUBLANES - tail + j + L, :] * cw_ref[j:j + 1, :]
                      for j in range(C_CONV)))
        new_tails.append(xp_scr[s, SUBLANES + n_valid - tail:SUBLANES + n_valid, :])
        xp_scr[s, SUBLANES - tail:SUBLANES, :] = new_tails[s]
        h0s.append(jnp.broadcast_to(h_scr[s, 0:1, :], (L, C_WIDTH)))
    u = cb_ref[...] + (us[0] if seqs == 1 else jnp.concatenate(us, axis=0))
    h_prev = h0s[0] if seqs == 1 else jnp.concatenate(h0s, axis=0)
    r = _sigmoid(_bdot(u, wa_ref[...]) + ba_ref[...])
    i = _sigmoid(_bdot(u, wi_ref[...]) + bi_ref[...])
    nlam = -lam_ref[...]
    softplus = jnp.maximum(nlam, 0.0) + jnp.log1p(jnp.exp(-jnp.abs(nlam)))
    log_a = -C_POW * r * softplus
    a = jnp.exp(log_a)
    b = jnp.sqrt(-jnp.tanh(log_a) * (a * a + 1.0)) * (i * u)
    t_idx = lax.broadcasted_iota(jnp.int32, (R, 1), 0) % L
    step = 1
    while step < L:
        keep = t_idx >= step
        a_sh = jnp.where(keep, pltpu.roll(a, step, 0), 1.0)
        b_sh = jnp.where(keep, pltpu.roll(b, step, 0), 0.0)
        b = a * b_sh + b
        a = a * a_sh
        step *= 2
    hseq = a * h_prev + b
    h_lasts = [hseq[s * L + n_valid - 1:s * L + n_valid, :] for s in range(seqs)]
    for s in range(seqs):
        h_scr[s, 0:1, :] = h_lasts[s]
    gr = gr_ref[...]
    gelu = 0.5 * gr * (1.0 + jnp.tanh(math.sqrt(2.0 / math.pi) * (gr + 0.044715 * (gr * gr * gr))))
    o_ref[...] = hseq * gelu

    @pl.when(ci == pl.num_programs(1) - 1)
    def _store():
        for s in range(seqs):
            h_ref[s] = h_lasts[s]
            cv_ref[s] = new_tails[s]


def _rglru(proj, row0, nb, nc, L, n_valid, h0, cv0, cw, cb, wa, ba, wi, bi, lam, shared_state, obuf, seqs=1, sb0=0):
    assert seqs == 1 or (nc == 1 and not shared_state and nb % seqs == 0)
    R = L * seqs
    blk0 = row0 // R
    rmap = lambda cb_: (lambda b, c: (blk0 + b * nc + c, cb_))
    smap = (lambda b, c: (0, 0, 0)) if shared_state else (lambda b, c: (sb0 + b, 0, 0))
    full2 = lambda a: pl.BlockSpec(a.shape, lambda b, c: (0, 0))
    w = C_WIDTH
    ospec, xspec, xarg, alias = _stream_out(obuf, 11, w, row0, nc, R)
    return pl.pallas_call(
        functools.partial(_rglru_kernel, n_valid=n_valid, seqs=seqs),
        grid=(nb // seqs, nc),
        in_specs=[pl.BlockSpec((R, w), rmap(0)), pl.BlockSpec((R, w), rmap(1)),
                  pl.BlockSpec((seqs, 1, w), smap), pl.BlockSpec((seqs, C_CONV - 1, w), smap),
                  full2(cw), full2(cb), full2(wa), full2(ba), full2(wi), full2(bi), full2(lam)] + xspec,
        out_specs=[ospec,
                   pl.BlockSpec((seqs, 1, w), lambda b, c: (b, 0, 0)),
                   pl.BlockSpec((seqs, C_CONV - 1, w), lambda b, c: (b, 0, 0))],
        out_shape=[jax.ShapeDtypeStruct((proj.shape[0], w), F32),
                   jax.ShapeDtypeStruct((nb, 1, w), F32),
                   jax.ShapeDtypeStruct((nb, C_CONV - 1, w), F32)],
        scratch_shapes=[pltpu.VMEM((seqs, SUBLANES + L, w), F32), pltpu.VMEM((seqs, SUBLANES, w), F32)],
        input_output_aliases=alias,
        compiler_params=_cparams(("parallel", "arbitrary")),
        name="rglru",
    )(proj, proj, h0, cv0, cw, cb, wa, ba, wi, bi, lam, *xarg)


def _outproj_kernel(o1_ref, o2_ref, w1_ref, w2_ref, x_ref, g_ref, b_ref, y_ref):
    mix = _bdot(o1_ref[...], w1_ref[...]) + _bdot(o2_ref[...], w2_ref[...])
    y_ref[...] = _layer_norm(DN_ALPHA * x_ref[...] + mix, g_ref[...], b_ref[...])


def _outproj_ln(o1, o2, w1, w2, x, g, b):
    n = x.shape[0]
    row = lambda a: pl.BlockSpec((TM, a.shape[1]), lambda i: (i, 0))
    full = lambda a: pl.BlockSpec(a.shape, lambda i: (0, 0))
    return pl.pallas_call(
        _outproj_kernel,
        grid=(n // TM,),
        in_specs=[row(o1), row(o2), full(w1), full(w2), row(x), full(g), full(b)],
        out_specs=row(x),
        out_shape=jax.ShapeDtypeStruct(x.shape, F32),
        compiler_params=_cparams(("parallel",)),
        name="outproj_ln",
    )(o1, o2, w1, w2, x, g, b)


ROUTER_E0 = N_GROUPS
MOE_EXPERTS_PER_CHUNK = 2


def _moe_kernel(x_ref, wr_ref, br_ref, w1_ref, w3_ref, w2_ref, g_ref, b_ref, y_ref, xb_scr, gate_scr, acc_scr):
    j = pl.program_id(1)

    @pl.when(j == 0)
    def _route():
        x = x_ref[...]
        xh = x.astype(BF16)
        xb_scr[...] = xh
        xl = (x - xh.astype(F32)).astype(BF16)
        wr = wr_ref[...]
        wh = wr.astype(BF16)
        wl = (wr - wh.astype(F32)).astype(BF16)
        dot = lambda a, b: jnp.dot(a, b, preferred_element_type=F32)
        logits = dot(xh, wh) + (dot(xh, wl) + dot(xl, wh)) + br_ref[...]
        lane = lax.broadcasted_iota(jnp.int32, logits.shape, 1)
        gl = jnp.where(lane < N_GROUPS, logits, -jnp.inf)
        gmax = jnp.max(gl, axis=1, keepdims=True)
        g_gate = 1.0 / jnp.sum(jnp.exp(gl - gmax), axis=1, keepdims=True)
        g_idx = jnp.min(jnp.where(gl == gmax, lane, LANES), axis=1, keepdims=True)
        e_lo = ROUTER_E0 + g_idx * E_PER_GROUP
        el = jnp.where(jnp.logical_and(lane >= e_lo, lane < e_lo + E_PER_GROUP), logits, -jnp.inf)
        e1 = jnp.max(el, axis=1, keepdims=True)
        i1 = jnp.min(jnp.where(el == e1, lane, LANES), axis=1, keepdims=True)
        el2 = jnp.where(lane == i1, -jnp.inf, el)
        e2 = jnp.max(el2, axis=1, keepdims=True)
        i2 = jnp.min(jnp.where(el2 == e2, lane, LANES), axis=1, keepdims=True)
        r = jnp.exp(e2 - e1)
        w1 = g_gate / (1.0 + r)
        gate = jnp.where(lane == i1, w1, 0.0) + jnp.where(lane == i2, w1 * r, 0.0)
        for gi in range(N_GROUPS):
            lo = ROUTER_E0 + gi * E_PER_GROUP
            gate_scr[gi] = gate[:, lo:lo + E_PER_GROUP]
        acc_scr[...] = jnp.zeros_like(acc_scr)

    xb = xb_scr[...]
    gate = gate_scr[j]
    cw = MOE_EXPERTS_PER_CHUNK * D_EXPERT

    def up(c):
        es = range(c * MOE_EXPERTS_PER_CHUNK, (c + 1) * MOE_EXPERTS_PER_CHUNK)
        r1 = jnp.concatenate([w1_ref[0, e] for e in es], axis=1)
        r3 = jnp.concatenate([w3_ref[0, e] for e in es], axis=1)
        return (jnp.dot(xb, r1, preferred_element_type=F32), jnp.dot(xb, r3, preferred_element_type=F32))

    parts = []
    nxt = up(0)
    for c in range(E_PER_GROUP // MOE_EXPERTS_PER_CHUNK):
        h1, h3 = nxt
        if (c + 1) * MOE_EXPERTS_PER_CHUNK < E_PER_GROUP:
            nxt = up(c + 1)
        hg = h1 * _sigmoid(h1) * h3
        for ee in range(MOE_EXPERTS_PER_CHUNK):
            e = c * MOE_EXPERTS_PER_CHUNK + ee
            parts.append((hg[:, ee * D_EXPERT:(ee + 1) * D_EXPERT] * gate[:, e:e + 1]).astype(BF16))
    acc_scr[...] += jnp.dot(jnp.concatenate(parts, axis=1), w2_ref[0].astype(BF16), preferred_element_type=F32)

    @pl.when(j == pl.num_programs(1) - 1)
    def _fin():
        y_ref[...] = _layer_norm(DN_ALPHA * x_ref[...] + acc_scr[...], g_ref[...], b_ref[...])


def _moe_ln(x, wr, br, w1, w3, w2, g, b, layer):
    n = x.shape[0]
    gw = E_PER_GROUP * D_EXPERT
    row = pl.BlockSpec((TM, D_MODEL), lambda i, j: (i, 0))
    full = lambda a: pl.BlockSpec(a.shape, lambda i, j: (0, 0))
    return pl.pallas_call(
        _moe_kernel,
        grid=(n // TM, N_GROUPS),
        in_specs=[row, full(wr), full(br),
                  pl.BlockSpec((1, E_PER_GROUP, D_MODEL, D_EXPERT), lambda i, j: (layer, j, 0, 0)),
                  pl.BlockSpec((1, E_PER_GROUP, D_MODEL, D_EXPERT), lambda i, j: (layer, j, 0, 0)),
                  pl.BlockSpec((1, gw, D_MODEL), lambda i, j: (layer * N_GROUPS + j, 0, 0)),
                  full(g), full(b)],
        out_specs=row,
        out_shape=jax.ShapeDtypeStruct(x.shape, F32),
        scratch_shapes=[pltpu.VMEM((TM, D_MODEL), BF16),
                        pltpu.VMEM((N_GROUPS, TM, E_PER_GROUP), F32),
                        pltpu.VMEM((TM, D_MODEL), F32)],
        compiler_params=_cparams(("parallel", "arbitrary")),
        name="moe_ln",
    )(x, wr, br, w1, w3, w2, g, b)


def _positions():
    pos = jnp.zeros((N_TOK,), F32)
    main = (N_META + jnp.arange(SEQ, dtype=jnp.int32)).astype(F32)
    pos = pos.at[:N_MAIN].set(jnp.tile(main, BATCH))
    samp = (PAST_LEN + jnp.minimum(jnp.arange(S_ROWS, dtype=jnp.int32), DEC_SEQ - 1)).astype(F32)
    pos = pos.at[ROW_S:ROW_M].set(jnp.tile(samp, DEC_BATCH))
    pos = pos.at[ROW_M:ROW_M + N_META].set(jnp.arange(N_META, dtype=jnp.int32).astype(F32))
    return pos


def _rope_tables_mla(pos):
    half = A_ROPE // 2
    freqs = ROPE_THETA ** (-jnp.arange(half, dtype=F32) / half)
    ang = pos[:, None] * freqs
    cos, sin = jnp.cos(ang), jnp.sin(ang)
    n = pos.shape[0]
    zeros = lambda w: jnp.zeros((n, w), F32)
    rest = LANES - ROPE_LANE0 - A_ROPE
    c = jnp.concatenate([jnp.ones((n, ROPE_LANE0), F32), cos, cos, zeros(rest)], axis=1)
    s1 = jnp.concatenate([zeros(ROPE_LANE0 + half), sin, zeros(rest)], axis=1)
    s2 = jnp.concatenate([zeros(ROPE_LANE0), -sin, zeros(half + rest)], axis=1)
    return c, s1, s2


def _rope_tables_ret(pos):
    half = D_DK // 2
    freqs = ROPE_THETA ** (-jnp.arange(half, dtype=F32) / half)
    ang = pos[:, None] * freqs
    cos, sin = jnp.cos(ang), jnp.sin(ang)
    return jnp.concatenate([cos, cos], axis=1), jnp.concatenate([-sin, sin], axis=1)


def _pad_cols(a, w):
    return jnp.pad(a, ((0, 0), (0, w - a.shape[1])))


def _even_in_weight(w):
    cuts = [0, A_QLORA, A_QLORA + A_KVLORA, A_QLORA + A_KVLORA + A_ROPE]
    c_q, c_kv, k_r = (w[:, cuts[i]:cuts[i + 1]] for i in range(3))
    m0 = cuts[3]
    mw = B_HEADS * B_DK
    big = w[:, m0:m0 + 4 * mw]
    gates = w[:, m0 + 4 * mw:]
    head = jnp.concatenate([c_q, c_kv, jnp.zeros((D_MODEL, ROPE_LANE0), w.dtype), k_r, gates], axis=1)
    return jnp.concatenate([_pad_cols(head, 1024), big], axis=1).astype(BF16)


def _head_pad_cols(a, width):
    k = a.shape[0]
    return jnp.pad(a, ((0, 0), (0, 0), (0, HEAD_PAD - width))).reshape(k, A_HEADS * HEAD_PAD)


def _block_diag(w):
    eye = jnp.eye(C_BLOCKS, dtype=w.dtype)
    return (w[:, :, None, :] * eye[:, None, :, None]).reshape(C_WIDTH, C_WIDTH)


def _assemble(main, samp, meta):
    w = main.shape[1]
    pad = jnp.zeros((N_TOK - ROW_M - N_META, w), main.dtype)
    return jnp.concatenate([main, samp, meta, pad], axis=0)


def kernel(x_prompt, x_sample, cache_mla, page_table, state_mlstm_c, state_mlstm_n, state_mlstm_m,
           state_rglru_h, state_rglru_conv, state_ret, meta_tokens, w_in_e, g_cq, g_ckv, w_uq, w_uk, w_uv,
           b_mi, b_mf, g_mlstm, w_out_e, w_in_o, conv_w, conv_b, w_ra, b_ra, w_ri, b_ri, lam, g_ret, w_out_o,
           ln_mix_g, ln_mix_b, ln_ffn_g, ln_ffn_b, w_gr, b_gr, w_er, b_er, w_e1, w_e3, w_e2):
    nc_main = SEQ // CHUNK
    xs = jnp.pad(x_sample, ((0, 0), (0, S_ROWS - DEC_SEQ), (0, 0))).reshape(N_SAMP, D_MODEL)
    x = _assemble(x_prompt.reshape(N_MAIN, D_MODEL), xs, meta_tokens)

    cache_t = jnp.swapaxes(cache_mla, 2, 3)
    w1_all, w3_all = w_e1.astype(BF16), w_e3.astype(BF16)
    w2_all = w_e2.reshape(DEPTH * N_GROUPS, E_PER_GROUP * D_EXPERT, D_MODEL)
    st_c = state_mlstm_c.reshape(N_EVEN * DEC_BATCH, B_HEADS, B_DK, B_DV)
    st_n = state_mlstm_n.reshape(N_EVEN * DEC_BATCH, B_HEADS, B_DK)
    st_m = state_mlstm_m.reshape(N_EVEN * DEC_BATCH, 1, B_HEADS)
    st_h = state_rglru_h.reshape(N_ODD * DEC_BATCH, 1, C_WIDTH)
    st_cv = state_rglru_conv.reshape(N_ODD * DEC_BATCH, C_CONV - 1, C_WIDTH)
    st_s = state_ret.reshape(N_ODD * DEC_BATCH, D_HEADS, D_DK, D_DV)

    pos = _positions()
    tabs_mla = _rope_tables_mla(pos)
    tabs_ret = _rope_tables_ret(pos)
    row2 = lambda a: a.reshape(1, -1)
    spare = {}

    def zbuf(w):
        return spare[w].pop() if spare.get(w) else jnp.zeros((N_TOK, w), F32)

    kv_p, kv_s, c_p, c_s, n_p, n_s, m_p, m_s = [], [], [], [], [], [], [], []
    h_p, h_s, cv_p, cv_s, s_p, s_s = [], [], [], [], [], []

    for l in range(DEPTH):
        j = l // 2
        if l % 2 == 0:
            proj = _matmul(x, _even_in_weight(w_in_e[j]))
            wuq = _head_pad_cols(w_uq[j].reshape(A_QLORA, A_HEADS, A_NOPE + A_ROPE), A_NOPE + A_ROPE).astype(BF16)
            wk = _head_pad_cols(w_uk[j], A_NOPE).astype(BF16)
            wv = _head_pad_cols(w_uv[j], A_V).astype(BF16)
            q, kp, vp, kvrow, kvt = _mla_prep(proj, tabs_mla, row2(g_cq[j]), row2(g_ckv[j]), wuq, wk, wv)
            wuk_t = jnp.transpose(w_uk[j], (1, 2, 0)).astype(BF16)
            wuv_p = jnp.pad(jnp.transpose(w_uv[j], (1, 0, 2)),
                            ((0, 0), (0, 0), (0, HEAD_PAD - A_V))).astype(BF16)
            o1 = _flash(q, kp, vp, zbuf(ATT_W), BATCH, SEQ, ROW_M)
            o1 = _sample_attn(page_table, q, kvrow, wuk_t, wuv_p, cache_t, o1, j)
            o1 = _meta_attn(q, kp, vp, o1, ROW_M)

            bi, bf, gn = row2(b_mi[j]), row2(b_mf[j]), row2(g_mlstm[j])
            zc = jnp.zeros((1, B_HEADS, B_DK, B_DV), F32)
            zn = jnp.zeros((1, B_HEADS, B_DK), F32)
            zm = jnp.zeros((1, 1, B_HEADS), F32)
            o2, c_m, n_m, m_m = _mlstm(proj, ROW_M, 1, 1, N_META, N_META, zc, zn, zm, bi, bf, gn, True,
                                       zbuf(B_HEADS * B_DV))
            o2, c_pm, n_pm, m_pm = _mlstm(proj, 0, BATCH, nc_main, CHUNK, CHUNK, c_m, n_m,
                                          m_m[:, :, :B_HEADS], bi, bf, gn, True, o2)
            o2, c_sm, n_sm, m_sm = _mlstm(proj, ROW_S, DEC_BATCH, 1, S_ROWS, DEC_SEQ, st_c, st_n, st_m,
                                          bi, bf, gn, False, o2, SEQS_PER_STEP, j * DEC_BATCH // SEQS_PER_STEP)

            w1 = jnp.pad(w_out_e[j][:A_HEADS * A_V].reshape(A_HEADS, A_V, D_MODEL),
                         ((0, 0), (0, HEAD_PAD - A_V), (0, 0))).reshape(ATT_W, D_MODEL).astype(BF16)
            w2 = w_out_e[j][A_HEADS * A_V:].astype(BF16)

            kvf = jnp.concatenate([kvt[:A_KVLORA], kvt[KV_ROPE0:KV_ROPE0 + A_ROPE]], axis=0)
            kv_main = kvf[:, :N_MAIN].reshape(A_LAT, BATCH, SEQ)
            kv_meta = jnp.broadcast_to(kvf[:, None, ROW_M:ROW_M + N_META], (A_LAT, BATCH, N_META))
            kv_p.append(jnp.transpose(jnp.concatenate([kv_meta, kv_main], axis=2), (1, 2, 0)))
            kv_sr = kvf[:, ROW_S:ROW_M].reshape(A_LAT, DEC_BATCH, S_ROWS)[:, :, :DEC_SEQ]
            kv_s.append(jnp.transpose(kv_sr, (1, 2, 0)))
            c_p.append(c_pm); n_p.append(n_pm); m_p.append(m_pm[:, 0, :B_HEADS])
            c_s.append(c_sm); n_s.append(n_sm); m_s.append(m_sm[:, 0, :B_HEADS])
        else:
            proj = _matmul(x, w_in_o[j].astype(BF16))
            cw, cb = conv_w[j], row2(conv_b[j])
            wa, wi = _block_diag(w_ra[j]).astype(BF16), _block_diag(w_ri[j]).astype(BF16)
            ba, bi_, lam_ = row2(b_ra[j]), row2(b_ri[j]), row2(lam[j])
            zh = jnp.zeros((1, 1, C_WIDTH), F32)
            zcv = jnp.zeros((1, C_CONV - 1, C_WIDTH), F32)
            args = (cw, cb, wa, ba, wi, bi_, lam_)
            o1, h_m, cv_m = _rglru(proj, ROW_M, 1, 1, N_META, N_META, zh, zcv, *args, True, zbuf(C_WIDTH))
            o1, h_pm, cv_pm = _rglru(proj, 0, BATCH, nc_main, CHUNK, CHUNK, h_m, cv_m, *args, True, o1)
            o1, h_sm, cv_sm = _rglru(proj, ROW_S, DEC_BATCH, 1, S_ROWS, DEC_SEQ, st_h, st_cv,
                                     *args, False, o1, SEQS_PER_STEP_RGLRU, j * DEC_BATCH // SEQS_PER_STEP_RGLRU)

            gn = row2(g_ret[j])
            zs = jnp.zeros((1, D_HEADS, D_DK, D_DV), F32)
            o2, s_m = _retention(proj, tabs_ret, ROW_M, 1, 1, N_META, N_META, zs, gn, True, zbuf(D_HEADS * D_DV))
            o2, s_pm = _retention(proj, tabs_ret, 0, BATCH, nc_main, CHUNK, CHUNK, s_m, gn, True, o2)
            o2, s_sm = _retention(proj, tabs_ret, ROW_S, DEC_BATCH, 1, S_ROWS, DEC_SEQ, st_s, gn, False, o2,
                                  SEQS_PER_STEP, j * DEC_BATCH // SEQS_PER_STEP)

            w1 = w_out_o[j][:C_WIDTH].astype(BF16)
            w2 = w_out_o[j][C_WIDTH:].astype(BF16)
            h_p.append(h_pm[:, 0]); cv_p.append(cv_pm); s_p.append(s_pm)
            h_s.append(h_sm[:, 0]); cv_s.append(cv_sm); s_s.append(s_sm)

        x = _outproj_ln(o1, o2, w1, w2, x, row2(ln_mix_g[l]), row2(ln_mix_b[l]))
        for dead in (o1, o2):
            spare.setdefault(dead.shape[1], []).append(dead)

        wr = jnp.concatenate([w_gr[l], jnp.transpose(w_er[l], (1, 0, 2)).reshape(D_MODEL, N_EXPERTS)], axis=1)
        br = jnp.concatenate([b_gr[l], b_er[l].reshape(N_EXPERTS)])
        x = _moe_ln(x, _pad_cols(wr, LANES), _pad_cols(row2(br), LANES), w1_all, w3_all, w2_all,
                    row2(ln_ffn_g[l]), row2(ln_ffn_b[l]), l)

    y_prompt = x[:N_MAIN].reshape(BATCH, SEQ, D_MODEL)
    y_sample = x[ROW_S:ROW_M].reshape(DEC_BATCH, S_ROWS, D_MODEL)[:, :DEC_SEQ]
    st = jnp.stack
    return (y_prompt, y_sample, st(kv_p), st(kv_s), st(c_p), st(c_s), st(n_p), st(n_s), st(m_p), st(m_s),
            st(h_p), st(h_s), st(cv_p), st(cv_s), st(s_p), st(s_s))
```

```python
import functools
import math

import jax
import jax.numpy as jnp
from jax import lax
from jax.experimental import pallas as pl
from jax.experimental.pallas import tpu as pltpu

F32 = jnp.float32
BF16 = jnp.bfloat16

D_MODEL = 1024
BATCH = 4
SEQ = 4096
DEPTH = 4
DEC_BATCH = 128
DEC_SEQ = 4
PAST_LEN = 8192
PAGE_SIZE = 128
N_PAGES = PAST_LEN // PAGE_SIZE
N_META = 16
N_EVEN = (DEPTH + 1) // 2
N_ODD = DEPTH // 2
CHUNK = 128
ROPE_THETA = 10000.0
A_HEADS = 8
A_NOPE = 64
A_ROPE = 32
A_V = 64
A_QLORA = 384
A_KVLORA = 256
A_LAT = A_KVLORA + A_ROPE
B_HEADS = 4
B_DK = 128
B_DV = 128
C_WIDTH = 512
C_BLOCKS = 8
C_BW = C_WIDTH // C_BLOCKS
C_CONV = 4
C_POW = 8.0
D_HEADS = 4
D_DK = 128
D_DV = 128
N_GROUPS = 4
E_PER_GROUP = 8
N_EXPERTS = N_GROUPS * E_PER_GROUP
D_EXPERT = 128
DN_ALPHA = (2.0 * DEPTH) ** 0.25
EPS = 1e-5

LANES = 128
SUBLANES = 8
VMEM_LIMIT = 48 * 1024 * 1024

S_ROWS = 8
N_MAIN = BATCH * SEQ
ROW_S = N_MAIN
N_SAMP = DEC_BATCH * S_ROWS
ROW_M = ROW_S + N_SAMP
TM = 512
N_TOK = -(-(ROW_M + N_META) // TM) * TM

HEAD_PAD = LANES
ATT_W = A_HEADS * HEAD_PAD
ROPE_LANE0 = A_NOPE
PROJ_E_W = 3072
GATE_COL = 640
MI_LANE = ROPE_LANE0 + A_ROPE
MF_LANE = MI_LANE + B_HEADS

TQ = 256
TK = 512
SEQS_PER_STEP = 4
SEQS_PER_STEP_RGLRU = 16


def _cparams(sem):
    return pltpu.CompilerParams(dimension_semantics=sem, vmem_limit_bytes=VMEM_LIMIT)


def _bdot(a, b):
    return jnp.dot(a.astype(BF16), b.astype(BF16), preferred_element_type=F32)


def _bdot_nt(a, b):
    return lax.dot_general(a.astype(BF16), b.astype(BF16), (((1,), (1,)), ((), ())),
                           preferred_element_type=F32)


def _bdot_tn(a, b):
    return lax.dot_general(a.astype(BF16), b.astype(BF16), (((0,), (0,)), ((), ())),
                           preferred_element_type=F32)


def _sigmoid(x):
    return 1.0 / (1.0 + jnp.exp(-x))


def _layer_norm(x, g, b):
    mu = jnp.mean(x, axis=-1, keepdims=True)
    xc = x - mu
    var = jnp.mean(xc * xc, axis=-1, keepdims=True)
    return xc * lax.rsqrt(var + EPS) * g + b


def _head_norm(x, g):
    mu = jnp.mean(x, axis=-1, keepdims=True)
    xc = x - mu
    var = jnp.mean(xc * xc, axis=-1, keepdims=True)
    return xc * lax.rsqrt(var + EPS) * g


def _mm_kernel(x_ref, w_ref, o_ref):
    o_ref[...] = _bdot(x_ref[...], w_ref[...])


def _matmul(x, w):
    n, k = x.shape
    m = w.shape[1]
    return pl.pallas_call(
        _mm_kernel,
        grid=(n // TM,),
        in_specs=[pl.BlockSpec((TM, k), lambda i: (i, 0)),
                  pl.BlockSpec((k, m), lambda i: (0, 0))],
        out_specs=pl.BlockSpec((TM, m), lambda i: (i, 0)),
        out_shape=jax.ShapeDtypeStruct((n, m), F32),
        compiler_params=_cparams(("parallel",)),
        name="proj_in",
    )(x, w)


def _rope_tile(x, c, s1, s2):
    half = A_ROPE // 2
    return x * c + pltpu.roll(x, half, 1) * s1 + pltpu.roll(x, LANES - half, 1) * s2


def _mla_prep_kernel(a_ref, c_ref, s1_ref, s2_ref, gq_ref, gkv_ref, wuq_ref, wk_ref, wv_ref,
                     q_ref, k_ref, v_ref, kv_ref, kvt_ref):
    a = a_ref[...]
    c, s1, s2 = c_ref[...], s1_ref[...], s2_ref[...]
    cq = a[:, :A_QLORA]
    ckv = a[:, A_QLORA:A_QLORA + A_KVLORA]
    cq = cq * lax.rsqrt(jnp.mean(cq * cq, axis=-1, keepdims=True) + EPS) * gq_ref[...]
    ckv = ckv * lax.rsqrt(jnp.mean(ckv * ckv, axis=-1, keepdims=True) + EPS) * gkv_ref[...]
    kr = _rope_tile(a[:, GATE_COL:GATE_COL + LANES], c, s1, s2)
    kv_ref[:, :A_KVLORA] = ckv
    kv_ref[:, A_KVLORA:] = kr
    kvt_ref[:A_KVLORA, :] = ckv.T
    kvt_ref[A_KVLORA:, :] = kr.T
    q = _bdot(cq, wuq_ref[...])
    kn = _bdot(ckv, wk_ref[...])
    lane = lax.broadcasted_iota(jnp.int32, (1, ATT_W), 1)
    ones_col = jnp.where(lane % HEAD_PAD == A_V, 1.0, 0.0)
    v_ref[...] = (_bdot(ckv, wv_ref[...]) + ones_col).astype(BF16)
    for h in range(A_HEADS):
        sl = slice(h * HEAD_PAD, (h + 1) * HEAD_PAD)
        q_ref[:, sl] = _rope_tile(q[:, sl], c, s1, s2)
        k_ref[:, sl] = (kn[:, sl] + kr).astype(BF16)


def _mla_prep(proj, tabs, gq, gkv, wuq, wk, wv):
    n = proj.shape[0]
    row = lambda w: pl.BlockSpec((TM, w), lambda i: (i, 0))
    full = lambda a: pl.BlockSpec(a.shape, lambda i: (0,) * a.ndim)
    return pl.pallas_call(
        _mla_prep_kernel,
        grid=(n // TM,),
        in_specs=[row(GATE_COL + LANES), row(LANES), row(LANES), row(LANES),
                  full(gq), full(gkv), full(wuq), full(wk), full(wv)],
        out_specs=[row(ATT_W), row(ATT_W), row(ATT_W), row(A_KVLORA + LANES),
                   pl.BlockSpec((A_KVLORA + LANES, TM), lambda i: (0, i))],
        out_shape=[jax.ShapeDtypeStruct((n, ATT_W), F32),
                   jax.ShapeDtypeStruct((n, ATT_W), BF16),
                   jax.ShapeDtypeStruct((n, ATT_W), BF16),
                   jax.ShapeDtypeStruct((n, A_KVLORA + LANES), F32),
                   jax.ShapeDtypeStruct((A_KVLORA + LANES, n), F32)],
        compiler_params=_cparams(("parallel",)),
        name="mla_prep",
    )(proj, *tabs, gq, gkv, wuq, wk, wv)


ATT_SCALE = (A_NOPE + A_ROPE) ** -0.5


QK_SCALE2 = ATT_SCALE * math.log2(math.e)


def _flash_kernel(qi_ref, ki_ref, q_ref, k_ref, v_ref, km_ref, vm_ref, buf_ref, o_ref, qs_ref, *head_scr):
    del buf_ref
    m_refs, acc_refs = head_scr[:A_HEADS], head_scr[A_HEADS:]
    qi = qi_ref[pl.program_id(1)]
    ki = ki_ref[pl.program_id(1)]
    kdiag = (qi * TQ) // TK

    @pl.when(ki == 0)
    def _init():
        qs_ref[...] = (q_ref[...] * QK_SCALE2).astype(BF16)
        for h in range(A_HEADS):
            sl = slice(h * HEAD_PAD, (h + 1) * HEAD_PAD)
            s = _bdot_nt(qs_ref[:, sl], km_ref[:, sl])
            m = jnp.max(s, axis=1, keepdims=True)
            m_refs[h][...] = jnp.broadcast_to(m, (TQ, HEAD_PAD))
            acc_refs[h][...] = _bdot(jnp.exp2(s - m), vm_ref[:, sl])

    def step(masked):
        if masked:
            row = lax.broadcasted_iota(jnp.int32, (TQ, TK), 0)
            col = lax.broadcasted_iota(jnp.int32, (TQ, TK), 1)
            visible = col <= row + (qi * TQ - ki * TK)
        def scores(h):
            sl = slice(h * HEAD_PAD, (h + 1) * HEAD_PAD)
            return _bdot_nt(qs_ref[:, sl], k_ref[:, sl])

        s_next = scores(0)
        for h in range(A_HEADS):
            sl = slice(h * HEAD_PAD, (h + 1) * HEAD_PAD)
            s = s_next
            if h + 1 < A_HEADS:
                s_next = scores(h + 1)
            if masked:
                s = jnp.where(visible, s, -jnp.inf)
            m_prev = m_refs[h][...]
            m_new = jnp.maximum(m_prev, jnp.max(s, axis=1, keepdims=True))
            alpha = jnp.exp2(m_prev - m_new)
            p = jnp.exp2(s - jnp.concatenate([m_new] * (TK // HEAD_PAD), axis=1))
            acc_refs[h][...] = alpha * acc_refs[h][...] + _bdot(p, v_ref[:, sl])
            m_refs[h][...] = m_new

    pl.when(ki < kdiag)(functools.partial(step, False))
    pl.when(ki == kdiag)(functools.partial(step, True))

    @pl.when(ki == kdiag)
    def _fin():
        for h in range(A_HEADS):
            a = acc_refs[h][...]
            o_ref[:, h * HEAD_PAD:(h + 1) * HEAD_PAD] = a / a[:, A_V:A_V + 1]


def _flash(q, k, v, obuf, nb, seq, meta_row):
    nq = seq // TQ
    nk = seq // TK
    meta_blk = meta_row // N_META
    pairs = [(i, j) for i in range(nq) for j in range((i * TQ) // TK + 1)]
    qi_tab = jnp.asarray([p[0] for p in pairs], jnp.int32)
    ki_tab = jnp.asarray([p[1] for p in pairs], jnp.int32)
    qmap = lambda b, p, qt, kt: (b * nq + qt[p], 0)
    kmap = lambda b, p, qt, kt: (b * nk + kt[p], 0)
    mmap = lambda b, p, qt, kt: (meta_blk, 0)
    grid_spec = pltpu.PrefetchScalarGridSpec(
        num_scalar_prefetch=2,
        grid=(nb, len(pairs)),
        in_specs=[pl.BlockSpec((TQ, ATT_W), qmap),
                  pl.BlockSpec((TK, ATT_W), kmap),
                  pl.BlockSpec((TK, ATT_W), kmap),
                  pl.BlockSpec((N_META, ATT_W), mmap),
                  pl.BlockSpec((N_META, ATT_W), mmap),
                  pl.BlockSpec(memory_space=pl.ANY)],
        out_specs=pl.BlockSpec((TQ, ATT_W), qmap),
        scratch_shapes=([pltpu.VMEM((TQ, ATT_W), BF16)]
                        + [pltpu.VMEM((TQ, HEAD_PAD), F32)] * A_HEADS
                        + [pltpu.VMEM((TQ, HEAD_PAD), F32)] * A_HEADS),
    )
    return pl.pallas_call(
        _flash_kernel,
        grid_spec=grid_spec,
        out_shape=jax.ShapeDtypeStruct(obuf.shape, F32),
        input_output_aliases={7: 0},
        compiler_params=_cparams(("parallel", "arbitrary")),
        name="mla_flash",
    )(qi_tab, ki_tab, q, k, v, k, v, obuf)


def _meta_attn_kernel(q_ref, k_ref, v_ref, buf_ref, o_ref):
    del buf_ref
    n = q_ref.shape[0]
    row = lax.broadcasted_iota(jnp.int32, (n, n), 0)
    col = lax.broadcasted_iota(jnp.int32, (n, n), 1)
    for h in range(A_HEADS):
        sl = slice(h * HEAD_PAD, (h + 1) * HEAD_PAD)
        s = _bdot_nt(q_ref[:, sl], k_ref[:, sl]) * ATT_SCALE
        s = jnp.where(col <= row, s, -jnp.inf)
        p = jnp.exp(s - jnp.max(s, axis=1, keepdims=True))
        o_ref[:, sl] = _bdot(p, v_ref[:, sl]) / jnp.sum(p, axis=1, keepdims=True)


def _meta_attn(q, k, v, buf, meta_row):
    blk = meta_row // N_META
    spec = pl.BlockSpec((N_META, ATT_W), lambda i: (blk, 0))
    return pl.pallas_call(
        _meta_attn_kernel,
        grid=(1,),
        in_specs=[spec, spec, spec, pl.BlockSpec(memory_space=pl.ANY)],
        out_specs=spec,
        out_shape=jax.ShapeDtypeStruct(buf.shape, F32),
        input_output_aliases={3: 0},
        compiler_params=_cparams(("arbitrary",)),
        name="mla_meta_attn",
    )(q, k, v, buf)


PAGES_PER_STEP = 8
KV_ROPE0 = A_KVLORA + ROPE_LANE0


def _sattn_kernel(pt_ref, q_ref, kvn_ref, wuk_ref, wuv_ref, cache_ref, obuf_ref, o_ref, buf_ref, sem_ref, *, layer):
    del obuf_ref
    b = pl.program_id(0)
    nb = pl.num_programs(0)

    def page_copy(seq, p, slot):
        return pltpu.make_async_copy(cache_ref.at[layer, pt_ref[seq, p]], buf_ref.at[slot, p], sem_ref.at[slot])

    def start(seq, slot):
        def body(p, carry):
            page_copy(seq, p, slot).start()
            return carry
        lax.fori_loop(0, N_PAGES, body, 0)

    def wait(seq, slot):
        def body(p, carry):
            page_copy(seq, p, slot).wait()
            return carry
        lax.fori_loop(0, N_PAGES, body, 0)

    @pl.when(b == 0)
    def _first():
        start(0, 0)

    @pl.when(b + 1 < nb)
    def _next():
        start(b + 1, (b + 1) % 2)

    slot = b % 2
    wait(b, slot)

    q = q_ref[...]
    ql, qr = [], []
    for h in range(A_HEADS):
        ql.append(_bdot(q[:, h * HEAD_PAD:h * HEAD_PAD + A_NOPE], wuk_ref[h]))
        qr.append(q[:, h * HEAD_PAD + ROPE_LANE0:h * HEAD_PAD + ROPE_LANE0 + A_ROPE])
    ql = (jnp.concatenate(ql, axis=0) * QK_SCALE2).astype(BF16)
    qr = (jnp.concatenate(qr, axis=0) * QK_SCALE2).astype(BF16)
    rows = A_HEADS * S_ROWS
    keys = PAGES_PER_STEP * PAGE_SIZE

    def scores(c):
        pages = [buf_ref[slot, c * PAGES_PER_STEP + i] for i in range(PAGES_PER_STEP)]
        ckv = jnp.concatenate([pg[:A_KVLORA].astype(BF16) for pg in pages], axis=1)
        kr = jnp.concatenate([pg[A_KVLORA:].astype(BF16) for pg in pages], axis=1)
        return ckv, _bdot(ql, ckv) + _bdot(qr, kr)

    m_prev = jnp.full((rows, 1), -jnp.inf, F32)
    l_prev = jnp.zeros((rows, 1), F32)
    acc = jnp.zeros((rows, A_KVLORA), F32)
    n_chunks = N_PAGES // PAGES_PER_STEP
    nxt = scores(0)
    for c in range(n_chunks):
        ckv, s = nxt
        if c + 1 < n_chunks:
            nxt = scores(c + 1)
        m_new = jnp.maximum(m_prev, jnp.max(s, axis=1, keepdims=True))
        alpha = jnp.exp2(m_prev - m_new)
        p = jnp.exp2(s - m_new)
        l_prev = alpha * l_prev + jnp.sum(p, axis=1, keepdims=True)
        acc = alpha * acc + _bdot_nt(p, ckv)
        m_prev = m_new

    kvn = kvn_ref[...]
    ckv_n = kvn[:, :A_KVLORA]
    kr_n = kvn[:, KV_ROPE0:KV_ROPE0 + A_ROPE]
    s = _bdot_nt(ql, ckv_n) + _bdot_nt(qr, kr_n)
    t_row = lax.broadcasted_iota(jnp.int32, (rows, S_ROWS), 0) % S_ROWS
    t_col = lax.broadcasted_iota(jnp.int32, (rows, S_ROWS), 1)
    s = jnp.where(t_col <= jnp.minimum(t_row, DEC_SEQ - 1), s, -jnp.inf)
    m_new = jnp.maximum(m_prev, jnp.max(s, axis=1, keepdims=True))
    alpha = jnp.exp2(m_prev - m_new)
    p = jnp.exp2(s - m_new)
    l_new = alpha * l_prev + jnp.sum(p, axis=1, keepdims=True)
    o_lat = (alpha * acc + _bdot(p, ckv_n)) / l_new
    for h in range(A_HEADS):
        o_ref[:, h * HEAD_PAD:(h + 1) * HEAD_PAD] = _bdot(o_lat[h * S_ROWS:(h + 1) * S_ROWS], wuv_ref[h])


def _sample_attn(page_table, q, kvrow, wuk_t, wuv_p, cache_t, obuf, layer):
    nb = page_table.shape[0]
    blk0 = ROW_S // S_ROWS
    grid_spec = pltpu.PrefetchScalarGridSpec(
        num_scalar_prefetch=1,
        grid=(nb,),
        in_specs=[pl.BlockSpec((S_ROWS, ATT_W), lambda b, pt: (blk0 + b, 0)),
                  pl.BlockSpec((S_ROWS, A_KVLORA + LANES), lambda b, pt: (blk0 + b, 0)),
                  pl.BlockSpec(wuk_t.shape, lambda b, pt: (0, 0, 0)),
                  pl.BlockSpec(wuv_p.shape, lambda b, pt: (0, 0, 0)),
                  pl.BlockSpec(memory_space=pl.ANY),
                  pl.BlockSpec(memory_space=pl.ANY)],
        out_specs=pl.BlockSpec((S_ROWS, ATT_W), lambda b, pt: (blk0 + b, 0)),
        scratch_shapes=[pltpu.VMEM((2, N_PAGES, A_LAT, PAGE_SIZE), F32),
                        pltpu.SemaphoreType.DMA((2,))],
    )
    return pl.pallas_call(
        functools.partial(_sattn_kernel, layer=layer),
        grid_spec=grid_spec,
        out_shape=jax.ShapeDtypeStruct(obuf.shape, F32),
        input_output_aliases={6: 0},
        compiler_params=_cparams(("arbitrary",)),
        name="mla_sample_attn",
    )(page_table, q, kvrow, wuk_t, wuv_p, cache_t, obuf)


def _row_of(col_vec, eye):
    return jnp.sum(jnp.where(eye, col_vec, 0.0), axis=0, keepdims=True)


def _mlstm_kernel(g_ref, q_ref, k_ref, v_ref, og_ref, c0_ref, n0_ref, m0_ref, bi_ref, bf_ref, gn_ref, buf_ref,
                  o_ref, c_ref, n_ref, m_ref, c_scr, n_scr, m_scr, *, n_valid, seqs):
    del buf_ref
    ci = pl.program_id(1)
    L = q_ref.shape[0] // seqs

    @pl.when(ci == 0)
    def _load():
        c_scr[...] = c0_ref[...]
        for s in range(seqs):
            n_scr[s, 0:B_HEADS, :] = n0_ref[s]
            for h in range(B_HEADS):
                m_scr[s, h:h + 1, :] = jnp.broadcast_to(m0_ref[s, 0:1, h:h + 1], (1, LANES))

    row = lax.broadcasted_iota(jnp.int32, (L, L), 0)
    col = lax.broadcasted_iota(jnp.int32, (L, L), 1)
    eye = row == col
    tril = col <= row
    valid = lax.broadcasted_iota(jnp.int32, (L, 1), 0) < n_valid
    lane = lax.broadcasted_iota(jnp.int32, (1, LANES), 1)
    wide = L == LANES and B_DV == LANES and B_DK == LANES
    cv = (lambda t: jnp.broadcast_to(t, (L, LANES))) if wide else (lambda t: t)
    m_outs = []
    for s in range(seqs):
        rs = slice(s * L, (s + 1) * L)
        g = g_ref[rs, :]
        qs, ks, vs, qk, qc = [], [], [], [], []
        for h in range(B_HEADS):
            sl = slice(h * B_DK, (h + 1) * B_DK)
            qs.append(q_ref[rs, sl])
            ks.append(k_ref[rs, sl] * (B_DK ** -0.5))
            vs.append(v_ref[rs, sl])
            qk.append(_bdot_nt(qs[h], ks[h]))
            qc.append(_bdot(qs[h], c_scr[s, h]))
        m_out = jnp.zeros((1, LANES), F32)
        for h in range(B_HEADS):
            sl = slice(h * B_DK, (h + 1) * B_DK)
            q, k, v = qs[h], ks[h], vs[h]
            ig_col = cv(jnp.where(valid, g[:, MI_LANE + h:MI_LANE + h + 1] + bi_ref[0:1, h:h + 1], -jnp.inf))
            fpre = g[:, MF_LANE + h:MF_LANE + h + 1] + bf_ref[0:1, h:h + 1]
            lf_col = cv(jnp.where(valid, jnp.minimum(fpre, 0.0) - jnp.log1p(jnp.exp(-jnp.abs(fpre))), 0.0))
            ig_row = _row_of(ig_col, eye)
            lf_row = _row_of(lf_col, eye)
            b_col = cv(jnp.sum(jnp.where(tril, lf_row, 0.0), axis=1, keepdims=True))
            b_row = jnp.sum(jnp.where(row <= col, lf_col, 0.0), axis=0, keepdims=True)
            m_prev = m_scr[s, h:h + 1, :] if wide else m_scr[s, h:h + 1, 0:1]
            c_prev = c_scr[s, h]
            n_prev = n_scr[s, h:h + 1, :]
            dmat = jnp.where(tril, b_col - b_row + ig_row, -jnp.inf)
            inter = b_col + m_prev
            m_t = jnp.maximum(inter, cv(jnp.max(dmat, axis=1, keepdims=True)))
            w_inter = jnp.exp(inter - m_t)
            sc = qk[h] * jnp.exp(dmat - m_t)
            num = w_inter * qc[h] + _bdot(sc, v)
            den = (w_inter * cv(jnp.sum(q * n_prev, axis=1, keepdims=True))
                   + cv(jnp.sum(sc, axis=1, keepdims=True)))
            hh = num / jnp.maximum(jnp.abs(den), jnp.exp(-m_t))
            m_new = m_t[n_valid - 1:n_valid, :]
            b_last = b_col[n_valid - 1:n_valid, :]
            w_end = jnp.exp(b_last - b_col + ig_col - m_new)
            decay = jnp.exp(b_last + m_prev - m_new)
            kw = k * w_end
            c_scr[s, h] = decay * c_prev + _bdot_tn(kw, v)
            n_scr[s, h:h + 1, :] = decay * n_prev + jnp.sum(kw, axis=0, keepdims=True)
            m_scr[s, h:h + 1, :] = jnp.broadcast_to(m_new, (1, LANES))
            m_out = jnp.where(lane == h, m_new, m_out)
            o_ref[rs, sl] = _head_norm(hh, gn_ref[0:1, sl]) * _sigmoid(og_ref[rs, sl])
        m_outs.append(m_out)

    @pl.when(ci == pl.num_programs(1) - 1)
    def _store():
        c_ref[...] = c_scr[...]
        for s in range(seqs):
            n_ref[s] = n_scr[s, 0:B_HEADS, :]
            m_ref[s] = m_outs[s]


def _stream_out(obuf, n_in, w, row0, nc, L):
    blk0 = row0 // L
    spec = pl.BlockSpec((L, w), lambda b, c: (blk0 + b * nc + c, 0))
    return spec, [pl.BlockSpec(memory_space=pl.ANY)], [obuf], {n_in: 0}


def _mlstm(proj, row0, nb, nc, L, n_valid, c0, n0, m0, bi, bf, gn, shared_state, obuf, seqs=1, sb0=0):
    assert seqs == 1 or (nc == 1 and not shared_state and nb % seqs == 0)
    R = L * seqs
    blk0 = row0 // R
    rmap = lambda cb: (lambda b, c: (blk0 + b * nc + c, cb))
    smap = (lambda b, c: (0, 0, 0, 0)) if shared_state else (lambda b, c: (sb0 + b, 0, 0, 0))
    smap3 = (lambda b, c: (0, 0, 0)) if shared_state else (lambda b, c: (sb0 + b, 0, 0))
    w = B_HEADS * B_DK
    full2 = lambda a: pl.BlockSpec(a.shape, lambda b, c: (0, 0))
    ospec, xspec, xarg, alias = _stream_out(obuf, 11, w, row0, nc, R)
    return pl.pallas_call(
        functools.partial(_mlstm_kernel, n_valid=n_valid, seqs=seqs),
        grid=(nb // seqs, nc),
        in_specs=[pl.BlockSpec((R, LANES), rmap(GATE_COL // LANES)),
                  pl.BlockSpec((R, w), rmap(2)), pl.BlockSpec((R, w), rmap(3)),
                  pl.BlockSpec((R, w), rmap(4)), pl.BlockSpec((R, w), rmap(5)),
                  pl.BlockSpec((seqs, B_HEADS, B_DK, B_DV), smap),
                  pl.BlockSpec((seqs, B_HEADS, B_DK), smap3),
                  pl.BlockSpec((seqs, 1, B_HEADS), smap3),
                  full2(bi), full2(bf), full2(gn)] + xspec,
        out_specs=[ospec,
                   pl.BlockSpec((seqs, B_HEADS, B_DK, B_DV), lambda b, c: (b, 0, 0, 0)),
                   pl.BlockSpec((seqs, B_HEADS, B_DK), lambda b, c: (b, 0, 0)),
                   pl.BlockSpec((seqs, 1, LANES), lambda b, c: (b, 0, 0))],
        out_shape=[jax.ShapeDtypeStruct((proj.shape[0], w), F32),
                   jax.ShapeDtypeStruct((nb, B_HEADS, B_DK, B_DV), F32),
                   jax.ShapeDtypeStruct((nb, B_HEADS, B_DK), F32),
                   jax.ShapeDtypeStruct((nb, 1, LANES), F32)],
        scratch_shapes=[pltpu.VMEM((seqs, B_HEADS, B_DK, B_DV), F32),
                        pltpu.VMEM((seqs, SUBLANES, B_DK), F32),
                        pltpu.VMEM((seqs, SUBLANES, LANES), F32)],
        input_output_aliases=alias,
        compiler_params=_cparams(("parallel", "arbitrary")),
        name="mlstm",
    )(proj, proj, proj, proj, proj, c0, n0, m0, bi, bf, gn, *xarg)


RET_LOG_GAMMA = tuple(math.log1p(-(2.0 ** (-5.0 - h))) for h in range(D_HEADS))


def _ret_kernel(q_ref, k_ref, v_ref, g_ref, cr_ref, sr_ref, s0_ref, gn_ref, buf_ref, o_ref, s_ref, s_scr, *,
                n_valid, seqs):
    del buf_ref
    ci = pl.program_id(1)
    L = q_ref.shape[0] // seqs

    @pl.when(ci == 0)
    def _load():
        s_scr[...] = s0_ref[...]

    row = lax.broadcasted_iota(jnp.int32, (L, L), 0)
    col = lax.broadcasted_iota(jnp.int32, (L, L), 1)
    rel = (row - col).astype(F32)
    t_col = lax.broadcasted_iota(jnp.int32, (L, D_DV if D_DV == D_DK else 1), 0)
    tf = t_col.astype(F32)
    half = D_DK // 2
    for s in range(seqs):
        rs = slice(s * L, (s + 1) * L)
        cr, sr = cr_ref[rs, :], sr_ref[rs, :]
        qs, ks, vs, qk, qs_state = [], [], [], [], []
        for h in range(D_HEADS):
            sl = slice(h * D_DK, (h + 1) * D_DK)
            q = q_ref[rs, sl]
            k = k_ref[rs, sl]
            qs.append(q * cr + pltpu.roll(q, half, 1) * sr)
            ks.append((k * cr + pltpu.roll(k, half, 1) * sr) * (D_DK ** -0.5))
            vs.append(v_ref[rs, sl])
            qk.append(_bdot_nt(qs[h], ks[h]))
            qs_state.append(_bdot(qs[h], s_scr[s, h]))
        for h in range(D_HEADS):
            lg = RET_LOG_GAMMA[h]
            sl = slice(h * D_DK, (h + 1) * D_DK)
            k, v = ks[h], vs[h]
            dec = jnp.where(rel >= 0, jnp.exp(lg * jnp.maximum(rel, 0.0)), 0.0)
            inner = qk[h] * dec
            o = _bdot(inner, v) + jnp.exp((tf + 1.0) * lg) * qs_state[h]
            w_end = jnp.where(t_col < n_valid, jnp.exp((n_valid - 1.0 - tf) * lg), 0.0)
            s_scr[s, h] = math.exp(n_valid * lg) * s_scr[s, h] + _bdot_tn(k * w_end, v)
            gate = g_ref[rs, sl]
            o_ref[rs, sl] = _head_norm(o, gn_ref[0:1, sl]) * (gate * _sigmoid(gate))

    @pl.when(ci == pl.num_programs(1) - 1)
    def _store():
        s_ref[...] = s_scr[...]


def _retention(proj, tabs, row0, nb, nc, L, n_valid, s0, gn, shared_state, obuf, seqs=1, sb0=0):
    assert seqs == 1 or (nc == 1 and not shared_state and nb % seqs == 0)
    R = L * seqs
    blk0 = row0 // R
    rmap = lambda cb: (lambda b, c: (blk0 + b * nc + c, cb))
    smap = (lambda b, c: (0, 0, 0, 0)) if shared_state else (lambda b, c: (sb0 + b, 0, 0, 0))
    w = D_HEADS * D_DK
    ospec, xspec, xarg, alias = _stream_out(obuf, 8, w, row0, nc, R)
    return pl.pallas_call(
        functools.partial(_ret_kernel, n_valid=n_valid, seqs=seqs),
        grid=(nb // seqs, nc),
        in_specs=[pl.BlockSpec((R, w), rmap(2)), pl.BlockSpec((R, w), rmap(3)),
                  pl.BlockSpec((R, w), rmap(4)), pl.BlockSpec((R, w), rmap(5)),
                  pl.BlockSpec((R, LANES), rmap(0)), pl.BlockSpec((R, LANES), rmap(0)),
                  pl.BlockSpec((seqs, D_HEADS, D_DK, D_DV), smap),
                  pl.BlockSpec(gn.shape, lambda b, c: (0, 0))] + xspec,
        out_specs=[ospec,
                   pl.BlockSpec((seqs, D_HEADS, D_DK, D_DV), lambda b, c: (b, 0, 0, 0))],
        out_shape=[jax.ShapeDtypeStruct((proj.shape[0], w), F32),
                   jax.ShapeDtypeStruct((nb, D_HEADS, D_DK, D_DV), F32)],
        scratch_shapes=[pltpu.VMEM((seqs, D_HEADS, D_DK, D_DV), F32)],
        input_output_aliases=alias,
        compiler_params=_cparams(("parallel", "arbitrary")),
        name="retention",
    )(proj, proj, proj, proj, tabs[0], tabs[1], s0, gn, *xarg)


def _rglru_kernel(x_ref, gr_ref, h0_ref, cv0_ref, cw_ref, cb_ref, wa_ref, ba_ref, wi_ref, bi_ref, lam_ref, buf_ref,
                  o_ref, h_ref, cv_ref, xp_scr, h_scr, *, n_valid, seqs):
    del buf_ref
    ci = pl.program_id(1)
    R = x_ref.shape[0]
    L = R // seqs
    tail = C_CONV - 1

    @pl.when(ci == 0)
    def _load():
        for s in range(seqs):
            xp_scr[s, SUBLANES - tail:SUBLANES, :] = cv0_ref[s]
            h_scr[s, 0:1, :] = h0_ref[s]

    us, h0s, new_tails = [], [], []
    for s in range(seqs):
        xp_scr[s, SUBLANES:SUBLANES + L, :] = x_ref[s * L:(s + 1) * L, :]
        us.append(sum(xp_scr[s, SUBLANES - tail + j:SUBLANES - tail + j + L, :] * cw_ref[j:j + 1, :]
                      for j in range(C_CONV)))
        new_tails.append(xp_scr[s, SUBLANES + n_valid - tail:SUBLANES + n_valid, :])
        xp_scr[s, SUBLANES - tail:SUBLANES, :] = new_tails[s]
        h0s.append(jnp.broadcast_to(h_scr[s, 0:1, :], (L, C_WIDTH)))
    u = cb_ref[...] + (us[0] if seqs == 1 else jnp.concatenate(us, axis=0))
    h_prev = h0s[0] if seqs == 1 else jnp.concatenate(h0s, axis=0)
    r = _sigmoid(_bdot(u, wa_ref[...]) + ba_ref[...])
    i = _sigmoid(_bdot(u, wi_ref[...]) + bi_ref[...])
    nlam = -lam_ref[...]
    softplus = jnp.maximum(nlam, 0.0) + jnp.log1p(jnp.exp(-jnp.abs(nlam)))
    log_a = -C_POW * r * softplus
    a = jnp.exp(log_a)
    b = jnp.sqrt(-jnp.tanh(log_a) * (a * a + 1.0)) * (i * u)
    t_idx = lax.broadcasted_iota(jnp.int32, (R, 1), 0) % L
    step = 1
    while step < L:
        keep = t_idx >= step
        a_sh = jnp.where(keep, pltpu.roll(a, step, 0), 1.0)
        b_sh = jnp.where(keep, pltpu.roll(b, step, 0), 0.0)
        b = a * b_sh + b
        a = a * a_sh
        step *= 2
    hseq = a * h_prev + b
    h_lasts = [hseq[s * L + n_valid - 1:s * L + n_valid, :] for s in range(seqs)]
    for s in range(seqs):
        h_scr[s, 0:1, :] = h_lasts[s]
    gr = gr_ref[...]
    gelu = 0.5 * gr * (1.0 + jnp.tanh(math.sqrt(2.0 / math.pi) * (gr + 0.044715 * (gr * gr * gr))))
    o_ref[...] = hseq * gelu

    @pl.when(ci == pl.num_programs(1) - 1)
    def _store():
        for s in range(seqs):
            h_ref[s] = h_lasts[s]
            cv_ref[s] = new_tails[s]


def _rglru(proj, row0, nb, nc, L, n_valid, h0, cv0, cw, cb, wa, ba, wi, bi, lam, shared_state, obuf, seqs=1, sb0=0):
    assert seqs == 1 or (nc == 1 and not shared_state and nb % seqs == 0)
    R = L * seqs
    blk0 = row0 // R
    rmap = lambda cb_: (lambda b, c: (blk0 + b * nc + c, cb_))
    smap = (lambda b, c: (0, 0, 0)) if shared_state else (lambda b, c: (sb0 + b, 0, 0))
    full2 = lambda a: pl.BlockSpec(a.shape, lambda b, c: (0, 0))
    w = C_WIDTH
    ospec, xspec, xarg, alias = _stream_out(obuf, 11, w, row0, nc, R)
    return pl.pallas_call(
        functools.partial(_rglru_kernel, n_valid=n_valid, seqs=seqs),
        grid=(nb // seqs, nc),
        in_specs=[pl.BlockSpec((R, w), rmap(0)), pl.BlockSpec((R, w), rmap(1)),
                  pl.BlockSpec((seqs, 1, w), smap), pl.BlockSpec((seqs, C_CONV - 1, w), smap),
                  full2(cw), full2(cb), full2(wa), full2(ba), full2(wi), full2(bi), full2(lam)] + xspec,
        out_specs=[ospec,
                   pl.BlockSpec((seqs, 1, w), lambda b, c: (b, 0, 0)),
                   pl.BlockSpec((seqs, C_CONV - 1, w), lambda b, c: (b, 0, 0))],
        out_shape=[jax.ShapeDtypeStruct((proj.shape[0], w), F32),
                   jax.ShapeDtypeStruct((nb, 1, w), F32),
                   jax.ShapeDtypeStruct((nb, C_CONV - 1, w), F32)],
        scratch_shapes=[pltpu.VMEM((seqs, SUBLANES + L, w), F32), pltpu.VMEM((seqs, SUBLANES, w), F32)],
        input_output_aliases=alias,
        compiler_params=_cparams(("parallel", "arbitrary")),
        name="rglru",
    )(proj, proj, h0, cv0, cw, cb, wa, ba, wi, bi, lam, *xarg)


def _outproj_kernel(o1_ref, o2_ref, w1_ref, w2_ref, x_ref, g_ref, b_ref, y_ref):
    mix = _bdot(o1_ref[...], w1_ref[...]) + _bdot(o2_ref[...], w2_ref[...])
    y_ref[...] = _layer_norm(DN_ALPHA * x_ref[...] + mix, g_ref[...], b_ref[...])


def _outproj_ln(o1, o2, w1, w2, x, g, b):
    n = x.shape[0]
    row = lambda a: pl.BlockSpec((TM, a.shape[1]), lambda i: (i, 0))
    full = lambda a: pl.BlockSpec(a.shape, lambda i: (0, 0))
    return pl.pallas_call(
        _outproj_kernel,
        grid=(n // TM,),
        in_specs=[row(o1), row(o2), full(w1), full(w2), row(x), full(g), full(b)],
        out_specs=row(x),
        out_shape=jax.ShapeDtypeStruct(x.shape, F32),
        compiler_params=_cparams(("parallel",)),
        name="outproj_ln",
    )(o1, o2, w1, w2, x, g, b)


ROUTER_E0 = N_GROUPS
MOE_EXPERTS_PER_CHUNK = 2


def _moe_kernel(x_ref, wr_ref, br_ref, w1_ref, w3_ref, w2_ref, g_ref, b_ref, y_ref, xb_scr, gate_scr, acc_scr):
    j = pl.program_id(1)

    @pl.when(j == 0)
    def _route():
        x = x_ref[...]
        xh = x.astype(BF16)
        xb_scr[...] = xh
        xl = (x - xh.astype(F32)).astype(BF16)
        wr = wr_ref[...]
        wh = wr.astype(BF16)
        wl = (wr - wh.astype(F32)).astype(BF16)
        dot = lambda a, b: jnp.dot(a, b, preferred_element_type=F32)
        logits = dot(xh, wh) + (dot(xh, wl) + dot(xl, wh)) + br_ref[...]
        lane = lax.broadcasted_iota(jnp.int32, logits.shape, 1)
        gl = jnp.where(lane < N_GROUPS, logits, -jnp.inf)
        gmax = jnp.max(gl, axis=1, keepdims=True)
        g_gate = 1.0 / jnp.sum(jnp.exp(gl - gmax), axis=1, keepdims=True)
        g_idx = jnp.min(jnp.where(gl == gmax, lane, LANES), axis=1, keepdims=True)
        e_lo = ROUTER_E0 + g_idx * E_PER_GROUP
        el = jnp.where(jnp.logical_and(lane >= e_lo, lane < e_lo + E_PER_GROUP), logits, -jnp.inf)
        e1 = jnp.max(el, axis=1, keepdims=True)
        i1 = jnp.min(jnp.where(el == e1, lane, LANES), axis=1, keepdims=True)
        el2 = jnp.where(lane == i1, -jnp.inf, el)
        e2 = jnp.max(el2, axis=1, keepdims=True)
        i2 = jnp.min(jnp.where(el2 == e2, lane, LANES), axis=1, keepdims=True)
        r = jnp.exp(e2 - e1)
        w1 = g_gate / (1.0 + r)
        gate = jnp.where(lane == i1, w1, 0.0) + jnp.where(lane == i2, w1 * r, 0.0)
        for gi in range(N_GROUPS):
            lo = ROUTER_E0 + gi * E_PER_GROUP
            gate_scr[gi] = gate[:, lo:lo + E_PER_GROUP]
        acc_scr[...] = jnp.zeros_like(acc_scr)

    xb = xb_scr[...]
    gate = gate_scr[j]
    cw = MOE_EXPERTS_PER_CHUNK * D_EXPERT

    def up(c):
        es = range(c * MOE_EXPERTS_PER_CHUNK, (c + 1) * MOE_EXPERTS_PER_CHUNK)
        r1 = jnp.concatenate([w1_ref[0, e] for e in es], axis=1)
        r3 = jnp.concatenate([w3_ref[0, e] for e in es], axis=1)
        return (jnp.dot(xb, r1, preferred_element_type=F32), jnp.dot(xb, r3, preferred_element_type=F32))

    parts = []
    nxt = up(0)
    for c in range(E_PER_GROUP // MOE_EXPERTS_PER_CHUNK):
        h1, h3 = nxt
        if (c + 1) * MOE_EXPERTS_PER_CHUNK < E_PER_GROUP:
            nxt = up(c + 1)
        hg = h1 * _sigmoid(h1) * h3
        for ee in range(MOE_EXPERTS_PER_CHUNK):
            e = c * MOE_EXPERTS_PER_CHUNK + ee
            parts.append((hg[:, ee * D_EXPERT:(ee + 1) * D_EXPERT] * gate[:, e:e + 1]).astype(BF16))
    acc_scr[...] += jnp.dot(jnp.concatenate(parts, axis=1), w2_ref[0].astype(BF16), preferred_element_type=F32)

    @pl.when(j == pl.num_programs(1) - 1)
    def _fin():
        y_ref[...] = _layer_norm(DN_ALPHA * x_ref[...] + acc_scr[...], g_ref[...], b_ref[...])


def _moe_ln(x, wr, br, w1, w3, w2, g, b, layer):
    n = x.shape[0]
    gw = E_PER_GROUP * D_EXPERT
    row = pl.BlockSpec((TM, D_MODEL), lambda i, j: (i, 0))
    full = lambda a: pl.BlockSpec(a.shape, lambda i, j: (0, 0))
    return pl.pallas_call(
        _moe_kernel,
        grid=(n // TM, N_GROUPS),
        in_specs=[row, full(wr), full(br),
                  pl.BlockSpec((1, E_PER_GROUP, D_MODEL, D_EXPERT), lambda i, j: (layer, j, 0, 0)),
                  pl.BlockSpec((1, E_PER_GROUP, D_MODEL, D_EXPERT), lambda i, j: (layer, j, 0, 0)),
                  pl.BlockSpec((1, gw, D_MODEL), lambda i, j: (layer * N_GROUPS + j, 0, 0)),
                  full(g), full(b)],
        out_specs=row,
        out_shape=jax.ShapeDtypeStruct(x.shape, F32),
        scratch_shapes=[pltpu.VMEM((TM, D_MODEL), BF16),
                        pltpu.VMEM((N_GROUPS, TM, E_PER_GROUP), F32),
                        pltpu.VMEM((TM, D_MODEL), F32)],
        compiler_params=_cparams(("parallel", "arbitrary")),
        name="moe_ln",
    )(x, wr, br, w1, w3, w2, g, b)


def _positions():
    pos = jnp.zeros((N_TOK,), F32)
    main = (N_META + jnp.arange(SEQ, dtype=jnp.int32)).astype(F32)
    pos = pos.at[:N_MAIN].set(jnp.tile(main, BATCH))
    samp = (PAST_LEN + jnp.minimum(jnp.arange(S_ROWS, dtype=jnp.int32), DEC_SEQ - 1)).astype(F32)
    pos = pos.at[ROW_S:ROW_M].set(jnp.tile(samp, DEC_BATCH))
    pos = pos.at[ROW_M:ROW_M + N_META].set(jnp.arange(N_META, dtype=jnp.int32).astype(F32))
    return pos


def _rope_tables_mla(pos):
    half = A_ROPE // 2
    freqs = ROPE_THETA ** (-jnp.arange(half, dtype=F32) / half)
    ang = pos[:, None] * freqs
    cos, sin = jnp.cos(ang), jnp.sin(ang)
    n = pos.shape[0]
    zeros = lambda w: jnp.zeros((n, w), F32)
    rest = LANES - ROPE_LANE0 - A_ROPE
    c = jnp.concatenate([jnp.ones((n, ROPE_LANE0), F32), cos, cos, zeros(rest)], axis=1)
    s1 = jnp.concatenate([zeros(ROPE_LANE0 + half), sin, zeros(rest)], axis=1)
    s2 = jnp.concatenate([zeros(ROPE_LANE0), -sin, zeros(half + rest)], axis=1)
    return c, s1, s2


def _rope_tables_ret(pos):
    half = D_DK // 2
    freqs = ROPE_THETA ** (-jnp.arange(half, dtype=F32) / half)
    ang = pos[:, None] * freqs
    cos, sin = jnp.cos(ang), jnp.sin(ang)
    return jnp.concatenate([cos, cos], axis=1), jnp.concatenate([-sin, sin], axis=1)


def _pad_cols(a, w):
    return jnp.pad(a, ((0, 0), (0, w - a.shape[1])))


def _even_in_weight(w):
    cuts = [0, A_QLORA, A_QLORA + A_KVLORA, A_QLORA + A_KVLORA + A_ROPE]
    c_q, c_kv, k_r = (w[:, cuts[i]:cuts[i + 1]] for i in range(3))
    m0 = cuts[3]
    mw = B_HEADS * B_DK
    big = w[:, m0:m0 + 4 * mw]
    gates = w[:, m0 + 4 * mw:]
    head = jnp.concatenate([c_q, c_kv, jnp.zeros((D_MODEL, ROPE_LANE0), w.dtype), k_r, gates], axis=1)
    return jnp.concatenate([_pad_cols(head, 1024), big], axis=1).astype(BF16)


def _head_pad_cols(a, width):
    k = a.shape[0]
    return jnp.pad(a, ((0, 0), (0, 0), (0, HEAD_PAD - width))).reshape(k, A_HEADS * HEAD_PAD)


def _block_diag(w):
    eye = jnp.eye(C_BLOCKS, dtype=w.dtype)
    return (w[:, :, None, :] * eye[:, None, :, None]).reshape(C_WIDTH, C_WIDTH)


def _assemble(main, samp, meta):
    w = main.shape[1]
    pad = jnp.zeros((N_TOK - ROW_M - N_META, w), main.dtype)
    return jnp.concatenate([main, samp, meta, pad], axis=0)


def kernel(x_prompt, x_sample, cache_mla, page_table, state_mlstm_c, state_mlstm_n, state_mlstm_m,
           state_rglru_h, state_rglru_conv, state_ret, meta_tokens, w_in_e, g_cq, g_ckv, w_uq, w_uk, w_uv,
           b_mi, b_mf, g_mlstm, w_out_e, w_in_o, conv_w, conv_b, w_ra, b_ra, w_ri, b_ri, lam, g_ret, w_out_o,
           ln_mix_g, ln_mix_b, ln_ffn_g, ln_ffn_b, w_gr, b_gr, w_er, b_er, w_e1, w_e3, w_e2):
    nc_main = SEQ // CHUNK
    xs = jnp.pad(x_sample, ((0, 0), (0, S_ROWS - DEC_SEQ), (0, 0))).reshape(N_SAMP, D_MODEL)
    x = _assemble(x_prompt.reshape(N_MAIN, D_MODEL), xs, meta_tokens)

    cache_t = jnp.swapaxes(cache_mla, 2, 3)
    w1_all, w3_all = w_e1.astype(BF16), w_e3.astype(BF16)
    w2_all = w_e2.reshape(DEPTH * N_GROUPS, E_PER_GROUP * D_EXPERT, D_MODEL)
    st_c = state_mlstm_c.reshape(N_EVEN * DEC_BATCH, B_HEADS, B_DK, B_DV)
    st_n = state_mlstm_n.reshape(N_EVEN * DEC_BATCH, B_HEADS, B_DK)
    st_m = state_mlstm_m.reshape(N_EVEN * DEC_BATCH, 1, B_HEADS)
    st_h = state_rglru_h.reshape(N_ODD * DEC_BATCH, 1, C_WIDTH)
    st_cv = state_rglru_conv.reshape(N_ODD * DEC_BATCH, C_CONV - 1, C_WIDTH)
    st_s = state_ret.reshape(N_ODD * DEC_BATCH, D_HEADS, D_DK, D_DV)

    pos = _positions()
    tabs_mla = _rope_tables_mla(pos)
    tabs_ret = _rope_tables_ret(pos)
    row2 = lambda a: a.reshape(1, -1)
    spare = {}

    def zbuf(w):
        return spare[w].pop() if spare.get(w) else jnp.zeros((N_TOK, w), F32)

    kv_p, kv_s, c_p, c_s, n_p, n_s, m_p, m_s = [], [], [], [], [], [], [], []
    h_p, h_s, cv_p, cv_s, s_p, s_s = [], [], [], [], [], []

    for l in range(DEPTH):
        j = l // 2
        if l % 2 == 0:
            proj = _matmul(x, _even_in_weight(w_in_e[j]))
            wuq = _head_pad_cols(w_uq[j].reshape(A_QLORA, A_HEADS, A_NOPE + A_ROPE), A_NOPE + A_ROPE).astype(BF16)
            wk = _head_pad_cols(w_uk[j], A_NOPE).astype(BF16)
            wv = _head_pad_cols(w_uv[j], A_V).astype(BF16)
            q, kp, vp, kvrow, kvt = _mla_prep(proj, tabs_mla, row2(g_cq[j]), row2(g_ckv[j]), wuq, wk, wv)
            wuk_t = jnp.transpose(w_uk[j], (1, 2, 0)).astype(BF16)
            wuv_p = jnp.pad(jnp.transpose(w_uv[j], (1, 0, 2)),
                            ((0, 0), (0, 0), (0, HEAD_PAD - A_V))).astype(BF16)
            o1 = _flash(q, kp, vp, zbuf(ATT_W), BATCH, SEQ, ROW_M)
            o1 = _sample_attn(page_table, q, kvrow, wuk_t, wuv_p, cache_t, o1, j)
            o1 = _meta_attn(q, kp, vp, o1, ROW_M)

            bi, bf, gn = row2(b_mi[j]), row2(b_mf[j]), row2(g_mlstm[j])
            zc = jnp.zeros((1, B_HEADS, B_DK, B_DV), F32)
            zn = jnp.zeros((1, B_HEADS, B_DK), F32)
            zm = jnp.zeros((1, 1, B_HEADS), F32)
            o2, c_m, n_m, m_m = _mlstm(proj, ROW_M, 1, 1, N_META, N_META, zc, zn, zm, bi, bf, gn, True,
                                       zbuf(B_HEADS * B_DV))
            o2, c_pm, n_pm, m_pm = _mlstm(proj, 0, BATCH, nc_main, CHUNK, CHUNK, c_m, n_m,
                                          m_m[:, :, :B_HEADS], bi, bf, gn, True, o2)
            o2, c_sm, n_sm, m_sm = _mlstm(proj, ROW_S, DEC_BATCH, 1, S_ROWS, DEC_SEQ, st_c, st_n, st_m,
                                          bi, bf, gn, False, o2, SEQS_PER_STEP, j * DEC_BATCH // SEQS_PER_STEP)

            w1 = jnp.pad(w_out_e[j][:A_HEADS * A_V].reshape(A_HEADS, A_V, D_MODEL),
                         ((0, 0), (0, HEAD_PAD - A_V), (0, 0))).reshape(ATT_W, D_MODEL).astype(BF16)
            w2 = w_out_e[j][A_HEADS * A_V:].astype(BF16)

            kvf = jnp.concatenate([kvt[:A_KVLORA], kvt[KV_ROPE0:KV_ROPE0 + A_ROPE]], axis=0)
            kv_main = kvf[:, :N_MAIN].reshape(A_LAT, BATCH, SEQ)
            kv_meta = jnp.broadcast_to(kvf[:, None, ROW_M:ROW_M + N_META], (A_LAT, BATCH, N_META))
            kv_p.append(jnp.transpose(jnp.concatenate([kv_meta, kv_main], axis=2), (1, 2, 0)))
            kv_sr = kvf[:, ROW_S:ROW_M].reshape(A_LAT, DEC_BATCH, S_ROWS)[:, :, :DEC_SEQ]
            kv_s.append(jnp.transpose(kv_sr, (1, 2, 0)))
            c_p.append(c_pm); n_p.append(n_pm); m_p.append(m_pm[:, 0, :B_HEADS])
            c_s.append(c_sm); n_s.append(n_sm); m_s.append(m_sm[:, 0, :B_HEADS])
        else:
            proj = _matmul(x, w_in_o[j].astype(BF16))
            cw, cb = conv_w[j], row2(conv_b[j])
            wa, wi = _block_diag(w_ra[j]).astype(BF16), _block_diag(w_ri[j]).astype(BF16)
            ba, bi_, lam_ = row2(b_ra[j]), row2(b_ri[j]), row2(lam[j])
            zh = jnp.zeros((1, 1, C_WIDTH), F32)
            zcv = jnp.zeros((1, C_CONV - 1, C_WIDTH), F32)
            args = (cw, cb, wa, ba, wi, bi_, lam_)
            o1, h_m, cv_m = _rglru(proj, ROW_M, 1, 1, N_META, N_META, zh, zcv, *args, True, zbuf(C_WIDTH))
            o1, h_pm, cv_pm = _rglru(proj, 0, BATCH, nc_main, CHUNK, CHUNK, h_m, cv_m, *args, True, o1)
            o1, h_sm, cv_sm = _rglru(proj, ROW_S, DEC_BATCH, 1, S_ROWS, DEC_SEQ, st_h, st_cv,
                                     *args, False, o1, SEQS_PER_STEP_RGLRU, j * DEC_BATCH // SEQS_PER_STEP_RGLRU)

            gn = row2(g_ret[j])
            zs = jnp.zeros((1, D_HEADS, D_DK, D_DV), F32)
            o2, s_m = _retention(proj, tabs_ret, ROW_M, 1, 1, N_META, N_META, zs, gn, True, zbuf(D_HEADS * D_DV))
            o2, s_pm = _retention(proj, tabs_ret, 0, BATCH, nc_main, CHUNK, CHUNK, s_m, gn, True, o2)
            o2, s_sm = _retention(proj, tabs_ret, ROW_S, DEC_BATCH, 1, S_ROWS, DEC_SEQ, st_s, gn, False, o2,
                                  SEQS_PER_STEP, j * DEC_BATCH // SEQS_PER_STEP)

            w1 = w_out_o[j][:C_WIDTH].astype(BF16)
            w2 = w_out_o[j][C_WIDTH:].astype(BF16)
            h_p.append(h_pm[:, 0]); cv_p.append(cv_pm); s_p.append(s_pm)
            h_s.append(h_sm[:, 0]); cv_s.append(cv_sm); s_s.append(s_sm)

        x = _outproj_ln(o1, o2, w1, w2, x, row2(ln_mix_g[l]), row2(ln_mix_b[l]))
        for dead in (o1, o2):
            spare.setdefault(dead.shape[1], []).append(dead)

        wr = jnp.concatenate([w_gr[l], jnp.transpose(w_er[l], (1, 0, 2)).reshape(D_MODEL, N_EXPERTS)], axis=1)
        br = jnp.concatenate([b_gr[l], b_er[l].reshape(N_EXPERTS)])
        x = _moe_ln(x, _pad_cols(wr, LANES), _pad_cols(row2(br), LANES), w1_all, w3_all, w2_all,
                    row2(ln_ffn_g[l]), row2(ln_ffn_b[l]), l)

    y_prompt = x[:N_MAIN].reshape(BATCH, SEQ, D_MODEL)
    y_sample = x[ROW_S:ROW_M].reshape(DEC_BATCH, S_ROWS, D_MODEL)[:, :DEC_SEQ]
    st = jnp.stack
    return (y_prompt, y_sample, st(kv_p), st(kv_s), st(c_p), st(c_s), st(n_p), st(n_s), st(m_p), st(m_s),
            st(h_p), st(h_s), st(cv_p), st(cv_s), st(s_p), st(s_s))
```

```python
import functools
import math

import jax
import jax.numpy as jnp
from jax import lax
from jax.experimental import pallas as pl
from jax.experimental.pallas import tpu as pltpu

F32 = jnp.float32
BF16 = jnp.bfloat16

D_MODEL = 1024
BATCH = 4
SEQ = 4096
DEPTH = 4
DEC_BATCH = 128
DEC_SEQ = 4
PAST_LEN = 8192
PAGE_SIZE = 128
N_PAGES = PAST_LEN // PAGE_SIZE
N_META = 16
N_EVEN = (DEPTH + 1) // 2
N_ODD = DEPTH // 2
CHUNK = 128
ROPE_THETA = 10000.0
A_HEADS = 8
A_NOPE = 64
A_ROPE = 32
A_V = 64
A_QLORA = 384
A_KVLORA = 256
A_LAT = A_KVLORA + A_ROPE
B_HEADS = 4
B_DK = 128
B_DV = 128
C_WIDTH = 512
C_BLOCKS = 8
C_BW = C_WIDTH // C_BLOCKS
C_CONV = 4
C_POW = 8.0
D_HEADS = 4
D_DK = 128
D_DV = 128
N_GROUPS = 4
E_PER_GROUP = 8
N_EXPERTS = N_GROUPS * E_PER_GROUP
D_EXPERT = 128
DN_ALPHA = (2.0 * DEPTH) ** 0.25
EPS = 1e-5

LANES = 128
SUBLANES = 8
VMEM_LIMIT = 48 * 1024 * 1024
MOE_VMEM_LIMIT = 56 * 1024 * 1024

S_ROWS = 8
N_MAIN = BATCH * SEQ
ROW_S = N_MAIN
N_SAMP = DEC_BATCH * S_ROWS
ROW_M = ROW_S + N_SAMP
TM = 512
N_TOK = -(-(ROW_M + N_META) // TM) * TM

HEAD_PAD = LANES
ATT_W = A_HEADS * HEAD_PAD
ROPE_LANE0 = A_NOPE
PROJ_E_W = 3072
GATE_COL = 640
MI_LANE = ROPE_LANE0 + A_ROPE
MF_LANE = MI_LANE + B_HEADS

TQ = 512
TK = 512
SEQS_PER_STEP = 4
SEQS_PER_STEP_RGLRU = 16


def _cparams(sem):
    return pltpu.CompilerParams(dimension_semantics=sem, vmem_limit_bytes=VMEM_LIMIT)


def _bdot(a, b):
    return jnp.dot(a.astype(BF16), b.astype(BF16), preferred_element_type=F32)


def _bdot_nt(a, b):
    return lax.dot_general(a.astype(BF16), b.astype(BF16), (((1,), (1,)), ((), ())),
                           preferred_element_type=F32)


def _bdot_tn(a, b):
    return lax.dot_general(a.astype(BF16), b.astype(BF16), (((0,), (0,)), ((), ())),
                           preferred_element_type=F32)


def _sigmoid(x):
    return 1.0 / (1.0 + jnp.exp(-x))


def _layer_norm(x, g, b):
    mu = jnp.mean(x, axis=-1, keepdims=True)
    xc = x - mu
    var = jnp.mean(xc * xc, axis=-1, keepdims=True)
    return xc * lax.rsqrt(var + EPS) * g + b


def _head_norm(x, g):
    mu = jnp.mean(x, axis=-1, keepdims=True)
    xc = x - mu
    var = jnp.mean(xc * xc, axis=-1, keepdims=True)
    return xc * lax.rsqrt(var + EPS) * g


def _mm_kernel(x_ref, w_ref, o_ref):
    o_ref[...] = _bdot(x_ref[...], w_ref[...])


def _matmul(x, w):
    n, k = x.shape
    m = w.shape[1]
    return pl.pallas_call(
        _mm_kernel,
        grid=(n // TM,),
        in_specs=[pl.BlockSpec((TM, k), lambda i: (i, 0)),
                  pl.BlockSpec((k, m), lambda i: (0, 0))],
        out_specs=pl.BlockSpec((TM, m), lambda i: (i, 0)),
        out_shape=jax.ShapeDtypeStruct((n, m), F32),
        compiler_params=_cparams(("parallel",)),
        name="proj_in",
    )(x, w)


def _rope_tile(x, c, s1, s2):
    half = A_ROPE // 2
    return x * c + pltpu.roll(x, half, 1) * s1 + pltpu.roll(x, LANES - half, 1) * s2


def _mla_prep_kernel(a_ref, c_ref, s1_ref, s2_ref, gq_ref, gkv_ref, wuq_ref, wk_ref, wv_ref,
                     q_ref, k_ref, v_ref, kv_ref, kvt_ref):
    a = a_ref[...]
    c, s1, s2 = c_ref[...], s1_ref[...], s2_ref[...]
    cq = a[:, :A_QLORA]
    ckv = a[:, A_QLORA:A_QLORA + A_KVLORA]
    cq = cq * lax.rsqrt(jnp.mean(cq * cq, axis=-1, keepdims=True) + EPS) * gq_ref[...]
    ckv = ckv * lax.rsqrt(jnp.mean(ckv * ckv, axis=-1, keepdims=True) + EPS) * gkv_ref[...]
    kr = _rope_tile(a[:, GATE_COL:GATE_COL + LANES], c, s1, s2)
    kv_ref[:, :A_KVLORA] = ckv
    kv_ref[:, A_KVLORA:] = kr
    kvt_ref[:A_KVLORA, :] = ckv.T
    kvt_ref[A_KVLORA:, :] = kr.T
    q = _bdot(cq, wuq_ref[...])
    kn = _bdot(ckv, wk_ref[...])
    lane = lax.broadcasted_iota(jnp.int32, (1, ATT_W), 1)
    ones_col = jnp.where(lane % HEAD_PAD == A_V, 1.0, 0.0)
    v_ref[...] = (_bdot(ckv, wv_ref[...]) + ones_col).astype(BF16)
    for h in range(A_HEADS):
        sl = slice(h * HEAD_PAD, (h + 1) * HEAD_PAD)
        q_ref[:, sl] = _rope_tile(q[:, sl], c, s1, s2)
        k_ref[:, sl] = (kn[:, sl] + kr).astype(BF16)


def _mla_prep(proj, tabs, gq, gkv, wuq, wk, wv):
    n = proj.shape[0]
    row = lambda w: pl.BlockSpec((TM, w), lambda i: (i, 0))
    full = lambda a: pl.BlockSpec(a.shape, lambda i: (0,) * a.ndim)
    return pl.pallas_call(
        _mla_prep_kernel,
        grid=(n // TM,),
        in_specs=[row(GATE_COL + LANES), row(LANES), row(LANES), row(LANES),
                  full(gq), full(gkv), full(wuq), full(wk), full(wv)],
        out_specs=[row(ATT_W), row(ATT_W), row(ATT_W), row(A_KVLORA + LANES),
                   pl.BlockSpec((A_KVLORA + LANES, TM), lambda i: (0, i))],
        out_shape=[jax.ShapeDtypeStruct((n, ATT_W), F32),
                   jax.ShapeDtypeStruct((n, ATT_W), BF16),
                   jax.ShapeDtypeStruct((n, ATT_W), BF16),
                   jax.ShapeDtypeStruct((n, A_KVLORA + LANES), F32),
                   jax.ShapeDtypeStruct((A_KVLORA + LANES, n), F32)],
        compiler_params=_cparams(("parallel",)),
        name="mla_prep",
    )(proj, *tabs, gq, gkv, wuq, wk, wv)


ATT_SCALE = (A_NOPE + A_ROPE) ** -0.5


QK_SCALE2 = ATT_SCALE * math.log2(math.e)


def _flash_kernel(qi_ref, ki_ref, q_ref, k_ref, v_ref, km_ref, vm_ref, buf_ref, o_ref, qs_ref, *head_scr):
    del buf_ref
    m_refs, acc_refs = head_scr[:A_HEADS], head_scr[A_HEADS:]
    qi = qi_ref[pl.program_id(1)]
    ki = ki_ref[pl.program_id(1)]
    kdiag = (qi * TQ) // TK

    @pl.when(ki == 0)
    def _init():
        qs_ref[...] = (q_ref[...] * QK_SCALE2).astype(BF16)
        for h in range(A_HEADS):
            sl = slice(h * HEAD_PAD, (h + 1) * HEAD_PAD)
            s = _bdot_nt(qs_ref[:, sl], km_ref[:, sl])
            m = jnp.max(s, axis=1, keepdims=True)
            m_refs[h][...] = jnp.broadcast_to(m, (TQ, HEAD_PAD))
            acc_refs[h][...] = _bdot(jnp.exp2(s - m), vm_ref[:, sl])

    def step(masked):
        if masked:
            row = lax.broadcasted_iota(jnp.int32, (TQ, TK), 0)
            col = lax.broadcasted_iota(jnp.int32, (TQ, TK), 1)
            visible = col <= row + (qi * TQ - ki * TK)
        def scores(h):
            sl = slice(h * HEAD_PAD, (h + 1) * HEAD_PAD)
            return _bdot_nt(qs_ref[:, sl], k_ref[:, sl])

        s_next = scores(0)
        for h in range(A_HEADS):
            sl = slice(h * HEAD_PAD, (h + 1) * HEAD_PAD)
            s = s_next
            if h + 1 < A_HEADS:
                s_next = scores(h + 1)
            if masked:
                s = jnp.where(visible, s, -jnp.inf)
            m_prev = m_refs[h][...]
            m_new = jnp.maximum(m_prev, jnp.max(s, axis=1, keepdims=True))
            alpha = jnp.exp2(m_prev - m_new)
            p = jnp.exp2(s - jnp.concatenate([m_new] * (TK // HEAD_PAD), axis=1))
            acc_refs[h][...] = alpha * acc_refs[h][...] + _bdot(p, v_ref[:, sl])
            m_refs[h][...] = m_new

    pl.when(ki < kdiag)(functools.partial(step, False))
    pl.when(ki == kdiag)(functools.partial(step, True))

    @pl.when(ki == kdiag)
    def _fin():
        for h in range(A_HEADS):
            a = acc_refs[h][...]
            o_ref[:, h * HEAD_PAD:(h + 1) * HEAD_PAD] = a / a[:, A_V:A_V + 1]


def _flash(q, k, v, obuf, nb, seq, meta_row):
    nq = seq // TQ
    nk = seq // TK
    meta_blk = meta_row // N_META
    pairs = [(i, j) for i in range(nq) for j in range((i * TQ) // TK + 1)]
    qi_tab = jnp.asarray([p[0] for p in pairs], jnp.int32)
    ki_tab = jnp.asarray([p[1] for p in pairs], jnp.int32)
    qmap = lambda b, p, qt, kt: (b * nq + qt[p], 0)
    kmap = lambda b, p, qt, kt: (b * nk + kt[p], 0)
    mmap = lambda b, p, qt, kt: (meta_blk, 0)
    grid_spec = pltpu.PrefetchScalarGridSpec(
        num_scalar_prefetch=2,
        grid=(nb, len(pairs)),
        in_specs=[pl.BlockSpec((TQ, ATT_W), qmap),
                  pl.BlockSpec((TK, ATT_W), kmap),
                  pl.BlockSpec((TK, ATT_W), kmap),
                  pl.BlockSpec((N_META, ATT_W), mmap),
                  pl.BlockSpec((N_META, ATT_W), mmap),
                  pl.BlockSpec(memory_space=pl.ANY)],
        out_specs=pl.BlockSpec((TQ, ATT_W), qmap),
        scratch_shapes=([pltpu.VMEM((TQ, ATT_W), BF16)]
                        + [pltpu.VMEM((TQ, HEAD_PAD), F32)] * A_HEADS
                        + [pltpu.VMEM((TQ, HEAD_PAD), F32)] * A_HEADS),
    )
    return pl.pallas_call(
        _flash_kernel,
        grid_spec=grid_spec,
        out_shape=jax.ShapeDtypeStruct(obuf.shape, F32),
        input_output_aliases={7: 0},
        compiler_params=_cparams(("parallel", "arbitrary")),
        name="mla_flash",
    )(qi_tab, ki_tab, q, k, v, k, v, obuf)


def _meta_attn_kernel(q_ref, k_ref, v_ref, buf_ref, o_ref):
    del buf_ref
    n = q_ref.shape[0]
    row = lax.broadcasted_iota(jnp.int32, (n, n), 0)
    col = lax.broadcasted_iota(jnp.int32, (n, n), 1)
    for h in range(A_HEADS):
        sl = slice(h * HEAD_PAD, (h + 1) * HEAD_PAD)
        s = _bdot_nt(q_ref[:, sl], k_ref[:, sl]) * ATT_SCALE
        s = jnp.where(col <= row, s, -jnp.inf)
        p = jnp.exp(s - jnp.max(s, axis=1, keepdims=True))
        o_ref[:, sl] = _bdot(p, v_ref[:, sl]) / jnp.sum(p, axis=1, keepdims=True)


def _meta_attn(q, k, v, buf, meta_row):
    blk = meta_row // N_META
    spec = pl.BlockSpec((N_META, ATT_W), lambda i: (blk, 0))
    return pl.pallas_call(
        _meta_attn_kernel,
        grid=(1,),
        in_specs=[spec, spec, spec, pl.BlockSpec(memory_space=pl.ANY)],
        out_specs=spec,
        out_shape=jax.ShapeDtypeStruct(buf.shape, F32),
        input_output_aliases={3: 0},
        compiler_params=_cparams(("arbitrary",)),
        name="mla_meta_attn",
    )(q, k, v, buf)


PAGES_PER_STEP = 8
KV_ROPE0 = A_KVLORA + ROPE_LANE0


def _sattn_kernel(pt_ref, q_ref, kvn_ref, wuk_ref, wuv_ref, cache_ref, obuf_ref, o_ref, buf_ref, sem_ref, *, layer):
    del obuf_ref
    b = pl.program_id(0)
    nb = pl.num_programs(0)

    def page_copy(seq, p, slot):
        return pltpu.make_async_copy(cache_ref.at[layer, pt_ref[seq, p]], buf_ref.at[slot, p], sem_ref.at[slot])

    def start(seq, slot):
        def body(p, carry):
            page_copy(seq, p, slot).start()
            return carry
        lax.fori_loop(0, N_PAGES, body, 0)

    def wait(seq, slot):
        def body(p, carry):
            page_copy(seq, p, slot).wait()
            return carry
        lax.fori_loop(0, N_PAGES, body, 0)

    @pl.when(b == 0)
    def _first():
        start(0, 0)

    @pl.when(b + 1 < nb)
    def _next():
        start(b + 1, (b + 1) % 2)

    slot = b % 2
    wait(b, slot)

    q = q_ref[...]
    ql, qr = [], []
    for h in range(A_HEADS):
        ql.append(_bdot(q[:, h * HEAD_PAD:h * HEAD_PAD + A_NOPE], wuk_ref[h]))
        qr.append(q[:, h * HEAD_PAD + ROPE_LANE0:h * HEAD_PAD + ROPE_LANE0 + A_ROPE])
    ql = (jnp.concatenate(ql, axis=0) * QK_SCALE2).astype(BF16)
    qr = (jnp.concatenate(qr, axis=0) * QK_SCALE2).astype(BF16)
    rows = A_HEADS * S_ROWS
    keys = PAGES_PER_STEP * PAGE_SIZE

    def scores(c):
        pages = [buf_ref[slot, c * PAGES_PER_STEP + i] for i in range(PAGES_PER_STEP)]
        ckv = jnp.concatenate([pg[:A_KVLORA].astype(BF16) for pg in pages], axis=1)
        kr = jnp.concatenate([pg[A_KVLORA:].astype(BF16) for pg in pages], axis=1)
        return ckv, _bdot(ql, ckv) + _bdot(qr, kr)

    m_prev = jnp.full((rows, 1), -jnp.inf, F32)
    l_prev = jnp.zeros((rows, 1), F32)
    acc = jnp.zeros((rows, A_KVLORA), F32)
    n_chunks = N_PAGES // PAGES_PER_STEP
    nxt = scores(0)
    for c in range(n_chunks):
        ckv, s = nxt
        if c + 1 < n_chunks:
            nxt = scores(c + 1)
        m_new = jnp.maximum(m_prev, jnp.max(s, axis=1, keepdims=True))
        alpha = jnp.exp2(m_prev - m_new)
        p = jnp.exp2(s - m_new)
        l_prev = alpha * l_prev + jnp.sum(p, axis=1, keepdims=True)
        acc = alpha * acc + _bdot_nt(p, ckv)
        m_prev = m_new

    kvn = kvn_ref[...]
    ckv_n = kvn[:, :A_KVLORA]
    kr_n = kvn[:, KV_ROPE0:KV_ROPE0 + A_ROPE]
    s = _bdot_nt(ql, ckv_n) + _bdot_nt(qr, kr_n)
    t_row = lax.broadcasted_iota(jnp.int32, (rows, S_ROWS), 0) % S_ROWS
    t_col = lax.broadcasted_iota(jnp.int32, (rows, S_ROWS), 1)
    s = jnp.where(t_col <= jnp.minimum(t_row, DEC_SEQ - 1), s, -jnp.inf)
    m_new = jnp.maximum(m_prev, jnp.max(s, axis=1, keepdims=True))
    alpha = jnp.exp2(m_prev - m_new)
    p = jnp.exp2(s - m_new)
    l_new = alpha * l_prev + jnp.sum(p, axis=1, keepdims=True)
    o_lat = (alpha * acc + _bdot(p, ckv_n)) / l_new
    for h in range(A_HEADS):
        o_ref[:, h * HEAD_PAD:(h + 1) * HEAD_PAD] = _bdot(o_lat[h * S_ROWS:(h + 1) * S_ROWS], wuv_ref[h])


def _sample_attn(page_table, q, kvrow, wuk_t, wuv_p, cache_t, obuf, layer):
    nb = page_table.shape[0]
    blk0 = ROW_S // S_ROWS
    grid_spec = pltpu.PrefetchScalarGridSpec(
        num_scalar_prefetch=1,
        grid=(nb,),
        in_specs=[pl.BlockSpec((S_ROWS, ATT_W), lambda b, pt: (blk0 + b, 0)),
                  pl.BlockSpec((S_ROWS, A_KVLORA + LANES), lambda b, pt: (blk0 + b, 0)),
                  pl.BlockSpec(wuk_t.shape, lambda b, pt: (0, 0, 0)),
                  pl.BlockSpec(wuv_p.shape, lambda b, pt: (0, 0, 0)),
                  pl.BlockSpec(memory_space=pl.ANY),
                  pl.BlockSpec(memory_space=pl.ANY)],
        out_specs=pl.BlockSpec((S_ROWS, ATT_W), lambda b, pt: (blk0 + b, 0)),
        scratch_shapes=[pltpu.VMEM((2, N_PAGES, A_LAT, PAGE_SIZE), F32),
                        pltpu.SemaphoreType.DMA((2,))],
    )
    return pl.pallas_call(
        functools.partial(_sattn_kernel, layer=layer),
        grid_spec=grid_spec,
        out_shape=jax.ShapeDtypeStruct(obuf.shape, F32),
        input_output_aliases={6: 0},
        compiler_params=_cparams(("arbitrary",)),
        name="mla_sample_attn",
    )(page_table, q, kvrow, wuk_t, wuv_p, cache_t, obuf)


def _row_of(col_vec, eye):
    return jnp.sum(jnp.where(eye, col_vec, 0.0), axis=0, keepdims=True)


def _mlstm_kernel(g_ref, q_ref, k_ref, v_ref, og_ref, c0_ref, n0_ref, m0_ref, bi_ref, bf_ref, gn_ref, buf_ref,
                  o_ref, c_ref, n_ref, m_ref, c_scr, n_scr, m_scr, *, n_valid, seqs):
    del buf_ref
    ci = pl.program_id(1)
    L = q_ref.shape[0] // seqs

    @pl.when(ci == 0)
    def _load():
        c_scr[...] = c0_ref[...]
        for s in range(seqs):
            n_scr[s, 0:B_HEADS, :] = n0_ref[s]
            for h in range(B_HEADS):
                m_scr[s, h:h + 1, :] = jnp.broadcast_to(m0_ref[s, 0:1, h:h + 1], (1, LANES))

    row = lax.broadcasted_iota(jnp.int32, (L, L), 0)
    col = lax.broadcasted_iota(jnp.int32, (L, L), 1)
    eye = row == col
    tril = col <= row
    valid = lax.broadcasted_iota(jnp.int32, (L, 1), 0) < n_valid
    lane = lax.broadcasted_iota(jnp.int32, (1, LANES), 1)
    wide = L == LANES and B_DV == LANES and B_DK == LANES
    cv = (lambda t: jnp.broadcast_to(t, (L, LANES))) if wide else (lambda t: t)
    m_outs = []
    for s in range(seqs):
        rs = slice(s * L, (s + 1) * L)
        g = g_ref[rs, :]
        qs, ks, vs, qk, qc = [], [], [], [], []
        for h in range(B_HEADS):
            sl = slice(h * B_DK, (h + 1) * B_DK)
            qs.append(q_ref[rs, sl])
            ks.append(k_ref[rs, sl] * (B_DK ** -0.5))
            vs.append(v_ref[rs, sl])
            qk.append(_bdot_nt(qs[h], ks[h]))
            qc.append(_bdot(qs[h], c_scr[s, h]))
        m_out = jnp.zeros((1, LANES), F32)
        for h in range(B_HEADS):
            sl = slice(h * B_DK, (h + 1) * B_DK)
            q, k, v = qs[h], ks[h], vs[h]
            ig_col = cv(jnp.where(valid, g[:, MI_LANE + h:MI_LANE + h + 1] + bi_ref[0:1, h:h + 1], -jnp.inf))
            fpre = g[:, MF_LANE + h:MF_LANE + h + 1] + bf_ref[0:1, h:h + 1]
            lf_col = cv(jnp.where(valid, jnp.minimum(fpre, 0.0) - jnp.log1p(jnp.exp(-jnp.abs(fpre))), 0.0))
            ig_row = _row_of(ig_col, eye)
            lf_row = _row_of(lf_col, eye)
            b_col = cv(jnp.sum(jnp.where(tril, lf_row, 0.0), axis=1, keepdims=True))
            b_row = jnp.sum(jnp.where(row <= col, lf_col, 0.0), axis=0, keepdims=True)
            m_prev = m_scr[s, h:h + 1, :] if wide else m_scr[s, h:h + 1, 0:1]
            c_prev = c_scr[s, h]
            n_prev = n_scr[s, h:h + 1, :]
            dmat = jnp.where(tril, b_col - b_row + ig_row, -jnp.inf)
            inter = b_col + m_prev
            m_t = jnp.maximum(inter, cv(jnp.max(dmat, axis=1, keepdims=True)))
            w_inter = jnp.exp(inter - m_t)
            sc = qk[h] * jnp.exp(dmat - m_t)
            num = w_inter * qc[h] + _bdot(sc, v)
            den = (w_inter * cv(jnp.sum(q * n_prev, axis=1, keepdims=True))
                   + cv(jnp.sum(sc, axis=1, keepdims=True)))
            hh = num / jnp.maximum(jnp.abs(den), jnp.exp(-m_t))
            m_new = m_t[n_valid - 1:n_valid, :]
            b_last = b_col[n_valid - 1:n_valid, :]
            w_end = jnp.exp(b_last - b_col + ig_col - m_new)
            decay = jnp.exp(b_last + m_prev - m_new)
            kw = k * w_end
            c_scr[s, h] = decay * c_prev + _bdot_tn(kw, v)
            n_scr[s, h:h + 1, :] = decay * n_prev + jnp.sum(kw, axis=0, keepdims=True)
            m_scr[s, h:h + 1, :] = jnp.broadcast_to(m_new, (1, LANES))
            m_out = jnp.where(lane == h, m_new, m_out)
            o_ref[rs, sl] = _head_norm(hh, gn_ref[0:1, sl]) * _sigmoid(og_ref[rs, sl])
        m_outs.append(m_out)

    @pl.when(ci == pl.num_programs(1) - 1)
    def _store():
        c_ref[...] = c_scr[...]
        for s in range(seqs):
            n_ref[s] = n_scr[s, 0:B_HEADS, :]
            m_ref[s] = m_outs[s]


def _stream_out(obuf, n_in, w, row0, nc, L):
    blk0 = row0 // L
    spec = pl.BlockSpec((L, w), lambda b, c: (blk0 + b * nc + c, 0))
    return spec, [pl.BlockSpec(memory_space=pl.ANY)], [obuf], {n_in: 0}


def _mlstm(proj, row0, nb, nc, L, n_valid, c0, n0, m0, bi, bf, gn, shared_state, obuf, seqs=1, sb0=0):
    assert seqs == 1 or (nc == 1 and not shared_state and nb % seqs == 0)
    R = L * seqs
    blk0 = row0 // R
    rmap = lambda cb: (lambda b, c: (blk0 + b * nc + c, cb))
    smap = (lambda b, c: (0, 0, 0, 0)) if shared_state else (lambda b, c: (sb0 + b, 0, 0, 0))
    smap3 = (lambda b, c: (0, 0, 0)) if shared_state else (lambda b, c: (sb0 + b, 0, 0))
    w = B_HEADS * B_DK
    full2 = lambda a: pl.BlockSpec(a.shape, lambda b, c: (0, 0))
    ospec, xspec, xarg, alias = _stream_out(obuf, 11, w, row0, nc, R)
    return pl.pallas_call(
        functools.partial(_mlstm_kernel, n_valid=n_valid, seqs=seqs),
        grid=(nb // seqs, nc),
        in_specs=[pl.BlockSpec((R, LANES), rmap(GATE_COL // LANES)),
                  pl.BlockSpec((R, w), rmap(2)), pl.BlockSpec((R, w), rmap(3)),
                  pl.BlockSpec((R, w), rmap(4)), pl.BlockSpec((R, w), rmap(5)),
                  pl.BlockSpec((seqs, B_HEADS, B_DK, B_DV), smap),
                  pl.BlockSpec((seqs, B_HEADS, B_DK), smap3),
                  pl.BlockSpec((seqs, 1, B_HEADS), smap3),
                  full2(bi), full2(bf), full2(gn)] + xspec,
        out_specs=[ospec,
                   pl.BlockSpec((seqs, B_HEADS, B_DK, B_DV), lambda b, c: (b, 0, 0, 0)),
                   pl.BlockSpec((seqs, B_HEADS, B_DK), lambda b, c: (b, 0, 0)),
                   pl.BlockSpec((seqs, 1, LANES), lambda b, c: (b, 0, 0))],
        out_shape=[jax.ShapeDtypeStruct((proj.shape[0], w), F32),
                   jax.ShapeDtypeStruct((nb, B_HEADS, B_DK, B_DV), F32),
                   jax.ShapeDtypeStruct((nb, B_HEADS, B_DK), F32),
                   jax.ShapeDtypeStruct((nb, 1, LANES), F32)],
        scratch_shapes=[pltpu.VMEM((seqs, B_HEADS, B_DK, B_DV), F32),
                        pltpu.VMEM((seqs, SUBLANES, B_DK), F32),
                        pltpu.VMEM((seqs, SUBLANES, LANES), F32)],
        input_output_aliases=alias,
        compiler_params=_cparams(("parallel", "arbitrary")),
        name="mlstm",
    )(proj, proj, proj, proj, proj, c0, n0, m0, bi, bf, gn, *xarg)


RET_LOG_GAMMA = tuple(math.log1p(-(2.0 ** (-5.0 - h))) for h in range(D_HEADS))


def _ret_kernel(q_ref, k_ref, v_ref, g_ref, cr_ref, sr_ref, s0_ref, gn_ref, buf_ref, o_ref, s_ref, s_scr, *,
                n_valid, seqs):
    del buf_ref
    ci = pl.program_id(1)
    L = q_ref.shape[0] // seqs

    @pl.when(ci == 0)
    def _load():
        s_scr[...] = s0_ref[...]

    row = lax.broadcasted_iota(jnp.int32, (L, L), 0)
    col = lax.broadcasted_iota(jnp.int32, (L, L), 1)
    rel = (row - col).astype(F32)
    t_col = lax.broadcasted_iota(jnp.int32, (L, D_DV if D_DV == D_DK else 1), 0)
    tf = t_col.astype(F32)
    half = D_DK // 2
    for s in range(seqs):
        rs = slice(s * L, (s + 1) * L)
        cr, sr = cr_ref[rs, :], sr_ref[rs, :]
        qs, ks, vs, qk, qs_state = [], [], [], [], []
        for h in range(D_HEADS):
            sl = slice(h * D_DK, (h + 1) * D_DK)
            q = q_ref[rs, sl]
            k = k_ref[rs, sl]
            qs.append(q * cr + pltpu.roll(q, half, 1) * sr)
            ks.append((k * cr + pltpu.roll(k, half, 1) * sr) * (D_DK ** -0.5))
            vs.append(v_ref[rs, sl])
            qk.append(_bdot_nt(qs[h], ks[h]))
            qs_state.append(_bdot(qs[h], s_scr[s, h]))
        for h in range(D_HEADS):
            lg = RET_LOG_GAMMA[h]
            sl = slice(h * D_DK, (h + 1) * D_DK)
            k, v = ks[h], vs[h]
            dec = jnp.where(rel >= 0, jnp.exp(lg * jnp.maximum(rel, 0.0)), 0.0)
            inner = qk[h] * dec
            o = _bdot(inner, v) + jnp.exp((tf + 1.0) * lg) * qs_state[h]
            w_end = jnp.where(t_col < n_valid, jnp.exp((n_valid - 1.0 - tf) * lg), 0.0)
            s_scr[s, h] = math.exp(n_valid * lg) * s_scr[s, h] + _bdot_tn(k * w_end, v)
            gate = g_ref[rs, sl]
            o_ref[rs, sl] = _head_norm(o, gn_ref[0:1, sl]) * (gate * _sigmoid(gate))

    @pl.when(ci == pl.num_programs(1) - 1)
    def _store():
        s_ref[...] = s_scr[...]


def _retention(proj, tabs, row0, nb, nc, L, n_valid, s0, gn, shared_state, obuf, seqs=1, sb0=0):
    assert seqs == 1 or (nc == 1 and not shared_state and nb % seqs == 0)
    R = L * seqs
    blk0 = row0 // R
    rmap = lambda cb: (lambda b, c: (blk0 + b * nc + c, cb))
    smap = (lambda b, c: (0, 0, 0, 0)) if shared_state else (lambda b, c: (sb0 + b, 0, 0, 0))
    w = D_HEADS * D_DK
    ospec, xspec, xarg, alias = _stream_out(obuf, 8, w, row0, nc, R)
    return pl.pallas_call(
        functools.partial(_ret_kernel, n_valid=n_valid, seqs=seqs),
        grid=(nb // seqs, nc),
        in_specs=[pl.BlockSpec((R, w), rmap(2)), pl.BlockSpec((R, w), rmap(3)),
                  pl.BlockSpec((R, w), rmap(4)), pl.BlockSpec((R, w), rmap(5)),
                  pl.BlockSpec((R, LANES), rmap(0)), pl.BlockSpec((R, LANES), rmap(0)),
                  pl.BlockSpec((seqs, D_HEADS, D_DK, D_DV), smap),
                  pl.BlockSpec(gn.shape, lambda b, c: (0, 0))] + xspec,
        out_specs=[ospec,
                   pl.BlockSpec((seqs, D_HEADS, D_DK, D_DV), lambda b, c: (b, 0, 0, 0))],
        out_shape=[jax.ShapeDtypeStruct((proj.shape[0], w), F32),
                   jax.ShapeDtypeStruct((nb, D_HEADS, D_DK, D_DV), F32)],
        scratch_shapes=[pltpu.VMEM((seqs, D_HEADS, D_DK, D_DV), F32)],
        input_output_aliases=alias,
        compiler_params=_cparams(("parallel", "arbitrary")),
        name="retention",
    )(proj, proj, proj, proj, tabs[0], tabs[1], s0, gn, *xarg)


def _rglru_kernel(x_ref, gr_ref, h0_ref, cv0_ref, cw_ref, cb_ref, wa_ref, ba_ref, wi_ref, bi_ref, lam_ref, buf_ref,
                  o_ref, h_ref, cv_ref, xp_scr, h_scr, *, n_valid, seqs):
    del buf_ref
    ci = pl.program_id(1)
    R = x_ref.shape[0]
    L = R // seqs
    tail = C_CONV - 1

    @pl.when(ci == 0)
    def _load():
        for s in range(seqs):
            xp_scr[s, SUBLANES - tail:SUBLANES, :] = cv0_ref[s]
            h_scr[s, 0:1, :] = h0_ref[s]

    us, h0s, new_tails = [], [], []
    for s in range(seqs):
        xp_scr[s, SUBLANES:SUBLANES + L, :] = x_ref[s * L:(s + 1) * L, :]
        us.append(sum(xp_scr[s, SUBLANES - tail + j:SUBLANES - tail + j + L, :] * cw_ref[j:j + 1, :]
                      for j in range(C_CONV)))
        new_tails.append(xp_scr[s, SUBLANES + n_valid - tail:SUBLANES + n_valid, :])
        xp_scr[s, SUBLANES - tail:SUBLANES, :] = new_tails[s]
        h0s.append(jnp.broadcast_to(h_scr[s, 0:1, :], (L, C_WIDTH)))
    u = cb_ref[...] + (us[0] if seqs == 1 else jnp.concatenate(us, axis=0))
    h_prev = h0s[0] if seqs == 1 else jnp.concatenate(h0s, axis=0)
    r = _sigmoid(_bdot(u, wa_ref[...]) + ba_ref[...])
    i = _sigmoid(_bdot(u, wi_ref[...]) + bi_ref[...])
    nlam = -lam_ref[...]
    softplus = jnp.maximum(nlam, 0.0) + jnp.log1p(jnp.exp(-jnp.abs(nlam)))
    log_a = -C_POW * r * softplus
    a = jnp.exp(log_a)
    b = jnp.sqrt(-jnp.tanh(log_a) * (a * a + 1.0)) * (i * u)
    t_idx = lax.broadcasted_iota(jnp.int32, (R, 1), 0) % L
    step = 1
    while step < L:
        keep = t_idx >= step
        a_sh = jnp.where(keep, pltpu.roll(a, step, 0), 1.0)
        b_sh = jnp.where(keep, pltpu.roll(b, step, 0), 0.0)
        b = a * b_sh + b
        a = a * a_sh
        step *= 2
    hseq = a * h_prev + b
    h_lasts = [hseq[s * L + n_valid - 1:s * L + n_valid, :] for s in range(seqs)]
    for s in range(seqs):
        h_scr[s, 0:1, :] = h_lasts[s]
    gr = gr_ref[...]
    gelu = 0.5 * gr * (1.0 + jnp.tanh(math.sqrt(2.0 / math.pi) * (gr + 0.044715 * (gr * gr * gr))))
    o_ref[...] = hseq * gelu

    @pl.when(ci == pl.num_programs(1) - 1)
    def _store():
        for s in range(seqs):
            h_ref[s] = h_lasts[s]
            cv_ref[s] = new_tails[s]


def _rglru(proj, row0, nb, nc, L, n_valid, h0, cv0, cw, cb, wa, ba, wi, bi, lam, shared_state, obuf, seqs=1, sb0=0):
    assert seqs == 1 or (nc == 1 and not shared_state and nb % seqs == 0)
    R = L * seqs
    blk0 = row0 // R
    rmap = lambda cb_: (lambda b, c: (blk0 + b * nc + c, cb_))
    smap = (lambda b, c: (0, 0, 0)) if shared_state else (lambda b, c: (sb0 + b, 0, 0))
    full2 = lambda a: pl.BlockSpec(a.shape, lambda b, c: (0, 0))
    w = C_WIDTH
    ospec, xspec, xarg, alias = _stream_out(obuf, 11, w, row0, nc, R)
    return pl.pallas_call(
        functools.partial(_rglru_kernel, n_valid=n_valid, seqs=seqs),
        grid=(nb // seqs, nc),
        in_specs=[pl.BlockSpec((R, w), rmap(0)), pl.BlockSpec((R, w), rmap(1)),
                  pl.BlockSpec((seqs, 1, w), smap), pl.BlockSpec((seqs, C_CONV - 1, w), smap),
                  full2(cw), full2(cb), full2(wa), full2(ba), full2(wi), full2(bi), full2(lam)] + xspec,
        out_specs=[ospec,
                   pl.BlockSpec((seqs, 1, w), lambda b, c: (b, 0, 0)),
                   pl.BlockSpec((seqs, C_CONV - 1, w), lambda b, c: (b, 0, 0))],
        out_shape=[jax.ShapeDtypeStruct((proj.shape[0], w), F32),
                   jax.ShapeDtypeStruct((nb, 1, w), F32),
                   jax.ShapeDtypeStruct((nb, C_CONV - 1, w), F32)],
        scratch_shapes=[pltpu.VMEM((seqs, SUBLANES + L, w), F32), pltpu.VMEM((seqs, SUBLANES, w), F32)],
        input_output_aliases=alias,
        compiler_params=_cparams(("parallel", "arbitrary")),
        name="rglru",
    )(proj, proj, h0, cv0, cw, cb, wa, ba, wi, bi, lam, *xarg)


def _outproj_kernel(o1_ref, o2_ref, w1_ref, w2_ref, x_ref, g_ref, b_ref, y_ref):
    mix = _bdot(o1_ref[...], w1_ref[...]) + _bdot(o2_ref[...], w2_ref[...])
    y_ref[...] = _layer_norm(DN_ALPHA * x_ref[...] + mix, g_ref[...], b_ref[...])


def _outproj_ln(o1, o2, w1, w2, x, g, b):
    n = x.shape[0]
    row = lambda a: pl.BlockSpec((TM, a.shape[1]), lambda i: (i, 0))
    full = lambda a: pl.BlockSpec(a.shape, lambda i: (0, 0))
    return pl.pallas_call(
        _outproj_kernel,
        grid=(n // TM,),
        in_specs=[row(o1), row(o2), full(w1), full(w2), row(x), full(g), full(b)],
        out_specs=row(x),
        out_shape=jax.ShapeDtypeStruct(x.shape, F32),
        compiler_params=_cparams(("parallel",)),
        name="outproj_ln",
    )(o1, o2, w1, w2, x, g, b)


ROUTER_E0 = N_GROUPS
MOE_EXPERTS_PER_CHUNK = 2


def _moe_kernel(x_ref, wr_ref, br_ref, w1_ref, w3_ref, w2_ref, g_ref, b_ref, y_ref):
    x = x_ref[...]
    xb = x.astype(BF16)
    xl = (x - xb.astype(F32)).astype(BF16)
    wr = wr_ref[...]
    wh = wr.astype(BF16)
    wl = (wr - wh.astype(F32)).astype(BF16)
    dot = lambda a, b: jnp.dot(a, b, preferred_element_type=F32)
    logits = dot(xb, wh) + (dot(xb, wl) + dot(xl, wh)) + br_ref[...]
    lane = lax.broadcasted_iota(jnp.int32, logits.shape, 1)
    gl = jnp.where(lane < N_GROUPS, logits, -jnp.inf)
    gmax = jnp.max(gl, axis=1, keepdims=True)
    g_gate = 1.0 / jnp.sum(jnp.exp(gl - gmax), axis=1, keepdims=True)
    g_idx = jnp.min(jnp.where(gl == gmax, lane, LANES), axis=1, keepdims=True)
    e_lo = ROUTER_E0 + g_idx * E_PER_GROUP
    el = jnp.where(jnp.logical_and(lane >= e_lo, lane < e_lo + E_PER_GROUP), logits, -jnp.inf)
    e1 = jnp.max(el, axis=1, keepdims=True)
    i1 = jnp.min(jnp.where(el == e1, lane, LANES), axis=1, keepdims=True)
    el2 = jnp.where(lane == i1, -jnp.inf, el)
    e2 = jnp.max(el2, axis=1, keepdims=True)
    i2 = jnp.min(jnp.where(el2 == e2, lane, LANES), axis=1, keepdims=True)
    r = jnp.exp(e2 - e1)
    gw1 = g_gate / (1.0 + r)
    gate = jnp.where(lane == i1, gw1, 0.0) + jnp.where(lane == i2, gw1 * r, 0.0)

    def up(c):
        es = range(c * MOE_EXPERTS_PER_CHUNK, (c + 1) * MOE_EXPERTS_PER_CHUNK)
        r1 = jnp.concatenate([w1_ref[0, e] for e in es], axis=1)
        r3 = jnp.concatenate([w3_ref[0, e] for e in es], axis=1)
        return dot(xb, r1), dot(xb, r3)

    n_chunks = N_EXPERTS // MOE_EXPERTS_PER_CHUNK
    parts = []
    nxt = up(0)
    for c in range(n_chunks):
        h1, h3 = nxt
        if c + 1 < n_chunks:
            nxt = up(c + 1)
        hg = h1 * _sigmoid(h1) * h3
        for ee in range(MOE_EXPERTS_PER_CHUNK):
            e = ROUTER_E0 + c * MOE_EXPERTS_PER_CHUNK + ee
            parts.append((hg[:, ee * D_EXPERT:(ee + 1) * D_EXPERT] * gate[:, e:e + 1]).astype(BF16))
    y = dot(jnp.concatenate(parts, axis=1), w2_ref[0])
    y_ref[...] = _layer_norm(DN_ALPHA * x + y, g_ref[...], b_ref[...])


def _moe_ln(x, wr, br, w1, w3, w2, g, b, layer):
    n = x.shape[0]
    row = pl.BlockSpec((TM, D_MODEL), lambda i: (i, 0))
    full = lambda a: pl.BlockSpec(a.shape, lambda i: (0, 0))
    resident = lambda a: pl.BlockSpec((1,) + a.shape[1:], lambda i: (layer,) + (0,) * (a.ndim - 1),
                                      pipeline_mode=pl.Buffered(1))
    return pl.pallas_call(
        _moe_kernel,
        grid=(n // TM,),
        in_specs=[row, full(wr), full(br), resident(w1), resident(w3), resident(w2), full(g), full(b)],
        out_specs=row,
        out_shape=jax.ShapeDtypeStruct(x.shape, F32),
        compiler_params=pltpu.CompilerParams(dimension_semantics=("parallel",),
                                             vmem_limit_bytes=MOE_VMEM_LIMIT),
        name="moe_ln",
    )(x, wr, br, w1, w3, w2, g, b)


def _positions():
    pos = jnp.zeros((N_TOK,), F32)
    main = (N_META + jnp.arange(SEQ, dtype=jnp.int32)).astype(F32)
    pos = pos.at[:N_MAIN].set(jnp.tile(main, BATCH))
    samp = (PAST_LEN + jnp.minimum(jnp.arange(S_ROWS, dtype=jnp.int32), DEC_SEQ - 1)).astype(F32)
    pos = pos.at[ROW_S:ROW_M].set(jnp.tile(samp, DEC_BATCH))
    pos = pos.at[ROW_M:ROW_M + N_META].set(jnp.arange(N_META, dtype=jnp.int32).astype(F32))
    return pos


def _rope_tables_mla(pos):
    half = A_ROPE // 2
    freqs = ROPE_THETA ** (-jnp.arange(half, dtype=F32) / half)
    ang = pos[:, None] * freqs
    cos, sin = jnp.cos(ang), jnp.sin(ang)
    n = pos.shape[0]
    zeros = lambda w: jnp.zeros((n, w), F32)
    rest = LANES - ROPE_LANE0 - A_ROPE
    c = jnp.concatenate([jnp.ones((n, ROPE_LANE0), F32), cos, cos, zeros(rest)], axis=1)
    s1 = jnp.concatenate([zeros(ROPE_LANE0 + half), sin, zeros(rest)], axis=1)
    s2 = jnp.concatenate([zeros(ROPE_LANE0), -sin, zeros(half + rest)], axis=1)
    return c, s1, s2


def _rope_tables_ret(pos):
    half = D_DK // 2
    freqs = ROPE_THETA ** (-jnp.arange(half, dtype=F32) / half)
    ang = pos[:, None] * freqs
    cos, sin = jnp.cos(ang), jnp.sin(ang)
    return jnp.concatenate([cos, cos], axis=1), jnp.concatenate([-sin, sin], axis=1)


def _pad_cols(a, w):
    return jnp.pad(a, ((0, 0), (0, w - a.shape[1])))


def _even_in_weight(w):
    cuts = [0, A_QLORA, A_QLORA + A_KVLORA, A_QLORA + A_KVLORA + A_ROPE]
    c_q, c_kv, k_r = (w[:, cuts[i]:cuts[i + 1]] for i in range(3))
    m0 = cuts[3]
    mw = B_HEADS * B_DK
    big = w[:, m0:m0 + 4 * mw]
    gates = w[:, m0 + 4 * mw:]
    head = jnp.concatenate([c_q, c_kv, jnp.zeros((D_MODEL, ROPE_LANE0), w.dtype), k_r, gates], axis=1)
    return jnp.concatenate([_pad_cols(head, 1024), big], axis=1).astype(BF16)


def _head_pad_cols(a, width):
    k = a.shape[0]
    return jnp.pad(a, ((0, 0), (0, 0), (0, HEAD_PAD - width))).reshape(k, A_HEADS * HEAD_PAD)


def _block_diag(w):
    eye = jnp.eye(C_BLOCKS, dtype=w.dtype)
    return (w[:, :, None, :] * eye[:, None, :, None]).reshape(C_WIDTH, C_WIDTH)


def _assemble(main, samp, meta):
    w = main.shape[1]
    pad = jnp.zeros((N_TOK - ROW_M - N_META, w), main.dtype)
    return jnp.concatenate([main, samp, meta, pad], axis=0)


def kernel(x_prompt, x_sample, cache_mla, page_table, state_mlstm_c, state_mlstm_n, state_mlstm_m,
           state_rglru_h, state_rglru_conv, state_ret, meta_tokens, w_in_e, g_cq, g_ckv, w_uq, w_uk, w_uv,
           b_mi, b_mf, g_mlstm, w_out_e, w_in_o, conv_w, conv_b, w_ra, b_ra, w_ri, b_ri, lam, g_ret, w_out_o,
           ln_mix_g, ln_mix_b, ln_ffn_g, ln_ffn_b, w_gr, b_gr, w_er, b_er, w_e1, w_e3, w_e2):
    nc_main = SEQ // CHUNK
    xs = jnp.pad(x_sample, ((0, 0), (0, S_ROWS - DEC_SEQ), (0, 0))).reshape(N_SAMP, D_MODEL)
    x = _assemble(x_prompt.reshape(N_MAIN, D_MODEL), xs, meta_tokens)

    cache_t = jnp.swapaxes(cache_mla, 2, 3)
    w1_all, w3_all = w_e1.astype(BF16), w_e3.astype(BF16)
    w2_all = w_e2.astype(BF16).reshape(DEPTH, N_EXPERTS * D_EXPERT, D_MODEL)
    st_c = state_mlstm_c.reshape(N_EVEN * DEC_BATCH, B_HEADS, B_DK, B_DV)
    st_n = state_mlstm_n.reshape(N_EVEN * DEC_BATCH, B_HEADS, B_DK)
    st_m = state_mlstm_m.reshape(N_EVEN * DEC_BATCH, 1, B_HEADS)
    st_h = state_rglru_h.reshape(N_ODD * DEC_BATCH, 1, C_WIDTH)
    st_cv = state_rglru_conv.reshape(N_ODD * DEC_BATCH, C_CONV - 1, C_WIDTH)
    st_s = state_ret.reshape(N_ODD * DEC_BATCH, D_HEADS, D_DK, D_DV)

    pos = _positions()
    tabs_mla = _rope_tables_mla(pos)
    tabs_ret = _rope_tables_ret(pos)
    row2 = lambda a: a.reshape(1, -1)
    spare = {}

    def zbuf(w):
        return spare[w].pop() if spare.get(w) else jnp.zeros((N_TOK, w), F32)

    kv_p, kv_s, c_p, c_s, n_p, n_s, m_p, m_s = [], [], [], [], [], [], [], []
    h_p, h_s, cv_p, cv_s, s_p, s_s = [], [], [], [], [], []

    for l in range(DEPTH):
        j = l // 2
        if l % 2 == 0:
            proj = _matmul(x, _even_in_weight(w_in_e[j]))
            wuq = _head_pad_cols(w_uq[j].reshape(A_QLORA, A_HEADS, A_NOPE + A_ROPE), A_NOPE + A_ROPE).astype(BF16)
            wk = _head_pad_cols(w_uk[j], A_NOPE).astype(BF16)
            wv = _head_pad_cols(w_uv[j], A_V).astype(BF16)
            q, kp, vp, kvrow, kvt = _mla_prep(proj, tabs_mla, row2(g_cq[j]), row2(g_ckv[j]), wuq, wk, wv)
            wuk_t = jnp.transpose(w_uk[j], (1, 2, 0)).astype(BF16)
            wuv_p = jnp.pad(jnp.transpose(w_uv[j], (1, 0, 2)),
                            ((0, 0), (0, 0), (0, HEAD_PAD - A_V))).astype(BF16)
            o1 = _flash(q, kp, vp, zbuf(ATT_W), BATCH, SEQ, ROW_M)
            o1 = _sample_attn(page_table, q, kvrow, wuk_t, wuv_p, cache_t, o1, j)
            o1 = _meta_attn(q, kp, vp, o1, ROW_M)

            bi, bf, gn = row2(b_mi[j]), row2(b_mf[j]), row2(g_mlstm[j])
            zc = jnp.zeros((1, B_HEADS, B_DK, B_DV), F32)
            zn = jnp.zeros((1, B_HEADS, B_DK), F32)
            zm = jnp.zeros((1, 1, B_HEADS), F32)
            o2, c_m, n_m, m_m = _mlstm(proj, ROW_M, 1, 1, N_META, N_META, zc, zn, zm, bi, bf, gn, True,
                                       zbuf(B_HEADS * B_DV))
            o2, c_pm, n_pm, m_pm = _mlstm(proj, 0, BATCH, nc_main, CHUNK, CHUNK, c_m, n_m,
                                          m_m[:, :, :B_HEADS], bi, bf, gn, True, o2)
            o2, c_sm, n_sm, m_sm = _mlstm(proj, ROW_S, DEC_BATCH, 1, S_ROWS, DEC_SEQ, st_c, st_n, st_m,
                                          bi, bf, gn, False, o2, SEQS_PER_STEP, j * DEC_BATCH // SEQS_PER_STEP)

            w1 = jnp.pad(w_out_e[j][:A_HEADS * A_V].reshape(A_HEADS, A_V, D_MODEL),
                         ((0, 0), (0, HEAD_PAD - A_V), (0, 0))).reshape(ATT_W, D_MODEL).astype(BF16)
            w2 = w_out_e[j][A_HEADS * A_V:].astype(BF16)

            kvf = jnp.concatenate([kvt[:A_KVLORA], kvt[KV_ROPE0:KV_ROPE0 + A_ROPE]], axis=0)
            kv_main = kvf[:, :N_MAIN].reshape(A_LAT, BATCH, SEQ)
            kv_meta = jnp.broadcast_to(kvf[:, None, ROW_M:ROW_M + N_META], (A_LAT, BATCH, N_META))
            kv_p.append(jnp.transpose(jnp.concatenate([kv_meta, kv_main], axis=2), (1, 2, 0)))
            kv_sr = kvf[:, ROW_S:ROW_M].reshape(A_LAT, DEC_BATCH, S_ROWS)[:, :, :DEC_SEQ]
            kv_s.append(jnp.transpose(kv_sr, (1, 2, 0)))
            c_p.append(c_pm); n_p.append(n_pm); m_p.append(m_pm[:, 0, :B_HEADS])
            c_s.append(c_sm); n_s.append(n_sm); m_s.append(m_sm[:, 0, :B_HEADS])
        else:
            proj = _matmul(x, w_in_o[j].astype(BF16))
            cw, cb = conv_w[j], row2(conv_b[j])
            wa, wi = _block_diag(w_ra[j]).astype(BF16), _block_diag(w_ri[j]).astype(BF16)
            ba, bi_, lam_ = row2(b_ra[j]), row2(b_ri[j]), row2(lam[j])
            zh = jnp.zeros((1, 1, C_WIDTH), F32)
            zcv = jnp.zeros((1, C_CONV - 1, C_WIDTH), F32)
            args = (cw, cb, wa, ba, wi, bi_, lam_)
            o1, h_m, cv_m = _rglru(proj, ROW_M, 1, 1, N_META, N_META, zh, zcv, *args, True, zbuf(C_WIDTH))
            o1, h_pm, cv_pm = _rglru(proj, 0, BATCH, nc_main, CHUNK, CHUNK, h_m, cv_m, *args, True, o1)
            o1, h_sm, cv_sm = _rglru(proj, ROW_S, DEC_BATCH, 1, S_ROWS, DEC_SEQ, st_h, st_cv,
                                     *args, False, o1, SEQS_PER_STEP_RGLRU, j * DEC_BATCH // SEQS_PER_STEP_RGLRU)

            gn = row2(g_ret[j])
            zs = jnp.zeros((1, D_HEADS, D_DK, D_DV), F32)
            o2, s_m = _retention(proj, tabs_ret, ROW_M, 1, 1, N_META, N_META, zs, gn, True, zbuf(D_HEADS * D_DV))
            o2, s_pm = _retention(proj, tabs_ret, 0, BATCH, nc_main, CHUNK, CHUNK, s_m, gn, True, o2)
            o2, s_sm = _retention(proj, tabs_ret, ROW_S, DEC_BATCH, 1, S_ROWS, DEC_SEQ, st_s, gn, False, o2,
                                  SEQS_PER_STEP, j * DEC_BATCH // SEQS_PER_STEP)

            w1 = w_out_o[j][:C_WIDTH].astype(BF16)
            w2 = w_out_o[j][C_WIDTH:].astype(BF16)
            h_p.append(h_pm[:, 0]); cv_p.append(cv_pm); s_p.append(s_pm)
            h_s.append(h_sm[:, 0]); cv_s.append(cv_sm); s_s.append(s_sm)

        x = _outproj_ln(o1, o2, w1, w2, x, row2(ln_mix_g[l]), row2(ln_mix_b[l]))
        for dead in (o1, o2):
            spare.setdefault(dead.shape[1], []).append(dead)

        wr = jnp.concatenate([w_gr[l], jnp.transpose(w_er[l], (1, 0, 2)).reshape(D_MODEL, N_EXPERTS)], axis=1)
        br = jnp.concatenate([b_gr[l], b_er[l].reshape(N_EXPERTS)])
        x = _moe_ln(x, _pad_cols(wr, LANES), _pad_cols(row2(br), LANES), w1_all, w3_all, w2_all,
                    row2(ln_ffn_g[l]), row2(ln_ffn_b[l]), l)

    y_prompt = x[:N_MAIN].reshape(BATCH, SEQ, D_MODEL)
    y_sample = x[ROW_S:ROW_M].reshape(DEC_BATCH, S_ROWS, D_MODEL)[:, :DEC_SEQ]
    st = jnp.stack
    return (y_prompt, y_sample, st(kv_p), st(kv_s), st(c_p), st(c_s), st(n_p), st(n_s), st(m_p), st(m_s),
            st(h_p), st(h_s), st(cv_p), st(cv_s), st(s_p), st(s_s))
```

```python
import functools
import math

import jax
import jax.numpy as jnp
from jax import lax
from jax.experimental import pallas as pl
from jax.experimental.pallas import tpu as pltpu

F32 = jnp.float32
BF16 = jnp.bfloat16

D_MODEL = 1024
BATCH = 4
SEQ = 4096
DEPTH = 4
DEC_BATCH = 128
DEC_SEQ = 4
PAST_LEN = 8192
PAGE_SIZE = 128
N_PAGES = PAST_LEN // PAGE_SIZE
N_META = 16
N_EVEN = (DEPTH + 1) // 2
N_ODD = DEPTH // 2
CHUNK = 128
ROPE_THETA = 10000.0
A_HEADS = 8
A_NOPE = 64
A_ROPE = 32
A_V = 64
A_QLORA = 384
A_KVLORA = 256
A_LAT = A_KVLORA + A_ROPE
B_HEADS = 4
B_DK = 128
B_DV = 128
C_WIDTH = 512
C_BLOCKS = 8
C_BW = C_WIDTH // C_BLOCKS
C_CONV = 4
C_POW = 8.0
D_HEADS = 4
D_DK = 128
D_DV = 128
N_GROUPS = 4
E_PER_GROUP = 8
N_EXPERTS = N_GROUPS * E_PER_GROUP
D_EXPERT = 128
DN_ALPHA = (2.0 * DEPTH) ** 0.25
EPS = 1e-5

LANES = 128
SUBLANES = 8
VMEM_LIMIT = 48 * 1024 * 1024
MOE_VMEM_LIMIT = 56 * 1024 * 1024

S_ROWS = 8
N_MAIN = BATCH * SEQ
ROW_S = N_MAIN
N_SAMP = DEC_BATCH * S_ROWS
ROW_M = ROW_S + N_SAMP
TM = 512
N_TOK = -(-(ROW_M + N_META) // TM) * TM

HEAD_PAD = LANES
ATT_W = A_HEADS * HEAD_PAD
ROPE_LANE0 = A_NOPE
PROJ_E_W = 3072
GATE_COL = 640
MI_LANE = ROPE_LANE0 + A_ROPE
MF_LANE = MI_LANE + B_HEADS

TQ = 512
TK = 512
SEQS_PER_STEP = 4
SEQS_PER_STEP_RET = 8
SEQS_PER_STEP_RGLRU = 16


def _cparams(sem):
    return pltpu.CompilerParams(dimension_semantics=sem, vmem_limit_bytes=VMEM_LIMIT)


def _bdot(a, b):
    return jnp.dot(a.astype(BF16), b.astype(BF16), preferred_element_type=F32)


def _bdot_nt(a, b):
    return lax.dot_general(a.astype(BF16), b.astype(BF16), (((1,), (1,)), ((), ())),
                           preferred_element_type=F32)


def _bdot_tn(a, b):
    return lax.dot_general(a.astype(BF16), b.astype(BF16), (((0,), (0,)), ((), ())),
                           preferred_element_type=F32)


def _sigmoid(x):
    return 1.0 / (1.0 + jnp.exp(-x))


def _layer_norm(x, g, b):
    mu = jnp.mean(x, axis=-1, keepdims=True)
    xc = x - mu
    var = jnp.mean(xc * xc, axis=-1, keepdims=True)
    return xc * lax.rsqrt(var + EPS) * g + b


def _head_norm(x, g):
    mu = jnp.mean(x, axis=-1, keepdims=True)
    xc = x - mu
    var = jnp.mean(xc * xc, axis=-1, keepdims=True)
    return xc * lax.rsqrt(var + EPS) * g


def _mm_kernel(x_ref, w_ref, o_ref):
    o_ref[...] = _bdot(x_ref[...], w_ref[...])


def _matmul(x, w):
    n, k = x.shape
    m = w.shape[1]
    return pl.pallas_call(
        _mm_kernel,
        grid=(n // TM,),
        in_specs=[pl.BlockSpec((TM, k), lambda i: (i, 0)),
                  pl.BlockSpec((k, m), lambda i: (0, 0))],
        out_specs=pl.BlockSpec((TM, m), lambda i: (i, 0)),
        out_shape=jax.ShapeDtypeStruct((n, m), F32),
        compiler_params=_cparams(("parallel",)),
        name="proj_in",
    )(x, w)


def _rope_tile(x, c, s1, s2):
    half = A_ROPE // 2
    return x * c + pltpu.roll(x, half, 1) * s1 + pltpu.roll(x, LANES - half, 1) * s2


def _mla_prep_kernel(a_ref, c_ref, s1_ref, s2_ref, gq_ref, gkv_ref, wuq_ref, wk_ref, wv_ref,
                     q_ref, k_ref, v_ref, kv_ref, kvt_ref):
    a = a_ref[...]
    c, s1, s2 = c_ref[...], s1_ref[...], s2_ref[...]
    cq = a[:, :A_QLORA]
    ckv = a[:, A_QLORA:A_QLORA + A_KVLORA]
    cq = cq * lax.rsqrt(jnp.mean(cq * cq, axis=-1, keepdims=True) + EPS) * gq_ref[...]
    ckv = ckv * lax.rsqrt(jnp.mean(ckv * ckv, axis=-1, keepdims=True) + EPS) * gkv_ref[...]
    kr = _rope_tile(a[:, GATE_COL:GATE_COL + LANES], c, s1, s2)
    kv_ref[:, :A_KVLORA] = ckv
    kv_ref[:, A_KVLORA:] = kr
    kvt_ref[:A_KVLORA, :] = ckv.T
    kvt_ref[A_KVLORA:, :] = kr.T
    q = _bdot(cq, wuq_ref[...])
    kn = _bdot(ckv, wk_ref[...])
    lane = lax.broadcasted_iota(jnp.int32, (1, ATT_W), 1)
    ones_col = jnp.where(lane % HEAD_PAD == A_V, 1.0, 0.0)
    v_ref[...] = (_bdot(ckv, wv_ref[...]) + ones_col).astype(BF16)
    for h in range(A_HEADS):
        sl = slice(h * HEAD_PAD, (h + 1) * HEAD_PAD)
        q_ref[:, sl] = _rope_tile(q[:, sl], c, s1, s2)
        k_ref[:, sl] = (kn[:, sl] + kr).astype(BF16)


def _mla_prep(proj, tabs, gq, gkv, wuq, wk, wv):
    n = proj.shape[0]
    row = lambda w: pl.BlockSpec((TM, w), lambda i: (i, 0))
    full = lambda a: pl.BlockSpec(a.shape, lambda i: (0,) * a.ndim)
    return pl.pallas_call(
        _mla_prep_kernel,
        grid=(n // TM,),
        in_specs=[row(GATE_COL + LANES), row(LANES), row(LANES), row(LANES),
                  full(gq), full(gkv), full(wuq), full(wk), full(wv)],
        out_specs=[row(ATT_W), row(ATT_W), row(ATT_W), row(A_KVLORA + LANES),
                   pl.BlockSpec((A_KVLORA + LANES, TM), lambda i: (0, i))],
        out_shape=[jax.ShapeDtypeStruct((n, ATT_W), F32),
                   jax.ShapeDtypeStruct((n, ATT_W), BF16),
                   jax.ShapeDtypeStruct((n, ATT_W), BF16),
                   jax.ShapeDtypeStruct((n, A_KVLORA + LANES), F32),
                   jax.ShapeDtypeStruct((A_KVLORA + LANES, n), F32)],
        compiler_params=_cparams(("parallel",)),
        name="mla_prep",
    )(proj, *tabs, gq, gkv, wuq, wk, wv)


ATT_SCALE = (A_NOPE + A_ROPE) ** -0.5


QK_SCALE2 = ATT_SCALE * math.log2(math.e)


def _flash_kernel(qi_ref, ki_ref, q_ref, k_ref, v_ref, km_ref, vm_ref, buf_ref, o_ref, qs_ref, *head_scr):
    del buf_ref
    m_refs, acc_refs = head_scr[:A_HEADS], head_scr[A_HEADS:]
    qi = qi_ref[pl.program_id(1)]
    ki = ki_ref[pl.program_id(1)]
    kdiag = (qi * TQ) // TK

    @pl.when(ki == 0)
    def _init():
        qs_ref[...] = (q_ref[...] * QK_SCALE2).astype(BF16)
        for h in range(A_HEADS):
            sl = slice(h * HEAD_PAD, (h + 1) * HEAD_PAD)
            s = _bdot_nt(qs_ref[:, sl], km_ref[:, sl])
            m = jnp.max(s, axis=1, keepdims=True)
            m_refs[h][...] = jnp.broadcast_to(m, (TQ, HEAD_PAD))
            acc_refs[h][...] = _bdot(jnp.exp2(s - m), vm_ref[:, sl])

    def step(masked):
        if masked:
            row = lax.broadcasted_iota(jnp.int32, (TQ, TK), 0)
            col = lax.broadcasted_iota(jnp.int32, (TQ, TK), 1)
            visible = col <= row + (qi * TQ - ki * TK)
        def scores(h):
            sl = slice(h * HEAD_PAD, (h + 1) * HEAD_PAD)
            return _bdot_nt(qs_ref[:, sl], k_ref[:, sl])

        s_next = scores(0)
        for h in range(A_HEADS):
            sl = slice(h * HEAD_PAD, (h + 1) * HEAD_PAD)
            s = s_next
            if h + 1 < A_HEADS:
                s_next = scores(h + 1)
            if masked:
                s = jnp.where(visible, s, -jnp.inf)
            m_prev = m_refs[h][...]
            m_new = jnp.maximum(m_prev, jnp.max(s, axis=1, keepdims=True))
            alpha = jnp.exp2(m_prev - m_new)
            p = jnp.exp2(s - jnp.concatenate([m_new] * (TK // HEAD_PAD), axis=1))
            acc_refs[h][...] = alpha * acc_refs[h][...] + _bdot(p, v_ref[:, sl])
            m_refs[h][...] = m_new

    pl.when(ki < kdiag)(functools.partial(step, False))
    pl.when(ki == kdiag)(functools.partial(step, True))

    @pl.when(ki == kdiag)
    def _fin():
        for h in range(A_HEADS):
            a = acc_refs[h][...]
            o_ref[:, h * HEAD_PAD:(h + 1) * HEAD_PAD] = a / a[:, A_V:A_V + 1]


def _flash(q, k, v, obuf, nb, seq, meta_row):
    nq = seq // TQ
    nk = seq // TK
    meta_blk = meta_row // N_META
    pairs = [(i, j) for i in range(nq) for j in range((i * TQ) // TK + 1)]
    qi_tab = jnp.asarray([p[0] for p in pairs], jnp.int32)
    ki_tab = jnp.asarray([p[1] for p in pairs], jnp.int32)
    qmap = lambda b, p, qt, kt: (b * nq + qt[p], 0)
    kmap = lambda b, p, qt, kt: (b * nk + kt[p], 0)
    mmap = lambda b, p, qt, kt: (meta_blk, 0)
    grid_spec = pltpu.PrefetchScalarGridSpec(
        num_scalar_prefetch=2,
        grid=(nb, len(pairs)),
        in_specs=[pl.BlockSpec((TQ, ATT_W), qmap),
                  pl.BlockSpec((TK, ATT_W), kmap),
                  pl.BlockSpec((TK, ATT_W), kmap),
                  pl.BlockSpec((N_META, ATT_W), mmap),
                  pl.BlockSpec((N_META, ATT_W), mmap),
                  pl.BlockSpec(memory_space=pl.ANY)],
        out_specs=pl.BlockSpec((TQ, ATT_W), qmap),
        scratch_shapes=([pltpu.VMEM((TQ, ATT_W), BF16)]
                        + [pltpu.VMEM((TQ, HEAD_PAD), F32)] * A_HEADS
                        + [pltpu.VMEM((TQ, HEAD_PAD), F32)] * A_HEADS),
    )
    return pl.pallas_call(
        _flash_kernel,
        grid_spec=grid_spec,
        out_shape=jax.ShapeDtypeStruct(obuf.shape, F32),
        input_output_aliases={7: 0},
        compiler_params=_cparams(("parallel", "arbitrary")),
        name="mla_flash",
    )(qi_tab, ki_tab, q, k, v, k, v, obuf)


def _meta_attn_kernel(q_ref, k_ref, v_ref, buf_ref, o_ref):
    del buf_ref
    n = q_ref.shape[0]
    row = lax.broadcasted_iota(jnp.int32, (n, n), 0)
    col = lax.broadcasted_iota(jnp.int32, (n, n), 1)
    for h in range(A_HEADS):
        sl = slice(h * HEAD_PAD, (h + 1) * HEAD_PAD)
        s = _bdot_nt(q_ref[:, sl], k_ref[:, sl]) * ATT_SCALE
        s = jnp.where(col <= row, s, -jnp.inf)
        p = jnp.exp(s - jnp.max(s, axis=1, keepdims=True))
        o_ref[:, sl] = _bdot(p, v_ref[:, sl]) / jnp.sum(p, axis=1, keepdims=True)


def _meta_attn(q, k, v, buf, meta_row):
    blk = meta_row // N_META
    spec = pl.BlockSpec((N_META, ATT_W), lambda i: (blk, 0))
    return pl.pallas_call(
        _meta_attn_kernel,
        grid=(1,),
        in_specs=[spec, spec, spec, pl.BlockSpec(memory_space=pl.ANY)],
        out_specs=spec,
        out_shape=jax.ShapeDtypeStruct(buf.shape, F32),
        input_output_aliases={3: 0},
        compiler_params=_cparams(("arbitrary",)),
        name="mla_meta_attn",
    )(q, k, v, buf)


PAGES_PER_STEP = 8
KV_ROPE0 = A_KVLORA + ROPE_LANE0


def _sattn_kernel(pt_ref, q_ref, kvn_ref, wuk_ref, wuv_ref, cache_ref, obuf_ref, o_ref, buf_ref, sem_ref, *, layer):
    del obuf_ref
    b = pl.program_id(0)
    nb = pl.num_programs(0)

    def page_copy(seq, p, slot):
        return pltpu.make_async_copy(cache_ref.at[layer, pt_ref[seq, p]], buf_ref.at[slot, p], sem_ref.at[slot])

    def start(seq, slot):
        def body(p, carry):
            page_copy(seq, p, slot).start()
            return carry
        lax.fori_loop(0, N_PAGES, body, 0, unroll=PAGES_PER_STEP)

    def wait(seq, slot):
        def body(p, carry):
            page_copy(seq, p, slot).wait()
            return carry
        lax.fori_loop(0, N_PAGES, body, 0, unroll=PAGES_PER_STEP)

    @pl.when(b == 0)
    def _first():
        start(0, 0)

    @pl.when(b + 1 < nb)
    def _next():
        start(b + 1, (b + 1) % 2)

    slot = b % 2
    wait(b, slot)

    q = q_ref[...]
    ql, qr = [], []
    for h in range(A_HEADS):
        ql.append(_bdot(q[:, h * HEAD_PAD:h * HEAD_PAD + A_NOPE], wuk_ref[h]))
        qr.append(q[:, h * HEAD_PAD + ROPE_LANE0:h * HEAD_PAD + ROPE_LANE0 + A_ROPE])
    ql = (jnp.concatenate(ql, axis=0) * QK_SCALE2).astype(BF16)
    qr = (jnp.concatenate(qr, axis=0) * QK_SCALE2).astype(BF16)
    rows = A_HEADS * S_ROWS
    keys = PAGES_PER_STEP * PAGE_SIZE

    def scores(c):
        pages = [buf_ref[slot, c * PAGES_PER_STEP + i] for i in range(PAGES_PER_STEP)]
        ckv = jnp.concatenate([pg[:A_KVLORA].astype(BF16) for pg in pages], axis=1)
        kr = jnp.concatenate([pg[A_KVLORA:].astype(BF16) for pg in pages], axis=1)
        return ckv, _bdot(ql, ckv) + _bdot(qr, kr)

    m_prev = jnp.full((rows, 1), -jnp.inf, F32)
    l_prev = jnp.zeros((rows, 1), F32)
    acc = jnp.zeros((rows, A_KVLORA), F32)
    n_chunks = N_PAGES // PAGES_PER_STEP
    nxt = scores(0)
    for c in range(n_chunks):
        ckv, s = nxt
        if c + 1 < n_chunks:
            nxt = scores(c + 1)
        m_new = jnp.maximum(m_prev, jnp.max(s, axis=1, keepdims=True))
        alpha = jnp.exp2(m_prev - m_new)
        p = jnp.exp2(s - m_new)
        l_prev = alpha * l_prev + jnp.sum(p, axis=1, keepdims=True)
        acc = alpha * acc + _bdot_nt(p, ckv)
        m_prev = m_new

    kvn = kvn_ref[...]
    ckv_n = kvn[:, :A_KVLORA]
    kr_n = kvn[:, KV_ROPE0:KV_ROPE0 + A_ROPE]
    s = _bdot_nt(ql, ckv_n) + _bdot_nt(qr, kr_n)
    t_row = lax.broadcasted_iota(jnp.int32, (rows, S_ROWS), 0) % S_ROWS
    t_col = lax.broadcasted_iota(jnp.int32, (rows, S_ROWS), 1)
    s = jnp.where(t_col <= jnp.minimum(t_row, DEC_SEQ - 1), s, -jnp.inf)
    m_new = jnp.maximum(m_prev, jnp.max(s, axis=1, keepdims=True))
    alpha = jnp.exp2(m_prev - m_new)
    p = jnp.exp2(s - m_new)
    l_new = alpha * l_prev + jnp.sum(p, axis=1, keepdims=True)
    o_lat = (alpha * acc + _bdot(p, ckv_n)) / l_new
    for h in range(A_HEADS):
        o_ref[:, h * HEAD_PAD:(h + 1) * HEAD_PAD] = _bdot(o_lat[h * S_ROWS:(h + 1) * S_ROWS], wuv_ref[h])


def _sample_attn(page_table, q, kvrow, wuk_t, wuv_p, cache_t, obuf, layer):
    nb = page_table.shape[0]
    blk0 = ROW_S // S_ROWS
    grid_spec = pltpu.PrefetchScalarGridSpec(
        num_scalar_prefetch=1,
        grid=(nb,),
        in_specs=[pl.BlockSpec((S_ROWS, ATT_W), lambda b, pt: (blk0 + b, 0)),
                  pl.BlockSpec((S_ROWS, A_KVLORA + LANES), lambda b, pt: (blk0 + b, 0)),
                  pl.BlockSpec(wuk_t.shape, lambda b, pt: (0, 0, 0)),
                  pl.BlockSpec(wuv_p.shape, lambda b, pt: (0, 0, 0)),
                  pl.BlockSpec(memory_space=pl.ANY),
                  pl.BlockSpec(memory_space=pl.ANY)],
        out_specs=pl.BlockSpec((S_ROWS, ATT_W), lambda b, pt: (blk0 + b, 0)),
        scratch_shapes=[pltpu.VMEM((2, N_PAGES, A_LAT, PAGE_SIZE), F32),
                        pltpu.SemaphoreType.DMA((2,))],
    )
    return pl.pallas_call(
        functools.partial(_sattn_kernel, layer=layer),
        grid_spec=grid_spec,
        out_shape=jax.ShapeDtypeStruct(obuf.shape, F32),
        input_output_aliases={6: 0},
        compiler_params=_cparams(("arbitrary",)),
        name="mla_sample_attn",
    )(page_table, q, kvrow, wuk_t, wuv_p, cache_t, obuf)


def _row_of(col_vec, eye):
    return jnp.sum(jnp.where(eye, col_vec, 0.0), axis=0, keepdims=True)


def _mlstm_kernel(g_ref, q_ref, k_ref, v_ref, og_ref, c0_ref, n0_ref, m0_ref, bi_ref, bf_ref, gn_ref, buf_ref,
                  o_ref, c_ref, n_ref, m_ref, c_scr, n_scr, m_scr, *, n_valid, seqs):
    del buf_ref
    ci = pl.program_id(1)
    L = q_ref.shape[0] // seqs

    @pl.when(ci == 0)
    def _load():
        c_scr[...] = c0_ref[...]
        for s in range(seqs):
            n_scr[s, 0:B_HEADS, :] = n0_ref[s]
            for h in range(B_HEADS):
                m_scr[s, h:h + 1, :] = jnp.broadcast_to(m0_ref[s, 0:1, h:h + 1], (1, LANES))

    row = lax.broadcasted_iota(jnp.int32, (L, L), 0)
    col = lax.broadcasted_iota(jnp.int32, (L, L), 1)
    eye = row == col
    tril = col <= row
    valid = lax.broadcasted_iota(jnp.int32, (L, 1), 0) < n_valid
    lane = lax.broadcasted_iota(jnp.int32, (1, LANES), 1)
    wide = L == LANES and B_DV == LANES and B_DK == LANES
    cv = (lambda t: jnp.broadcast_to(t, (L, LANES))) if wide else (lambda t: t)
    m_outs = []
    for s in range(seqs):
        rs = slice(s * L, (s + 1) * L)
        g = g_ref[rs, :]
        qs, ks, vs, qk, qc = [], [], [], [], []
        for h in range(B_HEADS):
            sl = slice(h * B_DK, (h + 1) * B_DK)
            qs.append(q_ref[rs, sl])
            ks.append(k_ref[rs, sl] * (B_DK ** -0.5))
            vs.append(v_ref[rs, sl])
            qk.append(_bdot_nt(qs[h], ks[h]))
            qc.append(_bdot(qs[h], c_scr[s, h]))
        m_out = jnp.zeros((1, LANES), F32)
        for h in range(B_HEADS):
            sl = slice(h * B_DK, (h + 1) * B_DK)
            q, k, v = qs[h], ks[h], vs[h]
            ig_col = cv(jnp.where(valid, g[:, MI_LANE + h:MI_LANE + h + 1] + bi_ref[0:1, h:h + 1], -jnp.inf))
            fpre = g[:, MF_LANE + h:MF_LANE + h + 1] + bf_ref[0:1, h:h + 1]
            lf_col = cv(jnp.where(valid, jnp.minimum(fpre, 0.0) - jnp.log1p(jnp.exp(-jnp.abs(fpre))), 0.0))
            ig_row = _row_of(ig_col, eye)
            lf_row = _row_of(lf_col, eye)
            b_col = cv(jnp.sum(jnp.where(tril, lf_row, 0.0), axis=1, keepdims=True))
            b_row = jnp.sum(jnp.where(row <= col, lf_col, 0.0), axis=0, keepdims=True)
            m_prev = m_scr[s, h:h + 1, :] if wide else m_scr[s, h:h + 1, 0:1]
            c_prev = c_scr[s, h]
            n_prev = n_scr[s, h:h + 1, :]
            dmat = jnp.where(tril, b_col - b_row + ig_row, -jnp.inf)
            inter = b_col + m_prev
            m_t = jnp.maximum(inter, cv(jnp.max(dmat, axis=1, keepdims=True)))
            w_inter = jnp.exp(inter - m_t)
            sc = qk[h] * jnp.exp(dmat - m_t)
            num = w_inter * qc[h] + _bdot(sc, v)
            den = (w_inter * cv(jnp.sum(q * n_prev, axis=1, keepdims=True))
                   + cv(jnp.sum(sc, axis=1, keepdims=True)))
            hh = num / jnp.maximum(jnp.abs(den), jnp.exp(-m_t))
            m_new = m_t[n_valid - 1:n_valid, :]
            b_last = b_col[n_valid - 1:n_valid, :]
            w_end = jnp.exp(b_last - b_col + ig_col - m_new)
            decay = jnp.exp(b_last + m_prev - m_new)
            kw = k * w_end
            c_scr[s, h] = decay * c_prev + _bdot_tn(kw, v)
            n_scr[s, h:h + 1, :] = decay * n_prev + jnp.sum(kw, axis=0, keepdims=True)
            m_scr[s, h:h + 1, :] = jnp.broadcast_to(m_new, (1, LANES))
            m_out = jnp.where(lane == h, m_new, m_out)
            o_ref[rs, sl] = _head_norm(hh, gn_ref[0:1, sl]) * _sigmoid(og_ref[rs, sl])
        m_outs.append(m_out)

    @pl.when(ci == pl.num_programs(1) - 1)
    def _store():
        c_ref[...] = c_scr[...]
        for s in range(seqs):
            n_ref[s] = n_scr[s, 0:B_HEADS, :]
            m_ref[s] = m_outs[s]


def _stream_out(obuf, n_in, w, row0, nc, L):
    blk0 = row0 // L
    spec = pl.BlockSpec((L, w), lambda b, c: (blk0 + b * nc + c, 0))
    return spec, [pl.BlockSpec(memory_space=pl.ANY)], [obuf], {n_in: 0}


def _mlstm(proj, row0, nb, nc, L, n_valid, c0, n0, m0, bi, bf, gn, shared_state, obuf, seqs=1, sb0=0):
    assert seqs == 1 or (nc == 1 and not shared_state and nb % seqs == 0)
    R = L * seqs
    blk0 = row0 // R
    rmap = lambda cb: (lambda b, c: (blk0 + b * nc + c, cb))
    smap = (lambda b, c: (0, 0, 0, 0)) if shared_state else (lambda b, c: (sb0 + b, 0, 0, 0))
    smap3 = (lambda b, c: (0, 0, 0)) if shared_state else (lambda b, c: (sb0 + b, 0, 0))
    w = B_HEADS * B_DK
    full2 = lambda a: pl.BlockSpec(a.shape, lambda b, c: (0, 0))
    ospec, xspec, xarg, alias = _stream_out(obuf, 11, w, row0, nc, R)
    return pl.pallas_call(
        functools.partial(_mlstm_kernel, n_valid=n_valid, seqs=seqs),
        grid=(nb // seqs, nc),
        in_specs=[pl.BlockSpec((R, LANES), rmap(GATE_COL // LANES)),
                  pl.BlockSpec((R, w), rmap(2)), pl.BlockSpec((R, w), rmap(3)),
                  pl.BlockSpec((R, w), rmap(4)), pl.BlockSpec((R, w), rmap(5)),
                  pl.BlockSpec((seqs, B_HEADS, B_DK, B_DV), smap),
                  pl.BlockSpec((seqs, B_HEADS, B_DK), smap3),
                  pl.BlockSpec((seqs, 1, B_HEADS), smap3),
                  full2(bi), full2(bf), full2(gn)] + xspec,
        out_specs=[ospec,
                   pl.BlockSpec((seqs, B_HEADS, B_DK, B_DV), lambda b, c: (b, 0, 0, 0)),
                   pl.BlockSpec((seqs, B_HEADS, B_DK), lambda b, c: (b, 0, 0)),
                   pl.BlockSpec((seqs, 1, LANES), lambda b, c: (b, 0, 0))],
        out_shape=[jax.ShapeDtypeStruct((proj.shape[0], w), F32),
                   jax.ShapeDtypeStruct((nb, B_HEADS, B_DK, B_DV), F32),
                   jax.ShapeDtypeStruct((nb, B_HEADS, B_DK), F32),
                   jax.ShapeDtypeStruct((nb, 1, LANES), F32)],
        scratch_shapes=[pltpu.VMEM((seqs, B_HEADS, B_DK, B_DV), F32),
                        pltpu.VMEM((seqs, SUBLANES, B_DK), F32),
                        pltpu.VMEM((seqs, SUBLANES, LANES), F32)],
        input_output_aliases=alias,
        compiler_params=_cparams(("parallel", "arbitrary")),
        name="mlstm",
    )(proj, proj, proj, proj, proj, c0, n0, m0, bi, bf, gn, *xarg)


RET_LOG_GAMMA = tuple(math.log1p(-(2.0 ** (-5.0 - h))) for h in range(D_HEADS))


def _ret_kernel(q_ref, k_ref, v_ref, g_ref, cr_ref, sr_ref, s0_ref, gn_ref, buf_ref, o_ref, s_ref, s_scr, *,
                n_valid, seqs):
    del buf_ref
    ci = pl.program_id(1)
    L = q_ref.shape[0] // seqs

    @pl.when(ci == 0)
    def _load():
        s_scr[...] = s0_ref[...]

    row = lax.broadcasted_iota(jnp.int32, (L, L), 0)
    col = lax.broadcasted_iota(jnp.int32, (L, L), 1)
    rel = (row - col).astype(F32)
    t_col = lax.broadcasted_iota(jnp.int32, (L, D_DV if D_DV == D_DK else 1), 0)
    tf = t_col.astype(F32)
    half = D_DK // 2
    for s in range(seqs):
        rs = slice(s * L, (s + 1) * L)
        cr, sr = cr_ref[rs, :], sr_ref[rs, :]
        qs, ks, vs, qk, qs_state = [], [], [], [], []
        for h in range(D_HEADS):
            sl = slice(h * D_DK, (h + 1) * D_DK)
            q = q_ref[rs, sl]
            k = k_ref[rs, sl]
            qs.append(q * cr + pltpu.roll(q, half, 1) * sr)
            ks.append((k * cr + pltpu.roll(k, half, 1) * sr) * (D_DK ** -0.5))
            vs.append(v_ref[rs, sl])
            qk.append(_bdot_nt(qs[h], ks[h]))
            qs_state.append(_bdot(qs[h], s_scr[s, h]))
        for h in range(D_HEADS):
            lg = RET_LOG_GAMMA[h]
            sl = slice(h * D_DK, (h + 1) * D_DK)
            k, v = ks[h], vs[h]
            dec = jnp.where(rel >= 0, jnp.exp(lg * jnp.maximum(rel, 0.0)), 0.0)
            inner = qk[h] * dec
            o = _bdot(inner, v) + jnp.exp((tf + 1.0) * lg) * qs_state[h]
            w_end = jnp.where(t_col < n_valid, jnp.exp((n_valid - 1.0 - tf) * lg), 0.0)
            s_scr[s, h] = math.exp(n_valid * lg) * s_scr[s, h] + _bdot_tn(k * w_end, v)
            gate = g_ref[rs, sl]
            o_ref[rs, sl] = _head_norm(o, gn_ref[0:1, sl]) * (gate * _sigmoid(gate))

    @pl.when(ci == pl.num_programs(1) - 1)
    def _store():
        s_ref[...] = s_scr[...]


def _retention(proj, tabs, row0, nb, nc, L, n_valid, s0, gn, shared_state, obuf, seqs=1, sb0=0):
    assert seqs == 1 or (nc == 1 and not shared_state and nb % seqs == 0)
    R = L * seqs
    blk0 = row0 // R
    rmap = lambda cb: (lambda b, c: (blk0 + b * nc + c, cb))
    smap = (lambda b, c: (0, 0, 0, 0)) if shared_state else (lambda b, c: (sb0 + b, 0, 0, 0))
    w = D_HEADS * D_DK
    ospec, xspec, xarg, alias = _stream_out(obuf, 8, w, row0, nc, R)
    return pl.pallas_call(
        functools.partial(_ret_kernel, n_valid=n_valid, seqs=seqs),
        grid=(nb // seqs, nc),
        in_specs=[pl.BlockSpec((R, w), rmap(2)), pl.BlockSpec((R, w), rmap(3)),
                  pl.BlockSpec((R, w), rmap(4)), pl.BlockSpec((R, w), rmap(5)),
                  pl.BlockSpec((R, LANES), rmap(0)), pl.BlockSpec((R, LANES), rmap(0)),
                  pl.BlockSpec((seqs, D_HEADS, D_DK, D_DV), smap),
                  pl.BlockSpec(gn.shape, lambda b, c: (0, 0))] + xspec,
        out_specs=[ospec,
                   pl.BlockSpec((seqs, D_HEADS, D_DK, D_DV), lambda b, c: (b, 0, 0, 0))],
        out_shape=[jax.ShapeDtypeStruct((proj.shape[0], w), F32),
                   jax.ShapeDtypeStruct((nb, D_HEADS, D_DK, D_DV), F32)],
        scratch_shapes=[pltpu.VMEM((seqs, D_HEADS, D_DK, D_DV), F32)],
        input_output_aliases=alias,
        compiler_params=_cparams(("parallel", "arbitrary")),
        name="retention",
    )(proj, proj, proj, proj, tabs[0], tabs[1], s0, gn, *xarg)


def _rglru_kernel(x_ref, gr_ref, h0_ref, cv0_ref, cw_ref, cb_ref, wa_ref, ba_ref, wi_ref, bi_ref, lam_ref, buf_ref,
                  o_ref, h_ref, cv_ref, xp_scr, h_scr, *, n_valid, seqs):
    del buf_ref
    ci = pl.program_id(1)
    R = x_ref.shape[0]
    L = R // seqs
    tail = C_CONV - 1

    @pl.when(ci == 0)
    def _load():
        for s in range(seqs):
            xp_scr[s, SUBLANES - tail:SUBLANES, :] = cv0_ref[s]
            h_scr[s, 0:1, :] = h0_ref[s]

    us, h0s, new_tails = [], [], []
    for s in range(seqs):
        xp_scr[s, SUBLANES:SUBLANES + L, :] = x_ref[s * L:(s + 1) * L, :]
        us.append(sum(xp_scr[s, SUBLANES - tail + j:SUBLANES - tail + j + L, :] * cw_ref[j:j + 1, :]
                      for j in range(C_CONV)))
        new_tails.append(xp_scr[s, SUBLANES + n_valid - tail:SUBLANES + n_valid, :])
        xp_scr[s, SUBLANES - tail:SUBLANES, :] = new_tails[s]
        h0s.append(jnp.broadcast_to(h_scr[s, 0:1, :], (L, C_WIDTH)))
    u = cb_ref[...] + (us[0] if seqs == 1 else jnp.concatenate(us, axis=0))
    h_prev = h0s[0] if seqs == 1 else jnp.concatenate(h0s, axis=0)
    r = _sigmoid(_bdot(u, wa_ref[...]) + ba_ref[...])
    i = _sigmoid(_bdot(u, wi_ref[...]) + bi_ref[...])
    nlam = -lam_ref[...]
    softplus = jnp.maximum(nlam, 0.0) + jnp.log1p(jnp.exp(-jnp.abs(nlam)))
    log_a = -C_POW * r * softplus
    a = jnp.exp(log_a)
    b = jnp.sqrt(-jnp.tanh(log_a) * (a * a + 1.0)) * (i * u)
    t_idx = lax.broadcasted_iota(jnp.int32, (R, 1), 0) % L
    step = 1
    while step < L:
        keep = t_idx >= step
        a_sh = jnp.where(keep, pltpu.roll(a, step, 0), 1.0)
        b_sh = jnp.where(keep, pltpu.roll(b, step, 0), 0.0)
        b = a * b_sh + b
        a = a * a_sh
        step *= 2
    hseq = a * h_prev + b
    h_lasts = [hseq[s * L + n_valid - 1:s * L + n_valid, :] for s in range(seqs)]
    for s in range(seqs):
        h_scr[s, 0:1, :] = h_lasts[s]
    gr = gr_ref[...]
    gelu = 0.5 * gr * (1.0 + jnp.tanh(math.sqrt(2.0 / math.pi) * (gr + 0.044715 * (gr * gr * gr))))
    o_ref[...] = hseq * gelu

    @pl.when(ci == pl.num_programs(1) - 1)
    def _store():
        for s in range(seqs):
            h_ref[s] = h_lasts[s]
            cv_ref[s] = new_tails[s]


def _rglru(proj, row0, nb, nc, L, n_valid, h0, cv0, cw, cb, wa, ba, wi, bi, lam, shared_state, obuf, seqs=1, sb0=0):
    assert seqs == 1 or (nc == 1 and not shared_state and nb % seqs == 0)
    R = L * seqs
    blk0 = row0 // R
    rmap = lambda cb_: (lambda b, c: (blk0 + b * nc + c, cb_))
    smap = (lambda b, c: (0, 0, 0)) if shared_state else (lambda b, c: (sb0 + b, 0, 0))
    full2 = lambda a: pl.BlockSpec(a.shape, lambda b, c: (0, 0))
    w = C_WIDTH
    ospec, xspec, xarg, alias = _stream_out(obuf, 11, w, row0, nc, R)
    return pl.pallas_call(
        functools.partial(_rglru_kernel, n_valid=n_valid, seqs=seqs),
        grid=(nb // seqs, nc),
        in_specs=[pl.BlockSpec((R, w), rmap(0)), pl.BlockSpec((R, w), rmap(1)),
                  pl.BlockSpec((seqs, 1, w), smap), pl.BlockSpec((seqs, C_CONV - 1, w), smap),
                  full2(cw), full2(cb), full2(wa), full2(ba), full2(wi), full2(bi), full2(lam)] + xspec,
        out_specs=[ospec,
                   pl.BlockSpec((seqs, 1, w), lambda b, c: (b, 0, 0)),
                   pl.BlockSpec((seqs, C_CONV - 1, w), lambda b, c: (b, 0, 0))],
        out_shape=[jax.ShapeDtypeStruct((proj.shape[0], w), F32),
                   jax.ShapeDtypeStruct((nb, 1, w), F32),
                   jax.ShapeDtypeStruct((nb, C_CONV - 1, w), F32)],
        scratch_shapes=[pltpu.VMEM((seqs, SUBLANES + L, w), F32), pltpu.VMEM((seqs, SUBLANES, w), F32)],
        input_output_aliases=alias,
        compiler_params=_cparams(("parallel", "arbitrary")),
        name="rglru",
    )(proj, proj, h0, cv0, cw, cb, wa, ba, wi, bi, lam, *xarg)


def _outproj_kernel(o1_ref, o2_ref, w1_ref, w2_ref, x_ref, g_ref, b_ref, y_ref):
    mix = _bdot(o1_ref[...], w1_ref[...]) + _bdot(o2_ref[...], w2_ref[...])
    y_ref[...] = _layer_norm(DN_ALPHA * x_ref[...] + mix, g_ref[...], b_ref[...])


def _outproj_ln(o1, o2, w1, w2, x, g, b):
    n = x.shape[0]
    row = lambda a: pl.BlockSpec((TM, a.shape[1]), lambda i: (i, 0))
    full = lambda a: pl.BlockSpec(a.shape, lambda i: (0, 0))
    return pl.pallas_call(
        _outproj_kernel,
        grid=(n // TM,),
        in_specs=[row(o1), row(o2), full(w1), full(w2), row(x), full(g), full(b)],
        out_specs=row(x),
        out_shape=jax.ShapeDtypeStruct(x.shape, F32),
        compiler_params=_cparams(("parallel",)),
        name="outproj_ln",
    )(o1, o2, w1, w2, x, g, b)


ROUTER_E0 = N_GROUPS
MOE_EXPERTS_PER_CHUNK = 2


def _moe_kernel(x_ref, wr_ref, br_ref, w1_ref, w3_ref, w2_ref, g_ref, b_ref, y_ref):
    x = x_ref[...]
    xb = x.astype(BF16)
    xl = (x - xb.astype(F32)).astype(BF16)
    wr = wr_ref[...]
    wh = wr.astype(BF16)
    wl = (wr - wh.astype(F32)).astype(BF16)
    dot = lambda a, b: jnp.dot(a, b, preferred_element_type=F32)
    logits = dot(xb, wh) + (dot(xb, wl) + dot(xl, wh)) + br_ref[...]
    lane = lax.broadcasted_iota(jnp.int32, logits.shape, 1)
    gl = jnp.where(lane < N_GROUPS, logits, -jnp.inf)
    gmax = jnp.max(gl, axis=1, keepdims=True)
    g_gate = 1.0 / jnp.sum(jnp.exp(gl - gmax), axis=1, keepdims=True)
    g_idx = jnp.min(jnp.where(gl == gmax, lane, LANES), axis=1, keepdims=True)
    e_lo = ROUTER_E0 + g_idx * E_PER_GROUP
    el = jnp.where(jnp.logical_and(lane >= e_lo, lane < e_lo + E_PER_GROUP), logits, -jnp.inf)
    e1 = jnp.max(el, axis=1, keepdims=True)
    i1 = jnp.min(jnp.where(el == e1, lane, LANES), axis=1, keepdims=True)
    el2 = jnp.where(lane == i1, -jnp.inf, el)
    e2 = jnp.max(el2, axis=1, keepdims=True)
    i2 = jnp.min(jnp.where(el2 == e2, lane, LANES), axis=1, keepdims=True)
    r = jnp.exp(e2 - e1)
    gw1 = g_gate / (1.0 + r)
    gate = jnp.where(lane == i1, gw1, 0.0) + jnp.where(lane == i2, gw1 * r, 0.0)

    def up(c):
        es = range(c * MOE_EXPERTS_PER_CHUNK, (c + 1) * MOE_EXPERTS_PER_CHUNK)
        r1 = jnp.concatenate([w1_ref[0, e] for e in es], axis=1)
        r3 = jnp.concatenate([w3_ref[0, e] for e in es], axis=1)
        return dot(xb, r1), dot(xb, r3)

    n_chunks = N_EXPERTS // MOE_EXPERTS_PER_CHUNK
    parts = []
    nxt = up(0)
    for c in range(n_chunks):
        h1, h3 = nxt
        if c + 1 < n_chunks:
            nxt = up(c + 1)
        hg = h1 * _sigmoid(h1) * h3
        for ee in range(MOE_EXPERTS_PER_CHUNK):
            e = ROUTER_E0 + c * MOE_EXPERTS_PER_CHUNK + ee
            parts.append((hg[:, ee * D_EXPERT:(ee + 1) * D_EXPERT] * gate[:, e:e + 1]).astype(BF16))
    y = dot(jnp.concatenate(parts, axis=1), w2_ref[0])
    y_ref[...] = _layer_norm(DN_ALPHA * x + y, g_ref[...], b_ref[...])


def _moe_ln(x, wr, br, w1, w3, w2, g, b, layer):
    n = x.shape[0]
    row = pl.BlockSpec((TM, D_MODEL), lambda i: (i, 0))
    full = lambda a: pl.BlockSpec(a.shape, lambda i: (0, 0))
    resident = lambda a: pl.BlockSpec((1,) + a.shape[1:], lambda i: (layer,) + (0,) * (a.ndim - 1),
                                      pipeline_mode=pl.Buffered(1))
    return pl.pallas_call(
        _moe_kernel,
        grid=(n // TM,),
        in_specs=[row, full(wr), full(br), resident(w1), resident(w3), resident(w2), full(g), full(b)],
        out_specs=row,
        out_shape=jax.ShapeDtypeStruct(x.shape, F32),
        compiler_params=pltpu.CompilerParams(dimension_semantics=("parallel",),
                                             vmem_limit_bytes=MOE_VMEM_LIMIT),
        name="moe_ln",
    )(x, wr, br, w1, w3, w2, g, b)


def _positions():
    pos = jnp.zeros((N_TOK,), F32)
    main = (N_META + jnp.arange(SEQ, dtype=jnp.int32)).astype(F32)
    pos = pos.at[:N_MAIN].set(jnp.tile(main, BATCH))
    samp = (PAST_LEN + jnp.minimum(jnp.arange(S_ROWS, dtype=jnp.int32), DEC_SEQ - 1)).astype(F32)
    pos = pos.at[ROW_S:ROW_M].set(jnp.tile(samp, DEC_BATCH))
    pos = pos.at[ROW_M:ROW_M + N_META].set(jnp.arange(N_META, dtype=jnp.int32).astype(F32))
    return pos


def _rope_tables_mla(pos):
    half = A_ROPE // 2
    freqs = ROPE_THETA ** (-jnp.arange(half, dtype=F32) / half)
    ang = pos[:, None] * freqs
    cos, sin = jnp.cos(ang), jnp.sin(ang)
    n = pos.shape[0]
    zeros = lambda w: jnp.zeros((n, w), F32)
    rest = LANES - ROPE_LANE0 - A_ROPE
    c = jnp.concatenate([jnp.ones((n, ROPE_LANE0), F32), cos, cos, zeros(rest)], axis=1)
    s1 = jnp.concatenate([zeros(ROPE_LANE0 + half), sin, zeros(rest)], axis=1)
    s2 = jnp.concatenate([zeros(ROPE_LANE0), -sin, zeros(half + rest)], axis=1)
    return c, s1, s2


def _rope_tables_ret(pos):
    half = D_DK // 2
    freqs = ROPE_THETA ** (-jnp.arange(half, dtype=F32) / half)
    ang = pos[:, None] * freqs
    cos, sin = jnp.cos(ang), jnp.sin(ang)
    return jnp.concatenate([cos, cos], axis=1), jnp.concatenate([-sin, sin], axis=1)


def _pad_cols(a, w):
    return jnp.pad(a, ((0, 0), (0, w - a.shape[1])))


def _even_in_weight(w):
    cuts = [0, A_QLORA, A_QLORA + A_KVLORA, A_QLORA + A_KVLORA + A_ROPE]
    c_q, c_kv, k_r = (w[:, cuts[i]:cuts[i + 1]] for i in range(3))
    m0 = cuts[3]
    mw = B_HEADS * B_DK
    big = w[:, m0:m0 + 4 * mw]
    gates = w[:, m0 + 4 * mw:]
    head = jnp.concatenate([c_q, c_kv, jnp.zeros((D_MODEL, ROPE_LANE0), w.dtype), k_r, gates], axis=1)
    return jnp.concatenate([_pad_cols(head, 1024), big], axis=1).astype(BF16)


def _head_pad_cols(a, width):
    k = a.shape[0]
    return jnp.pad(a, ((0, 0), (0, 0), (0, HEAD_PAD - width))).reshape(k, A_HEADS * HEAD_PAD)


def _block_diag(w):
    eye = jnp.eye(C_BLOCKS, dtype=w.dtype)
    return (w[:, :, None, :] * eye[:, None, :, None]).reshape(C_WIDTH, C_WIDTH)


def _assemble(main, samp, meta):
    w = main.shape[1]
    pad = jnp.zeros((N_TOK - ROW_M - N_META, w), main.dtype)
    return jnp.concatenate([main, samp, meta, pad], axis=0)


def kernel(x_prompt, x_sample, cache_mla, page_table, state_mlstm_c, state_mlstm_n, state_mlstm_m,
           state_rglru_h, state_rglru_conv, state_ret, meta_tokens, w_in_e, g_cq, g_ckv, w_uq, w_uk, w_uv,
           b_mi, b_mf, g_mlstm, w_out_e, w_in_o, conv_w, conv_b, w_ra, b_ra, w_ri, b_ri, lam, g_ret, w_out_o,
           ln_mix_g, ln_mix_b, ln_ffn_g, ln_ffn_b, w_gr, b_gr, w_er, b_er, w_e1, w_e3, w_e2):
    nc_main = SEQ // CHUNK
    xs = jnp.pad(x_sample, ((0, 0), (0, S_ROWS - DEC_SEQ), (0, 0))).reshape(N_SAMP, D_MODEL)
    x = _assemble(x_prompt.reshape(N_MAIN, D_MODEL), xs, meta_tokens)

    cache_t = jnp.swapaxes(cache_mla, 2, 3)
    w1_all, w3_all = w_e1.astype(BF16), w_e3.astype(BF16)
    w2_all = w_e2.astype(BF16).reshape(DEPTH, N_EXPERTS * D_EXPERT, D_MODEL)
    st_c = state_mlstm_c.reshape(N_EVEN * DEC_BATCH, B_HEADS, B_DK, B_DV)
    st_n = state_mlstm_n.reshape(N_EVEN * DEC_BATCH, B_HEADS, B_DK)
    st_m = state_mlstm_m.reshape(N_EVEN * DEC_BATCH, 1, B_HEADS)
    st_h = state_rglru_h.reshape(N_ODD * DEC_BATCH, 1, C_WIDTH)
    st_cv = state_rglru_conv.reshape(N_ODD * DEC_BATCH, C_CONV - 1, C_WIDTH)
    st_s = state_ret.reshape(N_ODD * DEC_BATCH, D_HEADS, D_DK, D_DV)

    pos = _positions()
    tabs_mla = _rope_tables_mla(pos)
    tabs_ret = _rope_tables_ret(pos)
    row2 = lambda a: a.reshape(1, -1)
    spare = {}

    def zbuf(w):
        return spare[w].pop() if spare.get(w) else jnp.zeros((N_TOK, w), F32)

    kv_p, kv_s, c_p, c_s, n_p, n_s, m_p, m_s = [], [], [], [], [], [], [], []
    h_p, h_s, cv_p, cv_s, s_p, s_s = [], [], [], [], [], []

    for l in range(DEPTH):
        j = l // 2
        if l % 2 == 0:
            proj = _matmul(x, _even_in_weight(w_in_e[j]))
            wuq = _head_pad_cols(w_uq[j].reshape(A_QLORA, A_HEADS, A_NOPE + A_ROPE), A_NOPE + A_ROPE).astype(BF16)
            wk = _head_pad_cols(w_uk[j], A_NOPE).astype(BF16)
            wv = _head_pad_cols(w_uv[j], A_V).astype(BF16)
            q, kp, vp, kvrow, kvt = _mla_prep(proj, tabs_mla, row2(g_cq[j]), row2(g_ckv[j]), wuq, wk, wv)
            wuk_t = jnp.transpose(w_uk[j], (1, 2, 0)).astype(BF16)
            wuv_p = jnp.pad(jnp.transpose(w_uv[j], (1, 0, 2)),
                            ((0, 0), (0, 0), (0, HEAD_PAD - A_V))).astype(BF16)
            o1 = _flash(q, kp, vp, zbuf(ATT_W), BATCH, SEQ, ROW_M)
            o1 = _sample_attn(page_table, q, kvrow, wuk_t, wuv_p, cache_t, o1, j)
            o1 = _meta_attn(q, kp, vp, o1, ROW_M)

            bi, bf, gn = row2(b_mi[j]), row2(b_mf[j]), row2(g_mlstm[j])
            zc = jnp.zeros((1, B_HEADS, B_DK, B_DV), F32)
            zn = jnp.zeros((1, B_HEADS, B_DK), F32)
            zm = jnp.zeros((1, 1, B_HEADS), F32)
            o2, c_m, n_m, m_m = _mlstm(proj, ROW_M, 1, 1, N_META, N_META, zc, zn, zm, bi, bf, gn, True,
                                       zbuf(B_HEADS * B_DV))
            o2, c_pm, n_pm, m_pm = _mlstm(proj, 0, BATCH, nc_main, CHUNK, CHUNK, c_m, n_m,
                                          m_m[:, :, :B_HEADS], bi, bf, gn, True, o2)
            o2, c_sm, n_sm, m_sm = _mlstm(proj, ROW_S, DEC_BATCH, 1, S_ROWS, DEC_SEQ, st_c, st_n, st_m,
                                          bi, bf, gn, False, o2, SEQS_PER_STEP, j * DEC_BATCH // SEQS_PER_STEP)

            w1 = jnp.pad(w_out_e[j][:A_HEADS * A_V].reshape(A_HEADS, A_V, D_MODEL),
                         ((0, 0), (0, HEAD_PAD - A_V), (0, 0))).reshape(ATT_W, D_MODEL).astype(BF16)
            w2 = w_out_e[j][A_HEADS * A_V:].astype(BF16)

            kvf = jnp.concatenate([kvt[:A_KVLORA], kvt[KV_ROPE0:KV_ROPE0 + A_ROPE]], axis=0)
            kv_main = kvf[:, :N_MAIN].reshape(A_LAT, BATCH, SEQ)
            kv_meta = jnp.broadcast_to(kvf[:, None, ROW_M:ROW_M + N_META], (A_LAT, BATCH, N_META))
            kv_p.append(jnp.transpose(jnp.concatenate([kv_meta, kv_main], axis=2), (1, 2, 0)))
            kv_sr = kvf[:, ROW_S:ROW_M].reshape(A_LAT, DEC_BATCH, S_ROWS)[:, :, :DEC_SEQ]
            kv_s.append(jnp.transpose(kv_sr, (1, 2, 0)))
            c_p.append(c_pm); n_p.append(n_pm); m_p.append(m_pm[:, 0, :B_HEADS])
            c_s.append(c_sm); n_s.append(n_sm); m_s.append(m_sm[:, 0, :B_HEADS])
        else:
            proj = _matmul(x, w_in_o[j].astype(BF16))
            cw, cb = conv_w[j], row2(conv_b[j])
            wa, wi = _block_diag(w_ra[j]).astype(BF16), _block_diag(w_ri[j]).astype(BF16)
            ba, bi_, lam_ = row2(b_ra[j]), row2(b_ri[j]), row2(lam[j])
            zh = jnp.zeros((1, 1, C_WIDTH), F32)
            zcv = jnp.zeros((1, C_CONV - 1, C_WIDTH), F32)
            args = (cw, cb, wa, ba, wi, bi_, lam_)
            o1, h_m, cv_m = _rglru(proj, ROW_M, 1, 1, N_META, N_META, zh, zcv, *args, True, zbuf(C_WIDTH))
            o1, h_pm, cv_pm = _rglru(proj, 0, BATCH, nc_main, CHUNK, CHUNK, h_m, cv_m, *args, True, o1)
            o1, h_sm, cv_sm = _rglru(proj, ROW_S, DEC_BATCH, 1, S_ROWS, DEC_SEQ, st_h, st_cv,
                                     *args, False, o1, SEQS_PER_STEP_RGLRU, j * DEC_BATCH // SEQS_PER_STEP_RGLRU)

            gn = row2(g_ret[j])
            zs = jnp.zeros((1, D_HEADS, D_DK, D_DV), F32)
            o2, s_m = _retention(proj, tabs_ret, ROW_M, 1, 1, N_META, N_META, zs, gn, True, zbuf(D_HEADS * D_DV))
            o2, s_pm = _retention(proj, tabs_ret, 0, BATCH, nc_main, CHUNK, CHUNK, s_m, gn, True, o2)
            o2, s_sm = _retention(proj, tabs_ret, ROW_S, DEC_BATCH, 1, S_ROWS, DEC_SEQ, st_s, gn, False, o2,
                                  SEQS_PER_STEP_RET, j * DEC_BATCH // SEQS_PER_STEP_RET)

            w1 = w_out_o[j][:C_WIDTH].astype(BF16)
            w2 = w_out_o[j][C_WIDTH:].astype(BF16)
            h_p.append(h_pm[:, 0]); cv_p.append(cv_pm); s_p.append(s_pm)
            h_s.append(h_sm[:, 0]); cv_s.append(cv_sm); s_s.append(s_sm)

        x = _outproj_ln(o1, o2, w1, w2, x, row2(ln_mix_g[l]), row2(ln_mix_b[l]))
        for dead in (o1, o2):
            spare.setdefault(dead.shape[1], []).append(dead)

        wr = jnp.concatenate([w_gr[l], jnp.transpose(w_er[l], (1, 0, 2)).reshape(D_MODEL, N_EXPERTS)], axis=1)
        br = jnp.concatenate([b_gr[l], b_er[l].reshape(N_EXPERTS)])
        x = _moe_ln(x, _pad_cols(wr, LANES), _pad_cols(row2(br), LANES), w1_all, w3_all, w2_all,
                    row2(ln_ffn_g[l]), row2(ln_ffn_b[l]), l)

    y_prompt = x[:N_MAIN].reshape(BATCH, SEQ, D_MODEL)
    y_sample = x[ROW_S:ROW_M].reshape(DEC_BATCH, S_ROWS, D_MODEL)[:, :DEC_SEQ]
    st = jnp.stack
    return (y_prompt, y_sample, st(kv_p), st(kv_s), st(c_p), st(c_s), st(n_p), st(n_s), st(m_p), st(m_s),
            st(h_p), st(h_s), st(cv_p), st(cv_s), st(s_p), st(s_s))
```
